```python
import math
import jax, jax.numpy as jnp
from jax import lax
import numpy as np

D_MODEL = 1024
BATCH = 8
SEQ = 2048
DEPTH = 4
DEC_BATCH = 128
DEC_SEQ = 8
PAST_LEN = 16384
PAGE_SIZE = 128

N_A_LAYERS = (DEPTH + 1) // 2
N_C_LAYERS = DEPTH // 2
A_HEADS = 4
A_DK = 128
A_DV = 128
A_CONV = 4
A_CHUNK = 64
A_QK = A_HEADS * A_DK
A_V = A_HEADS * A_DV
A_CONV_CH = 2 * A_QK + A_V
B_GROUPS = 4
B_GW = 128
B_WINDOWS = (2, 4, 8, 16)
B_HIST = max(B_WINDOWS) - 1
B_WIDTH = B_GROUPS * B_GW
EVEN_IN = A_CONV_CH + A_V + 2 * A_HEADS + B_WIDTH
EVEN_MIX = A_V + B_WIDTH
C_WIDTH = D_MODEL
C_HEADS = 4
C_HW = C_WIDTH // C_HEADS
C_CONV = 4
C_GATE = 8.0
ODD_IN = 2 * C_WIDTH
N_EXPERTS = 16
N_GROUPS = 4
EPG = N_EXPERTS // N_GROUPS
TOP_GROUPS = 1
TOP_K = 2
D_EXPERT = 512
ALPHA = (2 * DEPTH) ** 0.25
BETA_INIT = (8 * DEPTH) ** -0.25
LN_EPS = 1e-5
RMS_EPS = 1e-6
F32 = jnp.float32

kernel_name = 'hybrid_deltanet_pool_rglru_moe_step'


def layer_norm(x, g, b):
    xf = x.astype(F32)
    mu = xf.mean(-1, keepdims=True)
    var = jnp.square(xf - mu).mean(-1, keepdims=True)
    return ((xf - mu) * lax.rsqrt(var + LN_EPS) * g + b).astype(x.dtype)


def l2norm(t):
    tf = t.astype(F32)
    return tf * lax.rsqrt(jnp.sum(tf * tf, -1, keepdims=True) + 1e-6)


def causal_conv(u, hist, w, b=None):
    T = u.shape[1]
    ext = jnp.concatenate([hist.astype(u.dtype), u], axis=1)
    out = sum(ext[:, j:j + T] * w[j] for j in range(w.shape[0]))
    if b is not None:
        out = out + b
    return out, ext[:, T:]


def gated_delta_rule(q, k, v, g, beta, s0):
    bsz, T, H, dk = q.shape
    dv = v.shape[-1]
    C = min(A_CHUNK, T)
    n = -(-T // C)
    pad = n * C - T

    def blocks(t):
        t = jnp.pad(t.astype(F32), [(0, 0), (0, pad)] + [(0, 0)] * (t.ndim - 2))
        t = t.reshape((bsz, n, C) + t.shape[2:])
        return jnp.moveaxis(t, 3, 1)

    q, k, v, g, beta = blocks(q), blocks(k), blocks(v), blocks(g), blocks(beta)
    q = q * dk ** -0.5
    gc = jnp.cumsum(g, axis=-1)
    incl = jnp.tril(jnp.ones((C, C), bool))
    strict = jnp.tril(jnp.ones((C, C), bool), -1)
    diff = gc[..., :, None] - gc[..., None, :]
    decay = jnp.where(incl, jnp.exp(jnp.where(incl, diff, 0.0)), 0.0)
    kb = k * beta[..., None]
    m = jnp.where(strict, jnp.einsum('bhnid,bhnjd->bhnij', kb, k) * decay, 0.0)
    neg = -m
    tinv = jnp.eye(C, dtype=F32) + neg
    power = neg
    for _ in range(max((C - 1).bit_length() - 1, 0)):
        power = power @ power
        tinv = tinv + tinv @ power
    u_in = tinv @ (v * beta[..., None])
    w_in = tinv @ (kb * jnp.exp(gc)[..., None])
    a_intra = jnp.where(incl, jnp.einsum('bhnid,bhnjd->bhnij', q, k) * decay, 0.0)
    q_dec = q * jnp.exp(gc)[..., None]
    g_last = gc[..., -1]
    k_dec = k * jnp.exp(g_last[..., None] - gc)[..., None]

    def step(s, xs):
        u_c, w_c, a_c, qd_c, kd_c, gl_c = xs
        u_new = u_c - w_c @ s
        o = qd_c @ s + a_c @ u_new
        s = s * jnp.exp(gl_c)[..., None, None] + jnp.swapaxes(kd_c, -1, -2) @ u_new
        return s, o

    xs = tuple(jnp.moveaxis(t, 2, 0) for t in (u_in, w_in, a_intra, q_dec, k_dec, g_last))
    s_fin, o = lax.scan(step, s0.astype(F32), xs)
    o = jnp.moveaxis(o, 0, 2).reshape(bsz, H, n * C, dv)[:, :, :T]
    return jnp.swapaxes(o, 1, 2), s_fin


def multiscale_pool(u, hist, hist_valid):
    bsz, T, _ = u.shape
    ext = jnp.concatenate([hist.astype(u.dtype), u], axis=1)
    extf = ext.astype(F32)
    cs = jnp.concatenate([jnp.zeros((bsz, 1, B_WIDTH), F32), jnp.cumsum(extf, axis=1)], axis=1)
    end = cs[:, B_HIST + 1:]
    cur = extf[:, B_HIST:]
    pos = jnp.arange(T) + 1 + hist_valid
    parts = []
    for gi, w in enumerate(B_WINDOWS):
        lo, hi = gi * B_GW, (gi + 1) * B_GW
        start = cs[:, B_HIST + 1 - w:B_HIST + 1 - w + T, lo:hi]
        cnt = jnp.minimum(pos, w).astype(F32)[None, :, None]
        parts.append((end[..., lo:hi] - start) / cnt - cur[..., lo:hi])
    return jnp.concatenate(parts, axis=-1), ext[:, T:]


def delta_pool_mixer(x, conv_hist, s0, pool_hist, hist_valid, w_in, conv_w, log_decay, dt_bias, norm_w, w_group, scale, w_out):
    bsz, T, _ = x.shape
    proj = x @ w_in
    c0 = A_CONV_CH
    c1 = c0 + A_V
    c2 = c1 + A_HEADS
    c3 = c2 + A_HEADS
    qkv, z, b_raw, a_raw, u = jnp.split(proj, [c0, c1, c2, c3], axis=-1)
    qkv, conv_new = causal_conv(qkv, conv_hist, conv_w)
    qkv = jax.nn.silu(qkv)
    q, k, v = jnp.split(qkv, [A_QK, 2 * A_QK], axis=-1)
    q = l2norm(q.reshape(bsz, T, A_HEADS, A_DK))
    k = l2norm(k.reshape(bsz, T, A_HEADS, A_DK))
    v = v.reshape(bsz, T, A_HEADS, A_DV)
    beta = jax.nn.sigmoid(b_raw.astype(F32))
    g = -jnp.exp(log_decay.astype(F32)) * jax.nn.softplus(a_raw.astype(F32) + dt_bias.astype(F32))
    o, s_new = gated_delta_rule(q, k, v, g, beta, s0)
    o = o * lax.rsqrt(jnp.mean(o * o, -1, keepdims=True) + RMS_EPS) * norm_w.astype(F32)
    o = o * jax.nn.silu(z.astype(F32)).reshape(bsz, T, A_HEADS, A_DV)
    o_a = o.reshape(bsz, T, A_V).astype(x.dtype)
    pooled, pool_new = multiscale_pool(u, pool_hist, hist_valid)
    o_b = jnp.einsum('btgc,gcd->btgd', pooled.reshape(bsz, T, B_GROUPS, B_GW), w_group.astype(F32))
    o_b = (o_b.reshape(bsz, T, B_WIDTH) * scale.astype(F32)).astype(x.dtype)
    y = jnp.concatenate([o_a, o_b], axis=-1) @ w_out
    return y, conv_new.astype(conv_hist.dtype), s_new.astype(s0.dtype), pool_new.astype(pool_hist.dtype)


def rglru_mixer(x, conv_hist, h0, w_in, conv_w, conv_b, w_r, b_r, w_i, b_i, lam, w_out):
    bsz, T, _ = x.shape
    proj = x @ w_in
    y_br, x_br = jnp.split(proj, 2, axis=-1)
    y_br = jax.nn.gelu(y_br, approximate=True)
    xc, conv_new = causal_conv(x_br, conv_hist, conv_w, conv_b)
    xh = xc.reshape(bsz, T, C_HEADS, C_HW)
    r = jax.nn.sigmoid((jnp.einsum('bthi,hij->bthj', xh, w_r).reshape(bsz, T, C_WIDTH) + b_r).astype(F32))
    gi = jax.nn.sigmoid((jnp.einsum('bthi,hij->bthj', xh, w_i).reshape(bsz, T, C_WIDTH) + b_i).astype(F32))
    log_a = -C_GATE * r * jax.nn.softplus(-lam.astype(F32))
    a = jnp.exp(log_a)
    bt = jnp.sqrt(-jnp.expm1(2.0 * log_a)) * gi * xc.astype(F32)

    def step(h, ab):
        a_t, b_t = ab
        h = a_t * h + b_t
        return h, h

    h_last, hs = lax.scan(step, h0.astype(F32), (jnp.swapaxes(a, 0, 1), jnp.swapaxes(bt, 0, 1)))
    hs = jnp.swapaxes(hs, 0, 1).astype(x.dtype)
    y = (hs * y_br) @ w_out
    return y, conv_new.astype(conv_hist.dtype), h_last.astype(h0.dtype)


def moe(x, router_w, router_bias, w_gate, w_up, w_down):
    shp = x.shape
    xt = x.reshape(-1, D_MODEL)
    n = xt.shape[0]
    scores = jax.nn.sigmoid(xt.astype(F32) @ router_w.astype(F32))
    biased = scores + router_bias.astype(F32)
    grp_score = lax.top_k(biased.reshape(n, N_GROUPS, EPG), TOP_K)[0].sum(-1)
    _, g_idx = lax.top_k(grp_score, TOP_GROUPS)
    g_mask = jax.nn.one_hot(g_idx, N_GROUPS, dtype=F32).sum(-2)
    e_mask = jnp.repeat(g_mask, EPG, axis=-1) > 0
    _, e_idx = lax.top_k(jnp.where(e_mask, biased, -jnp.inf), TOP_K)
    w_sel = jnp.take_along_axis(scores, e_idx, axis=-1)
    w_sel = w_sel / jnp.sum(w_sel, -1, keepdims=True)
    combine = jnp.einsum('nk,nke->ne', w_sel, jax.nn.one_hot(e_idx, N_EXPERTS, dtype=F32)).astype(x.dtype)
    y = jnp.zeros_like(xt)
    for e in range(N_EXPERTS):
        h = jax.nn.silu(xt @ w_gate[e]) * (xt @ w_up[e])
        y = y + combine[:, e:e + 1] * (h @ w_down[e])
    return y.reshape(shp)


def trunk(x, states, hist_valid, weights):
    st_delta, st_dconv, st_pool, st_lru, st_lconv = states
    (ab_w_in, a_conv_w, a_log_decay, a_dt_bias, a_norm_w, b_w_group, b_scale, ab_w_out,
     c_w_in, c_conv_w, c_conv_b, c_w_r, c_b_r, c_w_i, c_b_i, c_lambda, c_w_out,
     ln_mix_g, ln_mix_b, ln_ffn_g, ln_ffn_b, router_w, router_bias, moe_w_gate, moe_w_up, moe_w_down) = weights
    n_delta, n_dconv, n_pool, n_lru, n_lconv = [], [], [], [], []
    for layer in range(DEPTH):
        j = layer // 2
        if layer % 2 == 0:
            h, dconv, delta, pool = delta_pool_mixer(
                x, st_dconv[j], st_delta[j], st_pool[j], hist_valid, ab_w_in[j], a_conv_w[j],
                a_log_decay[j], a_dt_bias[j], a_norm_w[j], b_w_group[j], b_scale[j], ab_w_out[j])
            n_delta.append(delta)
            n_dconv.append(dconv)
            n_pool.append(pool)
        else:
            h, lconv, lru = rglru_mixer(
                x, st_lconv[j], st_lru[j], c_w_in[j], c_conv_w[j], c_conv_b[j], c_w_r[j], c_b_r[j],
                c_w_i[j], c_b_i[j], c_lambda[j], c_w_out[j])
            n_lru.append(lru)
            n_lconv.append(lconv)
        x = layer_norm(ALPHA * x + h, ln_mix_g[layer], ln_mix_b[layer])
        f = moe(x, router_w, router_bias, moe_w_gate[layer], moe_w_up[layer], moe_w_down[layer])
        x = layer_norm(ALPHA * x + f, ln_ffn_g[layer], ln_ffn_b[layer])
    return (x, jnp.stack(n_delta), jnp.stack(n_dconv), jnp.stack(n_pool), jnp.stack(n_lru), jnp.stack(n_lconv))


def setup_inputs(seed: int = 0) -> dict:
    key = jax.random.key(seed)
    ks = iter(jax.random.split(key, 48))

    def nrm(shape, s):
        return jax.random.normal(next(ks), shape, F32) * s

    def gain(shape):
        return 1.0 + nrm(shape, 0.02)

    dt = jnp.exp(jax.random.uniform(next(ks), (N_A_LAYERS, A_HEADS), F32, math.log(1e-3), math.log(0.1)))
    a_base = jax.random.uniform(next(ks), (N_C_LAYERS, C_WIDTH), F32, 0.81, 0.998) ** (1.0 / C_GATE)
    return {
        'x_prompt': nrm((BATCH, SEQ, D_MODEL), 1.0),
        'x_sample': nrm((DEC_BATCH, DEC_SEQ, D_MODEL), 1.0),
        'state_delta': nrm((N_A_LAYERS, DEC_BATCH, A_HEADS, A_DK, A_DV), 0.2),
        'state_delta_conv': nrm((N_A_LAYERS, DEC_BATCH, A_CONV - 1, A_CONV_CH), 1.0),
        'state_pool': nrm((N_A_LAYERS, DEC_BATCH, B_HIST, B_WIDTH), 1.0),
        'state_lru': nrm((N_C_LAYERS, DEC_BATCH, C_WIDTH), 0.3),
        'state_lru_conv': nrm((N_C_LAYERS, DEC_BATCH, C_CONV - 1, C_WIDTH), 1.0),
        'ab_w_in': nrm((N_A_LAYERS, D_MODEL, EVEN_IN), D_MODEL ** -0.5),
        'a_conv_w': nrm((N_A_LAYERS, A_CONV, A_CONV_CH), A_CONV ** -0.5),
        'a_log_decay': jnp.log(jax.random.uniform(next(ks), (N_A_LAYERS, A_HEADS), F32, 1.0, 16.0)),
        'a_dt_bias': dt + jnp.log(-jnp.expm1(-dt)),
        'a_norm_w': gain((N_A_LAYERS, A_DV)),
        'b_w_group': nrm((N_A_LAYERS, B_GROUPS, B_GW, B_GW), B_GW ** -0.5),
        'b_scale': gain((N_A_LAYERS, B_WIDTH)),
        'ab_w_out': nrm((N_A_LAYERS, EVEN_MIX, D_MODEL), EVEN_MIX ** -0.5 * BETA_INIT),
        'c_w_in': nrm((N_C_LAYERS, D_MODEL, ODD_IN), D_MODEL ** -0.5),
        'c_conv_w': nrm((N_C_LAYERS, C_CONV, C_WIDTH), C_CONV ** -0.5),
        'c_conv_b': nrm((N_C_LAYERS, C_WIDTH), 0.02),
        'c_w_r': nrm((N_C_LAYERS, C_HEADS, C_HW, C_HW), C_HW ** -0.5),
        'c_b_r': nrm((N_C_LAYERS, C_WIDTH), 0.02),
        'c_w_i': nrm((N_C_LAYERS, C_HEADS, C_HW, C_HW), C_HW ** -0.5),
        'c_b_i': nrm((N_C_LAYERS, C_WIDTH), 0.02),
        'c_lambda': jnp.log(a_base) - jnp.log1p(-a_base),
        'c_w_out': nrm((N_C_LAYERS, C_WIDTH, D_MODEL), C_WIDTH ** -0.5 * BETA_INIT),
        'ln_mix_g': gain((DEPTH, D_MODEL)),
        'ln_mix_b': nrm((DEPTH, D_MODEL), 0.02),
        'ln_ffn_g': gain((DEPTH, D_MODEL)),
        'ln_ffn_b': nrm((DEPTH, D_MODEL), 0.02),
        'router_w': nrm((D_MODEL, N_EXPERTS), D_MODEL ** -0.5),
        'router_bias': nrm((N_EXPERTS,), 0.01),
        'moe_w_gate': nrm((DEPTH, N_EXPERTS, D_MODEL, D_EXPERT), D_MODEL ** -0.5),
        'moe_w_up': nrm((DEPTH, N_EXPERTS, D_MODEL, D_EXPERT), D_MODEL ** -0.5),
        'moe_w_down': nrm((DEPTH, N_EXPERTS, D_EXPERT, D_MODEL), D_EXPERT ** -0.5 * BETA_INIT),
    }


def reference(x_prompt, x_sample, state_delta, state_delta_conv, state_pool, state_lru, state_lru_conv,
              ab_w_in, a_conv_w, a_log_decay, a_dt_bias, a_norm_w, b_w_group, b_scale, ab_w_out,
              c_w_in, c_conv_w, c_conv_b, c_w_r, c_b_r, c_w_i, c_b_i, c_lambda, c_w_out,
              ln_mix_g, ln_mix_b, ln_ffn_g, ln_ffn_b, router_w, router_bias, moe_w_gate, moe_w_up, moe_w_down):
    weights = (ab_w_in, a_conv_w, a_log_decay, a_dt_bias, a_norm_w, b_w_group, b_scale, ab_w_out,
               c_w_in, c_conv_w, c_conv_b, c_w_r, c_b_r, c_w_i, c_b_i, c_lambda, c_w_out,
               ln_mix_g, ln_mix_b, ln_ffn_g, ln_ffn_b, router_w, router_bias, moe_w_gate, moe_w_up, moe_w_down)
    bp = x_prompt.shape[0]
    zero_states = (
        jnp.zeros((N_A_LAYERS, bp) + state_delta.shape[2:], state_delta.dtype),
        jnp.zeros((N_A_LAYERS, bp) + state_delta_conv.shape[2:], state_delta_conv.dtype),
        jnp.zeros((N_A_LAYERS, bp) + state_pool.shape[2:], state_pool.dtype),
        jnp.zeros((N_C_LAYERS, bp) + state_lru.shape[2:], state_lru.dtype),
        jnp.zeros((N_C_LAYERS, bp) + state_lru_conv.shape[2:], state_lru_conv.dtype),
    )
    y_prompt, p_delta, p_dconv, p_pool, p_lru, p_lconv = trunk(x_prompt, zero_states, 0, weights)
    sample_states = (state_delta, state_delta_conv, state_pool, state_lru, state_lru_conv)
    y_sample, s_delta, s_dconv, s_pool, s_lru, s_lconv = trunk(x_sample, sample_states, B_HIST, weights)
    return (y_prompt, y_sample, p_delta, p_dconv, p_pool, p_lru, p_lconv, s_delta, s_dconv, s_pool, s_lru, s_lconv)
```

```python
import functools

import jax
import jax.numpy as jnp
from jax import lax
from jax.experimental import pallas as pl
from jax.experimental.pallas import tpu as pltpu

F32 = jnp.float32
I32 = jnp.int32
MM = jnp.bfloat16
HIGHEST = lax.Precision.HIGHEST

D_MODEL = 1024
DEPTH = 4
A_HEADS = 4
A_DK = 128
A_DV = 128
A_QK = A_HEADS * A_DK
A_V = A_HEADS * A_DV
A_CONV_CH = 2 * A_QK + A_V
A_CHUNK = 64
B_GROUPS = 4
B_GW = 128
B_WINDOWS = (2, 4, 8, 16)
B_HIST = 15
B_WIDTH = B_GROUPS * B_GW
C_WIDTH = D_MODEL
C_HEADS = 4
C_HW = C_WIDTH // C_HEADS
C_GATE = 8.0
N_EXPERTS = 16
N_GROUPS = 4
EPG = N_EXPERTS // N_GROUPS
D_EXPERT = 512
ALPHA = (2 * DEPTH) ** 0.25
LN_EPS = 1e-5
RMS_EPS = 1e-6

LANES = 128
SUBLANES = 8
VMEM_LIMIT = 56 * 1024 * 1024

EV_Z0 = A_CONV_CH
EV_U0 = EV_Z0 + A_V
EV_BA0 = EV_U0 + B_WIDTH
EV_W = EV_BA0 + LANES

PAIRS = ((0, 1), (0, 2), (0, 3), (1, 2), (1, 3), (2, 3))
N_CLASSES = N_GROUPS * len(PAIRS)

TOKEN_TILE = 256


def _cparams(sem):
    return pltpu.CompilerParams(dimension_semantics=sem, vmem_limit_bytes=VMEM_LIMIT)


def _dot(a, b):
    return jnp.dot(a.astype(MM), b.astype(MM), preferred_element_type=F32)


def _dot_nt(a, b):
    return lax.dot_general(a.astype(MM), b.astype(MM), (((1,), (1,)), ((), ())),
                           preferred_element_type=F32)


def _dot_tn(a, b):
    return lax.dot_general(a.astype(MM), b.astype(MM), (((0,), (0,)), ((), ())),
                           preferred_element_type=F32)


def _sigmoid(x):
    return 1.0 / (1.0 + jnp.exp(-x))


def _silu(x):
    return x * _sigmoid(x)


def _softplus(x):
    return jnp.maximum(x, 0.0) + jnp.log1p(jnp.exp(-jnp.abs(x)))


def _gelu_tanh(x):
    return x * (0.5 * (1.0 + jnp.tanh(0.7978845608028654 * (x + 0.044715 * (x * x * x)))))


def _layer_norm(v, g, b):
    mu = jnp.mean(v, axis=-1, keepdims=True)
    d = v - mu
    var = jnp.mean(d * d, axis=-1, keepdims=True)
    return d * lax.rsqrt(var + LN_EPS) * g + b


def _proj_kernel(x_ref, w_ref, o_ref):
    o_ref[...] = jnp.dot(x_ref[...].astype(MM), w_ref[...], preferred_element_type=F32)


def _proj(x, w_mm, row0, nrows, tm):
    k, width = w_mm.shape
    off = row0 // tm
    return pl.pallas_call(
        _proj_kernel,
        grid=(nrows // tm,),
        in_specs=[pl.BlockSpec((tm, k), lambda i: (i + off, 0)),
                  pl.BlockSpec((k, width), lambda i: (0, 0))],
        out_specs=pl.BlockSpec((tm, width), lambda i: (i, 0)),
        out_shape=jax.ShapeDtypeStruct((nrows, width), F32),
        compiler_params=_cparams(("parallel",)),
        name="proj",
    )(x, w_mm)


def _even_kernel(p_ref, convh_ref, s0_ref, poolh_ref, convw_ref, gvec_ref, normw_ref,
                 wgrp_ref, scale_ref, o_ref, snew_ref, ext_scr, qkv_scr, pext_scr, s_scr,
                 *, nb_blk, rows, chunk, hist_valid, carry_hist):
    i = pl.program_id(1)
    n_chunks = rows // chunk
    n_neumann = max((chunk - 1).bit_length() - 1, 0)
    hb = 2 * SUBLANES

    @pl.when(i == 0)
    def _():
        s_scr[...] = s0_ref[...]
        ext_scr[:, 0:SUBLANES, :] = convh_ref[...]
        pext_scr[:, 0:hb, :] = poolh_ref[...]

    rid = lax.broadcasted_iota(I32, (chunk, chunk), 0)
    cid = lax.broadcasted_iota(I32, (chunk, chunk), 1)
    incl = rid >= cid
    strict = rid > cid
    eye = (rid == cid).astype(F32)
    ltri = incl.astype(F32)
    lane = lax.broadcasted_iota(I32, (chunk, LANES), 1)
    trow = lax.broadcasted_iota(I32, (rows, LANES), 0)
    pos = (i * rows + trow + (1 + hist_valid)).astype(F32)
    neg_decay_rate = -jnp.exp(gvec_ref[0:1, :])
    dt_bias = gvec_ref[1:2, :]

    for nb in range(nb_blk):
        r0 = nb * rows
        ext_scr[nb, SUBLANES:SUBLANES + rows, :] = p_ref[r0:r0 + rows, 0:A_CONV_CH]
        for ct in range(A_CONV_CH // LANES):
            cs = slice(ct * LANES, (ct + 1) * LANES)
            acc = None
            for j in range(4):
                lo = SUBLANES - 3 + j
                term = ext_scr[nb, lo:lo + rows, cs] * convw_ref[j:j + 1, cs]
                acc = term if acc is None else acc + term
            t = _silu(acc)
            if ct < 2 * A_HEADS:
                t = t * lax.rsqrt(jnp.sum(t * t, axis=-1, keepdims=True) + 1e-6)
                if ct < A_HEADS:
                    t = t * (A_DK ** -0.5)
            qkv_scr[r0:r0 + rows, cs] = t
        if carry_hist:
            ext_scr[nb, 0:SUBLANES, :] = ext_scr[nb, rows:rows + SUBLANES, :]

        for c in range(n_chunks):
            rs = slice(r0 + c * chunk, r0 + (c + 1) * chunk)
            ba = p_ref[rs, EV_BA0:EV_BA0 + LANES]
            beta_full = _sigmoid(ba)
            g_full = neg_decay_rate * _softplus(ba + dt_bias)
            gc_full = jnp.dot(ltri, g_full, precision=HIGHEST, preferred_element_type=F32)
            eg_full = jnp.exp(gc_full)
            gl_full = gc_full[chunk - 1:chunk, :]
            ekd_full = jnp.exp(gl_full - gc_full)
            egl_full = jnp.exp(gl_full)
            for h in range(A_HEADS):
                hs = slice(h * LANES, (h + 1) * LANES)
                gl = A_HEADS + h
                beta = beta_full[:, h:h + 1]
                gc = gc_full[:, gl:gl + 1]
                sel = (lane == gl).astype(F32)
                grow = lax.dot_general(sel, gc_full, (((1,), (1,)), ((), ())),
                                       precision=HIGHEST, preferred_element_type=F32)
                decay = jnp.where(incl, jnp.exp(jnp.where(incl, gc - grow, 0.0)), 0.0)
                q = qkv_scr[rs, hs]
                k = qkv_scr[rs, A_QK + h * LANES:A_QK + (h + 1) * LANES]
                v = qkv_scr[rs, 2 * A_QK + h * LANES:2 * A_QK + (h + 1) * LANES]
                kb = k * beta
                m = jnp.where(strict, _dot_nt(kb, k) * decay, 0.0)
                neg = -m
                tinv = eye + neg
                power = neg
                for _ in range(n_neumann):
                    power = _dot(power, power)
                    tinv = tinv + _dot(tinv, power)
                u_in = _dot(tinv, v * beta)
                w_in = _dot(tinv, kb * eg_full[:, gl:gl + 1])
                a_intra = jnp.where(incl, _dot_nt(q, k) * decay, 0.0)
                q_dec = q * eg_full[:, gl:gl + 1]
                k_dec = k * ekd_full[:, gl:gl + 1]
                s = s_scr[nb, h]
                u_new = u_in - _dot(w_in, s)
                o = _dot(q_dec, s) + _dot(a_intra, u_new)
                s_scr[nb, h] = s * egl_full[:, gl:gl + 1] + _dot_tn(k_dec, u_new)
                o = o * lax.rsqrt(jnp.mean(o * o, axis=-1, keepdims=True) + RMS_EPS) * normw_ref[...]
                z = p_ref[rs, EV_Z0 + h * LANES:EV_Z0 + (h + 1) * LANES]
                o_ref[rs, hs] = o * _silu(z)

        pext_scr[nb, hb:hb + rows, :] = p_ref[r0:r0 + rows, EV_U0:EV_U0 + B_WIDTH]
        for gi, w in enumerate(B_WINDOWS):
            cs = slice(gi * B_GW, (gi + 1) * B_GW)
            cur = pext_scr[nb, hb:hb + rows, cs]
            tot = cur
            for j in range(1, w):
                tot = tot + pext_scr[nb, hb - j:hb - j + rows, cs]
            pooled = tot / jnp.minimum(pos, float(w)) - cur
            ob = _dot(pooled, wgrp_ref[gi]) * scale_ref[:, cs]
            o_ref[r0:r0 + rows, A_V + gi * B_GW:A_V + (gi + 1) * B_GW] = ob
        if carry_hist:
            pext_scr[nb, 0:hb, :] = pext_scr[nb, rows:rows + hb, :]

    @pl.when(i == pl.num_programs(1) - 1)
    def _():
        snew_ref[...] = s_scr[...]


def _even_mixer(proj, convh8, s0, poolh16, convw, gvec, normw, wgrp_mm, scale,
                *, batch, seq, nb_blk, rows, hist_valid):
    chunk = min(A_CHUNK, seq)
    assert seq % rows == 0 and rows % chunk == 0 and batch % nb_blk == 0
    assert nb_blk == 1 or rows == seq
    t_blocks = seq // rows
    blk = nb_blk * rows
    hb = 2 * SUBLANES
    kern = functools.partial(_even_kernel, nb_blk=nb_blk, rows=rows, chunk=chunk,
                             hist_valid=hist_valid, carry_hist=t_blocks > 1)
    full = lambda shape: pl.BlockSpec(shape, lambda b, i: (0,) * len(shape))
    return pl.pallas_call(
        kern,
        grid=(batch // nb_blk, t_blocks),
        in_specs=[
            pl.BlockSpec((blk, EV_W), lambda b, i: (b * t_blocks + i, 0)),
            pl.BlockSpec((nb_blk, SUBLANES, A_CONV_CH), lambda b, i: (b, 0, 0)),
            pl.BlockSpec((nb_blk, A_HEADS, A_DK, A_DV), lambda b, i: (b, 0, 0, 0)),
            pl.BlockSpec((nb_blk, hb, B_WIDTH), lambda b, i: (b, 0, 0)),
            full((4, A_CONV_CH)), full((2, LANES)), full((1, A_DV)),
            full((B_GROUPS, B_GW, B_GW)), full((1, B_WIDTH)),
        ],
        out_specs=[
            pl.BlockSpec((blk, D_MODEL), lambda b, i: (b * t_blocks + i, 0)),
            pl.BlockSpec((nb_blk, A_HEADS, A_DK, A_DV), lambda b, i: (b, 0, 0, 0)),
        ],
        out_shape=[jax.ShapeDtypeStruct((batch * seq, D_MODEL), F32),
                   jax.ShapeDtypeStruct((batch, A_HEADS, A_DK, A_DV), F32)],
        scratch_shapes=[
            pltpu.VMEM((nb_blk, SUBLANES + rows, A_CONV_CH), F32),
            pltpu.VMEM((blk, A_CONV_CH), F32),
            pltpu.VMEM((nb_blk, hb + rows, B_WIDTH), F32),
            pltpu.VMEM((nb_blk, A_HEADS, A_DK, A_DV), F32),
        ],
        compiler_params=_cparams(("parallel", "arbitrary")),
        name="even_mixer",
    )(proj, convh8, s0, poolh16, convw, gvec, normw, wgrp_mm, scale)


def _odd_kernel(p_ref, convh_ref, h0_ref, convw_ref, convb_ref, wr_ref, br_ref, wi_ref, bi_ref,
                lam_ref, o_ref, hlast_ref, ext_scr, a_scr, b_scr, hs_scr, h_scr,
                *, nb_blk, rows, carry_hist):
    i = pl.program_id(1)
    blk = nb_blk * rows

    @pl.when(i == 0)
    def _():
        h_scr[...] = h0_ref[...]
        ext_scr[:, 0:SUBLANES, :] = convh_ref[...]

    ext_scr[:, SUBLANES:SUBLANES + rows, :] = p_ref[:, :, C_WIDTH:2 * C_WIDTH]
    log_base = -C_GATE * _softplus(-lam_ref[...])
    for h in range(C_HEADS):
        cs = slice(h * C_HW, (h + 1) * C_HW)
        acc = None
        for j in range(4):
            lo = SUBLANES - 3 + j
            term = ext_scr[:, lo:lo + rows, cs] * convw_ref[j:j + 1, cs]
            acc = term if acc is None else acc + term
        xc = (acc + convb_ref[:, cs]).reshape(blk, C_HW)
        r = _sigmoid(_dot(xc, wr_ref[h]) + br_ref[:, cs])
        gi = _sigmoid(_dot(xc, wi_ref[h]) + bi_ref[:, cs])
        log_a = r * log_base[:, cs]
        a = jnp.exp(log_a)
        th = jnp.tanh(log_a)
        bt = jnp.sqrt(-2.0 * th / (1.0 - th)) * gi * xc
        for lt in range(C_HW // LANES):
            a_scr[h * (C_HW // LANES) + lt] = a[:, lt * LANES:(lt + 1) * LANES]
            b_scr[h * (C_HW // LANES) + lt] = bt[:, lt * LANES:(lt + 1) * LANES]
    if carry_hist:
        ext_scr[:, 0:SUBLANES, :] = ext_scr[:, rows:rows + SUBLANES, :]

    n_lt = C_WIDTH // LANES

    def step(t, hcur):
        hnew = []
        for lt in range(n_lt):
            a_t = a_scr[lt, pl.ds(t, nb_blk, stride=rows), :]
            b_t = b_scr[lt, pl.ds(t, nb_blk, stride=rows), :]
            hn = a_t * hcur[lt] + b_t
            hs_scr[lt, pl.ds(t, nb_blk, stride=rows), :] = hn
            hnew.append(hn)
        return tuple(hnew)

    h_init = tuple(h_scr[:, lt * LANES:(lt + 1) * LANES] for lt in range(n_lt))
    h_fin = lax.fori_loop(0, rows, step, h_init, unroll=SUBLANES)
    for lt in range(n_lt):
        cs = slice(lt * LANES, (lt + 1) * LANES)
        h_scr[:, cs] = h_fin[lt]
        y = _gelu_tanh(p_ref[:, :, cs])
        o_ref[:, :, cs] = y * hs_scr[lt].reshape(nb_blk, rows, LANES)

    @pl.when(i == pl.num_programs(1) - 1)
    def _():
        hlast_ref[...] = h_scr[...]


def _odd_mixer(proj3, convh8, h0, convw, convb, wr_mm, br, wi_mm, bi, lam, *, nb_blk, rows):
    batch, seq, _ = proj3.shape
    assert seq % rows == 0 and batch % nb_blk == 0 and rows % SUBLANES == 0
    t_blocks = seq // rows
    blk = nb_blk * rows
    kern = functools.partial(_odd_kernel, nb_blk=nb_blk, rows=rows, carry_hist=t_blocks > 1)
    full = lambda shape: pl.BlockSpec(shape, lambda b, i: (0,) * len(shape))
    return pl.pallas_call(
        kern,
        grid=(batch // nb_blk, t_blocks),
        in_specs=[
            pl.BlockSpec((nb_blk, rows, 2 * C_WIDTH), lambda b, i: (b, i, 0)),
            pl.BlockSpec((nb_blk, SUBLANES, C_WIDTH), lambda b, i: (b, 0, 0)),
            pl.BlockSpec((nb_blk, C_WIDTH), lambda b, i: (b, 0)),
            full((4, C_WIDTH)), full((1, C_WIDTH)),
            full((C_HEADS, C_HW, C_HW)), full((1, C_WIDTH)),
            full((C_HEADS, C_HW, C_HW)), full((1, C_WIDTH)),
            full((1, C_WIDTH)),
        ],
        out_specs=[
            pl.BlockSpec((nb_blk, rows, C_WIDTH), lambda b, i: (b, i, 0)),
            pl.BlockSpec((nb_blk, C_WIDTH), lambda b, i: (b, 0)),
        ],
        out_shape=[jax.ShapeDtypeStruct((batch, seq, C_WIDTH), F32),
                   jax.ShapeDtypeStruct((batch, C_WIDTH), F32)],
        scratch_shapes=[
            pltpu.VMEM((nb_blk, SUBLANES + rows, C_WIDTH), F32),
            pltpu.VMEM((C_WIDTH // LANES, blk, LANES), F32),
            pltpu.VMEM((C_WIDTH // LANES, blk, LANES), F32),
            pltpu.VMEM((C_WIDTH // LANES, blk, LANES), F32),
            pltpu.VMEM((nb_blk, C_WIDTH), F32),
        ],
        compiler_params=_cparams(("parallel", "arbitrary")),
        name="odd_mixer",
    )(proj3, convh8, h0, convw, convb, wr_mm, br, wi_mm, bi, lam)


def _route(x1, rwt_ref, rb_ref, cls_ref, wab_ref):
    logits = lax.dot_general(rwt_ref[...], x1, (((1,), (1,)), ((), ())),
                             precision=HIGHEST, preferred_element_type=F32)
    sc = _sigmoid(logits)
    bz = sc + rb_ref[...]
    row = lambda arr, e: arr[e:e + 1, :]
    best = None
    gidx = None
    for g in range(N_GROUPS):
        r = [row(bz, EPG * g + k) for k in range(EPG)]
        top2 = None
        for a, b in PAIRS:
            s = r[a] + r[b]
            top2 = s if top2 is None else jnp.maximum(top2, s)
        if best is None:
            best, gidx = top2, jnp.zeros(top2.shape, I32)
        else:
            upd = top2 > best
            gidx = jnp.where(upd, g, gidx)
            best = jnp.where(upd, top2, best)
    sb, ss = [], []
    for k in range(EPG):
        vb, vs = row(bz, k), row(sc, k)
        for g in range(1, N_GROUPS):
            vb = jnp.where(gidx == g, row(bz, EPG * g + k), vb)
            vs = jnp.where(gidx == g, row(sc, EPG * g + k), vs)
        sb.append(vb)
        ss.append(vs)
    m1, i1 = sb[0], jnp.zeros(sb[0].shape, I32)
    for k in range(1, EPG):
        upd = sb[k] > m1
        i1 = jnp.where(upd, k, i1)
        m1 = jnp.where(upd, sb[k], m1)
    m2, i2 = None, None
    for k in range(EPG):
        cand = jnp.where(i1 == k, -jnp.inf, sb[k])
        if m2 is None:
            m2, i2 = cand, jnp.zeros(cand.shape, I32)
        else:
            upd = cand > m2
            i2 = jnp.where(upd, k, i2)
            m2 = jnp.where(upd, cand, m2)
    lo = jnp.minimum(i1, i2)
    hi = jnp.maximum(i1, i2)
    pair = jnp.where(lo == 0, 0, jnp.where(lo == 1, 3, 5)) + hi - lo - 1
    s_lo, s_hi = ss[0], ss[0]
    for k in range(1, EPG):
        s_lo = jnp.where(lo == k, ss[k], s_lo)
        s_hi = jnp.where(hi == k, ss[k], s_hi)
    den = s_lo + s_hi
    cls_ref[...] = gidx * len(PAIRS) + pair
    wab_ref[0:1, :] = s_lo / den
    wab_ref[1:2, :] = s_hi / den


def _mix_out_kernel(ap_ref, as_ref, w_ref, x_ref, g_ref, b_ref, rwt_ref, rb_ref,
                    x1_ref, cls_ref, wab_ref, *, np_tiles):
    i = pl.program_id(0)

    def finish(a_ref):
        hmix = jnp.dot(a_ref[...].astype(MM), w_ref[...], preferred_element_type=F32)
        x1 = _layer_norm(ALPHA * x_ref[...] + hmix, g_ref[...], b_ref[...])
        x1_ref[...] = x1
        _route(x1, rwt_ref, rb_ref, cls_ref, wab_ref)

    @pl.when(i < np_tiles)
    def _():
        finish(ap_ref)

    @pl.when(i >= np_tiles)
    def _():
        finish(as_ref)


def _mix_out(a_p, a_s, w_mm, x, g, b, rwt, rb, tm):
    n = x.shape[0]
    np_tiles = a_p.shape[0] // tm
    ns_tiles = a_s.shape[0] // tm
    assert (np_tiles + ns_tiles) * tm == n
    kern = functools.partial(_mix_out_kernel, np_tiles=np_tiles)
    full = lambda shape: pl.BlockSpec(shape, lambda i: (0,) * len(shape))
    return pl.pallas_call(
        kern,
        grid=(np_tiles + ns_tiles,),
        in_specs=[
            pl.BlockSpec((tm, D_MODEL), lambda i: (jnp.minimum(i, np_tiles - 1), 0)),
            pl.BlockSpec((tm, D_MODEL), lambda i: (jnp.maximum(i - np_tiles, 0), 0)),
            full((D_MODEL, D_MODEL)),
            pl.BlockSpec((tm, D_MODEL), lambda i: (i, 0)),
            full((1, D_MODEL)), full((1, D_MODEL)),
            full((N_EXPERTS, D_MODEL)), full((N_EXPERTS, 1)),
        ],
        out_specs=[
            pl.BlockSpec((tm, D_MODEL), lambda i: (i, 0)),
            pl.BlockSpec((1, tm), lambda i: (0, i)),
            pl.BlockSpec((2, tm), lambda i: (0, i)),
        ],
        out_shape=[jax.ShapeDtypeStruct((n, D_MODEL), F32),
                   jax.ShapeDtypeStruct((1, n), I32),
                   jax.ShapeDtypeStruct((2, n), F32)],
        compiler_params=_cparams(("arbitrary",)),
        name="mix_out",
    )(a_p, a_s, w_mm, x, g, b, rwt, rb)


def _moe_kernel(nvalid_ref, chga_ref, chgb_ref, ea_ref, eb_ref, rowsrc_ref, rowdst_ref,
                x_hbm, wa_ref, wb_ref, wga_ref, wua_ref, wda_ref, wgb_ref, wub_ref, wdb_ref,
                y_hbm, xbuf, obuf, ga_scr, ua_scr, da_scr, gb_scr, ub_scr, db_scr, gsem, ssem,
                *, tm):
    t = pl.program_id(0)
    nt = pl.num_programs(0)
    slot = t % 2

    def gather_copy(tok, r, sl):
        return pltpu.make_async_copy(x_hbm.at[pl.ds(tok, 1)], xbuf.at[sl, pl.ds(r, 1)], gsem.at[sl])

    def scatter_copy(tok, r):
        return pltpu.make_async_copy(obuf.at[pl.ds(r, 1)], y_hbm.at[pl.ds(tok, 1)], ssem.at[0])

    def start_gather(tile, sl):
        def body(r, carry):
            gather_copy(rowsrc_ref[tile * tm + r], r, sl).start()
            return carry
        lax.fori_loop(0, tm, body, 0, unroll=8)

    def wait_scatter():
        pltpu.make_async_copy(obuf, y_hbm.at[pl.ds(0, tm)], ssem.at[0]).wait()

    @pl.when(t == 0)
    def _():
        start_gather(0, 0)
        obuf[...] = jnp.zeros(obuf.shape, obuf.dtype)
        spare = pltpu.make_async_copy(obuf, y_hbm.at[pl.ds(y_hbm.shape[0] - tm, tm)], ssem.at[0])
        spare.start()
        spare.wait()

    nxt = jnp.minimum(t + 1, nt - 1)

    @pl.when(jnp.logical_and(t + 1 < nt, nvalid_ref[nxt] > 0))
    def _():
        start_gather(t + 1, 1 - slot)

    @pl.when(chga_ref[t] == 1)
    def _():
        ga_scr[...] = wga_ref[0].astype(MM)
        ua_scr[...] = wua_ref[0].astype(MM)
        da_scr[...] = wda_ref[0].astype(MM)

    @pl.when(chgb_ref[t] == 1)
    def _():
        gb_scr[...] = wgb_ref[0].astype(MM)
        ub_scr[...] = wub_ref[0].astype(MM)
        db_scr[...] = wdb_ref[0].astype(MM)

    prev = jnp.maximum(t - 1, 0)
    n_prev = jnp.where(t > 0, nvalid_ref[prev], 0)
    n_cur = nvalid_ref[t]

    @pl.when(n_cur > 0)
    def _():
        pltpu.make_async_copy(x_hbm.at[pl.ds(0, tm)], xbuf.at[slot], gsem.at[slot]).wait()
        x = xbuf[slot].astype(MM)

        def expert(wg, wu, wd, wrow):
            gate = jnp.dot(x, wg[...], preferred_element_type=F32)
            up = jnp.dot(x, wu[...], preferred_element_type=F32)
            hid = (_silu(gate) * up).astype(MM)
            return jnp.dot(hid, wd[...], preferred_element_type=F32) * wrow

        y = expert(ga_scr, ua_scr, da_scr, wa_ref[...]) + expert(gb_scr, ub_scr, db_scr, wb_ref[...])

        @pl.when(n_prev > 0)
        def _():
            wait_scatter()

        obuf[...] = y

        def body(r, carry):
            scatter_copy(rowdst_ref[t * tm + r], r).start()
            return carry
        lax.fori_loop(0, tm, body, 0, unroll=8)

        @pl.when(t == nt - 1)
        def _():
            wait_scatter()

    @pl.when(jnp.logical_and(n_cur == 0, n_prev > 0))
    def _():
        wait_scatter()


def _moe(x1, cls, wab, w_gate, w_up, w_down, tm):
    n = x1.shape[0]
    n_tiles = n // tm + N_CLASSES
    n_rows = n_tiles * tm
    cls = cls.reshape(n)
    onehot = (cls[:, None] == jnp.arange(N_CLASSES, dtype=I32)[None, :]).astype(I32)
    csum = jnp.cumsum(onehot, axis=0)
    counts = csum[-1]
    rank = jnp.take_along_axis(csum, cls[:, None], axis=1)[:, 0] - 1
    tiles_c = (counts + tm - 1) // tm
    tile_end_c = jnp.cumsum(tiles_c)
    tile_start_c = tile_end_c - tiles_c
    total = tile_end_c[-1]
    pos = tile_start_c[cls] * tm + rank
    tok = jnp.arange(n, dtype=I32)
    row_src = jnp.zeros((n_rows,), I32).at[pos].set(tok)
    row_dst = (n + jnp.arange(n_rows, dtype=I32) % tm).at[pos].set(tok)
    wa_row = jnp.zeros((n_rows,), F32).at[pos].set(wab[0]).reshape(n_rows, 1)
    wb_row = jnp.zeros((n_rows,), F32).at[pos].set(wab[1]).reshape(n_rows, 1)
    tid = jnp.arange(n_tiles, dtype=I32)
    t_eff = jnp.minimum(tid, total - 1)
    tile_cls = jnp.minimum(jnp.searchsorted(tile_end_c, t_eff, side="right").astype(I32), N_CLASSES - 1)
    left = counts[tile_cls] - (tid - tile_start_c[tile_cls]) * tm
    nvalid = jnp.where(tid < total, jnp.clip(left, 0, tm), 0).astype(I32)
    pair_lo = jnp.array([p[0] for p in PAIRS], I32)
    pair_hi = jnp.array([p[1] for p in PAIRS], I32)
    ea = EPG * (tile_cls // len(PAIRS)) + pair_lo[tile_cls % len(PAIRS)]
    eb = EPG * (tile_cls // len(PAIRS)) + pair_hi[tile_cls % len(PAIRS)]
    first = tid == 0
    chga = jnp.logical_or(first, ea != jnp.roll(ea, 1)).astype(I32)
    chgb = jnp.logical_or(first, eb != jnp.roll(eb, 1)).astype(I32)

    kern = functools.partial(_moe_kernel, tm=tm)
    wspec_a = lambda shape: pl.BlockSpec(shape, lambda t, nv, ca, cb, ea_, eb_, rs, rd: (ea_[t], 0, 0))
    wspec_b = lambda shape: pl.BlockSpec(shape, lambda t, nv, ca, cb, ea_, eb_, rs, rd: (eb_[t], 0, 0))
    rowspec = pl.BlockSpec((tm, 1), lambda t, nv, ca, cb, ea_, eb_, rs, rd: (t, 0))
    gu = (1, D_MODEL, D_EXPERT)
    dn = (1, D_EXPERT, D_MODEL)
    grid_spec = pltpu.PrefetchScalarGridSpec(
        num_scalar_prefetch=7,
        grid=(n_tiles,),
        in_specs=[pl.BlockSpec(memory_space=pl.ANY), rowspec, rowspec,
                  wspec_a(gu), wspec_a(gu), wspec_a(dn),
                  wspec_b(gu), wspec_b(gu), wspec_b(dn)],
        out_specs=pl.BlockSpec(memory_space=pl.ANY),
        scratch_shapes=[
            pltpu.VMEM((2, tm, D_MODEL), F32),
            pltpu.VMEM((tm, D_MODEL), F32),
            pltpu.VMEM((D_MODEL, D_EXPERT), MM), pltpu.VMEM((D_MODEL, D_EXPERT), MM),
            pltpu.VMEM((D_EXPERT, D_MODEL), MM),
            pltpu.VMEM((D_MODEL, D_EXPERT), MM), pltpu.VMEM((D_MODEL, D_EXPERT), MM),
            pltpu.VMEM((D_EXPERT, D_MODEL), MM),
            pltpu.SemaphoreType.DMA((2,)),
            pltpu.SemaphoreType.DMA((1,)),
        ],
    )
    return pl.pallas_call(
        kern,
        grid_spec=grid_spec,
        out_shape=jax.ShapeDtypeStruct((n + tm, D_MODEL), F32),
        compiler_params=_cparams(("arbitrary",)),
        name="moe",
    )(nvalid, chga, chgb, ea, eb, row_src, row_dst,
      x1, wa_row, wb_row, w_gate, w_up, w_down, w_gate, w_up, w_down)


def _ffn_out_kernel(x_ref, y_ref, g_ref, b_ref, o_ref):
    o_ref[...] = _layer_norm(ALPHA * x_ref[...] + y_ref[...], g_ref[...], b_ref[...])


def _ffn_out(x1, y, g, b, row0, nrows, tm):
    off = row0 // tm
    rowspec = pl.BlockSpec((tm, D_MODEL), lambda i: (i + off, 0))
    vec = pl.BlockSpec((1, D_MODEL), lambda i: (0, 0))
    return pl.pallas_call(
        _ffn_out_kernel,
        grid=(nrows // tm,),
        in_specs=[rowspec, rowspec, vec, vec],
        out_specs=pl.BlockSpec((tm, D_MODEL), lambda i: (i, 0)),
        out_shape=jax.ShapeDtypeStruct((nrows, D_MODEL), F32),
        compiler_params=_cparams(("parallel",)),
        name="ffn_out",
    )(x1, y, g, b)


def _pad_hist(hist, rows):
    b, r, c = hist.shape
    return jnp.concatenate([jnp.zeros((b, rows - r, c), hist.dtype), hist], axis=1)


def _new_hist(hist, cur, keep):
    t = cur.shape[1]
    if t >= keep:
        return cur[:, t - keep:]
    return jnp.concatenate([hist[:, t:], cur], axis=1)


def kernel(x_prompt, x_sample, state_delta, state_delta_conv, state_pool, state_lru, state_lru_conv,
           ab_w_in, a_conv_w, a_log_decay, a_dt_bias, a_norm_w, b_w_group, b_scale, ab_w_out,
           c_w_in, c_conv_w, c_conv_b, c_w_r, c_b_r, c_w_i, c_b_i, c_lambda, c_w_out,
           ln_mix_g, ln_mix_b, ln_ffn_g, ln_ffn_b, router_w, router_bias,
           moe_w_gate, moe_w_up, moe_w_down):
    bp, tp, d = x_prompt.shape
    bs, ts, _ = x_sample.shape
    n_p, n_s = bp * tp, bs * ts
    n = n_p + n_s
    tm = TOKEN_TILE
    assert n_p % tm == 0 and n_s % tm == 0

    x = jnp.concatenate([x_prompt.reshape(n_p, d), x_sample.reshape(n_s, d)], axis=0)
    groups = (
        dict(row0=0, batch=bp, seq=tp, hist_valid=0, fresh=True,
             ev=dict(nb_blk=1, rows=min(tp, 256)), od=dict(nb_blk=min(bp, 8), rows=min(tp, 64))),
        dict(row0=n_p, batch=bs, seq=ts, hist_valid=B_HIST, fresh=False,
             ev=dict(nb_blk=min(bs, 8), rows=ts), od=dict(nb_blk=min(bs, 32), rows=ts)),
    )
    rwt = router_w.T
    rb = router_bias.reshape(N_EXPERTS, 1)
    row = lambda v: v.reshape(1, -1)

    new = {k: ([], []) for k in ("delta", "dconv", "pool", "lru", "lconv")}
    for layer in range(DEPTH):
        j = layer // 2
        mixed = []
        if layer % 2 == 0:
            w = ab_w_in[j]
            c1 = EV_Z0 + A_V
            w_perm = jnp.concatenate(
                [w[:, :c1], w[:, c1 + 2 * A_HEADS:], w[:, c1:c1 + 2 * A_HEADS],
                 jnp.zeros((d, LANES - 2 * A_HEADS), w.dtype)], axis=1).astype(MM)
            gvec = jnp.zeros((2, LANES), F32)
            gvec = gvec.at[0, A_HEADS:2 * A_HEADS].set(a_log_decay[j])
            gvec = gvec.at[1, A_HEADS:2 * A_HEADS].set(a_dt_bias[j])
            for gi, g in enumerate(groups):
                b_, t_ = g["batch"], g["seq"]
                proj = _proj(x, w_perm, g["row0"], b_ * t_, tm)
                p3 = proj.reshape(b_, t_, EV_W)
                if g["fresh"]:
                    dconv = jnp.zeros((b_, 3, A_CONV_CH), F32)
                    delta = jnp.zeros((b_, A_HEADS, A_DK, A_DV), F32)
                    pool = jnp.zeros((b_, B_HIST, B_WIDTH), F32)
                else:
                    dconv, delta, pool = state_delta_conv[j], state_delta[j], state_pool[j]
                o, s_new = _even_mixer(
                    proj, _pad_hist(dconv, SUBLANES), delta, _pad_hist(pool, 2 * SUBLANES),
                    a_conv_w[j], gvec, row(a_norm_w[j]), b_w_group[j].astype(MM), row(b_scale[j]),
                    batch=b_, seq=t_, hist_valid=g["hist_valid"], **g["ev"])
                mixed.append(o)
                new["delta"][gi].append(s_new)
                new["dconv"][gi].append(_new_hist(dconv, p3[:, :, :A_CONV_CH], 3))
                new["pool"][gi].append(_new_hist(pool, p3[:, :, EV_U0:EV_U0 + B_WIDTH], B_HIST))
            w_out = ab_w_out[j].astype(MM)
        else:
            w_mm = c_w_in[j].astype(MM)
            for gi, g in enumerate(groups):
                b_, t_ = g["batch"], g["seq"]
                p3 = _proj(x, w_mm, g["row0"], b_ * t_, tm).reshape(b_, t_, 2 * C_WIDTH)
                if g["fresh"]:
                    lconv = jnp.zeros((b_, 3, C_WIDTH), F32)
                    lru = jnp.zeros((b_, C_WIDTH), F32)
                else:
                    lconv, lru = state_lru_conv[j], state_lru[j]
                o3, h_last = _odd_mixer(
                    p3, _pad_hist(lconv, SUBLANES), lru, c_conv_w[j], row(c_conv_b[j]),
                    c_w_r[j].astype(MM), row(c_b_r[j]), c_w_i[j].astype(MM), row(c_b_i[j]),
                    row(c_lambda[j]), **g["od"])
                mixed.append(o3.reshape(b_ * t_, C_WIDTH))
                new["lru"][gi].append(h_last)
                new["lconv"][gi].append(_new_hist(lconv, p3[:, :, C_WIDTH:], 3))
            w_out = c_w_out[j].astype(MM)

        x1, cls, wab = _mix_out(mixed[0], mixed[1], w_out, x, row(ln_mix_g[layer]),
                                row(ln_mix_b[layer]), rwt, rb, tm)
        y = _moe(x1, cls, wab, moe_w_gate[layer], moe_w_up[layer], moe_w_down[layer], tm)
        if layer < DEPTH - 1:
            x = _ffn_out(x1, y, row(ln_ffn_g[layer]), row(ln_ffn_b[layer]), 0, n, tm)
        else:
            y_p = _ffn_out(x1, y, row(ln_ffn_g[layer]), row(ln_ffn_b[layer]), 0, n_p, tm)
            y_s = _ffn_out(x1, y, row(ln_ffn_g[layer]), row(ln_ffn_b[layer]), n_p, n_s, tm)

    stack = lambda key, gi: jnp.stack(new[key][gi])
    return (y_p.reshape(bp, tp, d), y_s.reshape(bs, ts, d),
            stack("delta", 0), stack("dconv", 0), stack("pool", 0), stack("lru", 0), stack("lconv", 0),
            stack("delta", 1), stack("dconv", 1), stack("pool", 1), stack("lru", 1), stack("lconv", 1))
```

```python
import functools

import jax
import jax.numpy as jnp
from jax import lax
from jax.experimental import pallas as pl
from jax.experimental.pallas import tpu as pltpu

F32 = jnp.float32
I32 = jnp.int32
MM = jnp.bfloat16
HIGHEST = lax.Precision.HIGHEST

D_MODEL = 1024
DEPTH = 4
A_HEADS = 4
A_DK = 128
A_DV = 128
A_QK = A_HEADS * A_DK
A_V = A_HEADS * A_DV
A_CONV_CH = 2 * A_QK + A_V
A_CHUNK = 64
B_GROUPS = 4
B_GW = 128
B_WINDOWS = (2, 4, 8, 16)
B_HIST = 15
B_WIDTH = B_GROUPS * B_GW
C_WIDTH = D_MODEL
C_HEADS = 4
C_HW = C_WIDTH // C_HEADS
C_GATE = 8.0
N_EXPERTS = 16
N_GROUPS = 4
EPG = N_EXPERTS // N_GROUPS
D_EXPERT = 512
ALPHA = (2 * DEPTH) ** 0.25
LN_EPS = 1e-5
RMS_EPS = 1e-6

LANES = 128
SUBLANES = 8
VMEM_LIMIT = 56 * 1024 * 1024

EV_Z0 = A_CONV_CH
EV_U0 = EV_Z0 + A_V
EV_BA0 = EV_U0 + B_WIDTH
EV_W = EV_BA0 + LANES

PAIRS = ((0, 1), (0, 2), (0, 3), (1, 2), (1, 3), (2, 3))
N_CLASSES = N_GROUPS * len(PAIRS)

TOKEN_TILE = 256


def _cparams(sem):
    return pltpu.CompilerParams(dimension_semantics=sem, vmem_limit_bytes=VMEM_LIMIT)


def _dot(a, b):
    return jnp.dot(a.astype(MM), b.astype(MM), preferred_element_type=F32)


def _dot_nt(a, b):
    return lax.dot_general(a.astype(MM), b.astype(MM), (((1,), (1,)), ((), ())),
                           preferred_element_type=F32)


def _dot_tn(a, b):
    return lax.dot_general(a.astype(MM), b.astype(MM), (((0,), (0,)), ((), ())),
                           preferred_element_type=F32)


def _sigmoid(x):
    return 1.0 / (1.0 + jnp.exp(-x))


def _silu(x):
    return x * _sigmoid(x)


def _softplus(x):
    return jnp.maximum(x, 0.0) + jnp.log1p(jnp.exp(-jnp.abs(x)))


def _gelu_tanh(x):
    return x * (0.5 * (1.0 + jnp.tanh(0.7978845608028654 * (x + 0.044715 * (x * x * x)))))


def _layer_norm(v, g, b):
    mu = jnp.mean(v, axis=-1, keepdims=True)
    d = v - mu
    var = jnp.mean(d * d, axis=-1, keepdims=True)
    return d * lax.rsqrt(var + LN_EPS) * g + b


def _proj_kernel(x_ref, w_ref, o_ref):
    o_ref[...] = jnp.dot(x_ref[...].astype(MM), w_ref[...], preferred_element_type=F32)


def _proj(x, w_mm, row0, nrows, tm):
    k, width = w_mm.shape
    off = row0 // tm
    return pl.pallas_call(
        _proj_kernel,
        grid=(nrows // tm,),
        in_specs=[pl.BlockSpec((tm, k), lambda i: (i + off, 0)),
                  pl.BlockSpec((k, width), lambda i: (0, 0))],
        out_specs=pl.BlockSpec((tm, width), lambda i: (i, 0)),
        out_shape=jax.ShapeDtypeStruct((nrows, width), F32),
        compiler_params=_cparams(("parallel",)),
        name="proj",
    )(x, w_mm)


def _even_kernel(p_ref, convh_ref, s0_ref, poolh_ref, convw_ref, gvec_ref, normw_ref,
                 wgrp_ref, scale_ref, o_ref, snew_ref, ext_scr, qkv_scr, pext_scr, s_scr,
                 *, nb_blk, rows, chunk, hist_valid, carry_hist):
    i = pl.program_id(1)
    n_chunks = rows // chunk
    n_neumann = max((chunk - 1).bit_length() - 1, 0)
    hb = 2 * SUBLANES

    @pl.when(i == 0)
    def _():
        s_scr[...] = s0_ref[...]
        ext_scr[:, 0:SUBLANES, :] = convh_ref[...]
        pext_scr[:, 0:hb, :] = poolh_ref[...]

    rid = lax.broadcasted_iota(I32, (chunk, chunk), 0)
    cid = lax.broadcasted_iota(I32, (chunk, chunk), 1)
    incl = rid >= cid
    strict = rid > cid
    eye = (rid == cid).astype(F32)
    ltri = incl.astype(F32)
    lane = lax.broadcasted_iota(I32, (chunk, LANES), 1)
    trow = lax.broadcasted_iota(I32, (rows, LANES), 0)
    pos = (i * rows + trow + (1 + hist_valid)).astype(F32)
    neg_decay_rate = -jnp.exp(gvec_ref[0:1, :])
    dt_bias = gvec_ref[1:2, :]

    for nb in range(nb_blk):
        r0 = nb * rows
        ext_scr[nb, SUBLANES:SUBLANES + rows, :] = p_ref[r0:r0 + rows, 0:A_CONV_CH]
        for ct in range(A_CONV_CH // LANES):
            cs = slice(ct * LANES, (ct + 1) * LANES)
            acc = None
            for j in range(4):
                lo = SUBLANES - 3 + j
                term = ext_scr[nb, lo:lo + rows, cs] * convw_ref[j:j + 1, cs]
                acc = term if acc is None else acc + term
            t = _silu(acc)
            if ct < 2 * A_HEADS:
                t = t * lax.rsqrt(jnp.sum(t * t, axis=-1, keepdims=True) + 1e-6)
                if ct < A_HEADS:
                    t = t * (A_DK ** -0.5)
            qkv_scr[r0:r0 + rows, cs] = t
        if carry_hist:
            ext_scr[nb, 0:SUBLANES, :] = ext_scr[nb, rows:rows + SUBLANES, :]

        pext_scr[nb, hb:hb + rows, :] = p_ref[r0:r0 + rows, EV_U0:EV_U0 + B_WIDTH]
        for gi, w in enumerate(B_WINDOWS):
            cs = slice(gi * B_GW, (gi + 1) * B_GW)
            cur = pext_scr[nb, hb:hb + rows, cs]
            tot = cur
            for j in range(1, w):
                tot = tot + pext_scr[nb, hb - j:hb - j + rows, cs]
            pooled = tot / jnp.minimum(pos, float(w)) - cur
            ob = _dot(pooled, wgrp_ref[gi]) * scale_ref[:, cs]
            o_ref[r0:r0 + rows, A_V + gi * B_GW:A_V + (gi + 1) * B_GW] = ob
        if carry_hist:
            pext_scr[nb, 0:hb, :] = pext_scr[nb, rows:rows + hb, :]

    chunks = [(nb, c) for nb in range(nb_blk) for c in range(n_chunks)]
    probs = [(nb, c, h) for nb, c in chunks for h in range(A_HEADS)]
    rs = {(nb, c): slice(nb * rows + c * chunk, nb * rows + (c + 1) * chunk) for nb, c in chunks}
    gcol = lambda arr, h: arr[:, A_HEADS + h:A_HEADS + h + 1]
    qf = lambda p: qkv_scr[rs[p[:2]], p[2] * LANES:(p[2] + 1) * LANES]
    kf = lambda p: qkv_scr[rs[p[:2]], A_QK + p[2] * LANES:A_QK + (p[2] + 1) * LANES]
    vf = lambda p: qkv_scr[rs[p[:2]], 2 * A_QK + p[2] * LANES:2 * A_QK + (p[2] + 1) * LANES]

    ba = {ck: p_ref[rs[ck], EV_BA0:EV_BA0 + LANES] for ck in chunks}
    beta_full = {ck: _sigmoid(ba[ck]) for ck in chunks}
    g_full = {ck: neg_decay_rate * _softplus(ba[ck] + dt_bias) for ck in chunks}
    gc_full = {ck: jnp.dot(ltri, g_full[ck], precision=HIGHEST, preferred_element_type=F32)
               for ck in chunks}
    eg_full = {ck: jnp.exp(gc_full[ck]) for ck in chunks}
    gl_full = {ck: gc_full[ck][chunk - 1:chunk, :] for ck in chunks}
    ekd_full = {ck: jnp.exp(gl_full[ck] - gc_full[ck]) for ck in chunks}
    egl_full = {ck: jnp.exp(gl_full[ck]) for ck in chunks}
    sel = [(lane == A_HEADS + h).astype(F32) for h in range(A_HEADS)]
    grow = {p: lax.dot_general(sel[p[2]], gc_full[p[:2]], (((1,), (1,)), ((), ())),
                               precision=HIGHEST, preferred_element_type=F32) for p in probs}
    decay = {p: jnp.where(incl, jnp.exp(jnp.where(incl, gcol(gc_full[p[:2]], p[2]) - grow[p], 0.0)), 0.0)
             for p in probs}
    beta = {p: beta_full[p[:2]][:, p[2]:p[2] + 1] for p in probs}
    kk = {p: _dot_nt(kf(p) * beta[p], kf(p)) for p in probs}
    qk = {p: _dot_nt(qf(p), kf(p)) for p in probs}
    a_intra = {p: jnp.where(incl, qk[p] * decay[p], 0.0) for p in probs}
    power = {p: -jnp.where(strict, kk[p] * decay[p], 0.0) for p in probs}
    tinv = {p: eye + power[p] for p in probs}
    for _ in range(n_neumann):
        power = {p: _dot(power[p], power[p]) for p in probs}
        tinv = {p: tinv[p] + _dot(tinv[p], power[p]) for p in probs}
    u_in = {p: _dot(tinv[p], vf(p) * beta[p]) for p in probs}
    w_in = {p: _dot(tinv[p], kf(p) * beta[p] * gcol(eg_full[p[:2]], p[2])) for p in probs}

    for c in range(n_chunks):
        cp = [(nb, c, h) for nb in range(nb_blk) for h in range(A_HEADS)]
        s_old = {p: s_scr[p[0], p[2]] for p in cp}
        wq = {p: _dot(jnp.concatenate([w_in[p], qf(p) * gcol(eg_full[p[:2]], p[2])], axis=0), s_old[p])
              for p in cp}
        u_new = {p: u_in[p] - wq[p][0:chunk] for p in cp}
        au = {p: _dot(a_intra[p], u_new[p]) for p in cp}
        ku = {p: _dot_tn(kf(p) * gcol(ekd_full[p[:2]], p[2]), u_new[p]) for p in cp}
        for p in cp:
            nb, _, h = p
            s_scr[nb, h] = s_old[p] * gcol(egl_full[p[:2]], h) + ku[p]
            o = wq[p][chunk:2 * chunk] + au[p]
            o = o * lax.rsqrt(jnp.mean(o * o, axis=-1, keepdims=True) + RMS_EPS) * normw_ref[...]
            z = p_ref[rs[p[:2]], EV_Z0 + h * LANES:EV_Z0 + (h + 1) * LANES]
            o_ref[rs[p[:2]], h * LANES:(h + 1) * LANES] = o * _silu(z)

    @pl.when(i == pl.num_programs(1) - 1)
    def _():
        snew_ref[...] = s_scr[...]


def _even_mixer(proj, convh8, s0, poolh16, convw, gvec, normw, wgrp_mm, scale,
                *, batch, seq, nb_blk, rows, hist_valid):
    chunk = min(A_CHUNK, seq)
    assert seq % rows == 0 and rows % chunk == 0 and batch % nb_blk == 0
    assert nb_blk == 1 or rows == seq
    t_blocks = seq // rows
    blk = nb_blk * rows
    hb = 2 * SUBLANES
    kern = functools.partial(_even_kernel, nb_blk=nb_blk, rows=rows, chunk=chunk,
                             hist_valid=hist_valid, carry_hist=t_blocks > 1)
    full = lambda shape: pl.BlockSpec(shape, lambda b, i: (0,) * len(shape))
    return pl.pallas_call(
        kern,
        grid=(batch // nb_blk, t_blocks),
        in_specs=[
            pl.BlockSpec((blk, EV_W), lambda b, i: (b * t_blocks + i, 0)),
            pl.BlockSpec((nb_blk, SUBLANES, A_CONV_CH), lambda b, i: (b, 0, 0)),
            pl.BlockSpec((nb_blk, A_HEADS, A_DK, A_DV), lambda b, i: (b, 0, 0, 0)),
            pl.BlockSpec((nb_blk, hb, B_WIDTH), lambda b, i: (b, 0, 0)),
            full((4, A_CONV_CH)), full((2, LANES)), full((1, A_DV)),
            full((B_GROUPS, B_GW, B_GW)), full((1, B_WIDTH)),
        ],
        out_specs=[
            pl.BlockSpec((blk, D_MODEL), lambda b, i: (b * t_blocks + i, 0)),
            pl.BlockSpec((nb_blk, A_HEADS, A_DK, A_DV), lambda b, i: (b, 0, 0, 0)),
        ],
        out_shape=[jax.ShapeDtypeStruct((batch * seq, D_MODEL), F32),
                   jax.ShapeDtypeStruct((batch, A_HEADS, A_DK, A_DV), F32)],
        scratch_shapes=[
            pltpu.VMEM((nb_blk, SUBLANES + rows, A_CONV_CH), F32),
            pltpu.VMEM((blk, A_CONV_CH), F32),
            pltpu.VMEM((nb_blk, hb + rows, B_WIDTH), F32),
            pltpu.VMEM((nb_blk, A_HEADS, A_DK, A_DV), F32),
        ],
        compiler_params=_cparams(("parallel", "arbitrary")),
        name="even_mixer",
    )(proj, convh8, s0, poolh16, convw, gvec, normw, wgrp_mm, scale)


def _odd_kernel(p_ref, convh_ref, h0_ref, convw_ref, convb_ref, wr_ref, br_ref, wi_ref, bi_ref,
                lam_ref, o_ref, hlast_ref, ext_scr, a_scr, b_scr, hs_scr, h_scr,
                *, nb_blk, rows, carry_hist):
    i = pl.program_id(1)
    blk = nb_blk * rows

    @pl.when(i == 0)
    def _():
        h_scr[...] = h0_ref[...]
        ext_scr[:, 0:SUBLANES, :] = convh_ref[...]

    ext_scr[:, SUBLANES:SUBLANES + rows, :] = p_ref[:, :, C_WIDTH:2 * C_WIDTH]
    log_base = -C_GATE * _softplus(-lam_ref[...])
    for h in range(C_HEADS):
        cs = slice(h * C_HW, (h + 1) * C_HW)
        acc = None
        for j in range(4):
            lo = SUBLANES - 3 + j
            term = ext_scr[:, lo:lo + rows, cs] * convw_ref[j:j + 1, cs]
            acc = term if acc is None else acc + term
        xc = (acc + convb_ref[:, cs]).reshape(blk, C_HW)
        r = _sigmoid(_dot(xc, wr_ref[h]) + br_ref[:, cs])
        gi = _sigmoid(_dot(xc, wi_ref[h]) + bi_ref[:, cs])
        log_a = r * log_base[:, cs]
        a = jnp.exp(log_a)
        th = jnp.tanh(log_a)
        bt = jnp.sqrt(-2.0 * th / (1.0 - th)) * gi * xc
        for lt in range(C_HW // LANES):
            a_scr[h * (C_HW // LANES) + lt] = a[:, lt * LANES:(lt + 1) * LANES]
            b_scr[h * (C_HW // LANES) + lt] = bt[:, lt * LANES:(lt + 1) * LANES]
    if carry_hist:
        ext_scr[:, 0:SUBLANES, :] = ext_scr[:, rows:rows + SUBLANES, :]

    n_lt = C_WIDTH // LANES

    def step(t, hcur):
        hnew = []
        for lt in range(n_lt):
            a_t = a_scr[lt, pl.ds(t, nb_blk, stride=rows), :]
            b_t = b_scr[lt, pl.ds(t, nb_blk, stride=rows), :]
            hn = a_t * hcur[lt] + b_t
            hs_scr[lt, pl.ds(t, nb_blk, stride=rows), :] = hn
            hnew.append(hn)
        return tuple(hnew)

    h_init = tuple(h_scr[:, lt * LANES:(lt + 1) * LANES] for lt in range(n_lt))
    h_fin = lax.fori_loop(0, rows, step, h_init, unroll=SUBLANES)
    for lt in range(n_lt):
        cs = slice(lt * LANES, (lt + 1) * LANES)
        h_scr[:, cs] = h_fin[lt]
        y = _gelu_tanh(p_ref[:, :, cs])
        o_ref[:, :, cs] = y * hs_scr[lt].reshape(nb_blk, rows, LANES)

    @pl.when(i == pl.num_programs(1) - 1)
    def _():
        hlast_ref[...] = h_scr[...]


def _odd_mixer(proj3, convh8, h0, convw, convb, wr_mm, br, wi_mm, bi, lam, *, nb_blk, rows):
    batch, seq, _ = proj3.shape
    assert seq % rows == 0 and batch % nb_blk == 0 and rows % SUBLANES == 0
    t_blocks = seq // rows
    blk = nb_blk * rows
    kern = functools.partial(_odd_kernel, nb_blk=nb_blk, rows=rows, carry_hist=t_blocks > 1)
    full = lambda shape: pl.BlockSpec(shape, lambda b, i: (0,) * len(shape))
    return pl.pallas_call(
        kern,
        grid=(batch // nb_blk, t_blocks),
        in_specs=[
            pl.BlockSpec((nb_blk, rows, 2 * C_WIDTH), lambda b, i: (b, i, 0)),
            pl.BlockSpec((nb_blk, SUBLANES, C_WIDTH), lambda b, i: (b, 0, 0)),
            pl.BlockSpec((nb_blk, C_WIDTH), lambda b, i: (b, 0)),
            full((4, C_WIDTH)), full((1, C_WIDTH)),
            full((C_HEADS, C_HW, C_HW)), full((1, C_WIDTH)),
            full((C_HEADS, C_HW, C_HW)), full((1, C_WIDTH)),
            full((1, C_WIDTH)),
        ],
        out_specs=[
            pl.BlockSpec((nb_blk, rows, C_WIDTH), lambda b, i: (b, i, 0)),
            pl.BlockSpec((nb_blk, C_WIDTH), lambda b, i: (b, 0)),
        ],
        out_shape=[jax.ShapeDtypeStruct((batch, seq, C_WIDTH), F32),
                   jax.ShapeDtypeStruct((batch, C_WIDTH), F32)],
        scratch_shapes=[
            pltpu.VMEM((nb_blk, SUBLANES + rows, C_WIDTH), F32),
            pltpu.VMEM((C_WIDTH // LANES, blk, LANES), F32),
            pltpu.VMEM((C_WIDTH // LANES, blk, LANES), F32),
            pltpu.VMEM((C_WIDTH // LANES, blk, LANES), F32),
            pltpu.VMEM((nb_blk, C_WIDTH), F32),
        ],
        compiler_params=_cparams(("parallel", "arbitrary")),
        name="odd_mixer",
    )(proj3, convh8, h0, convw, convb, wr_mm, br, wi_mm, bi, lam)


def _route(x1, rwt_ref, rb_ref, cls_ref, wab_ref):
    logits = lax.dot_general(rwt_ref[...], x1, (((1,), (1,)), ((), ())),
                             precision=HIGHEST, preferred_element_type=F32)
    sc = _sigmoid(logits)
    bz = sc + rb_ref[...]
    row = lambda arr, e: arr[e:e + 1, :]
    best = None
    gidx = None
    for g in range(N_GROUPS):
        r = [row(bz, EPG * g + k) for k in range(EPG)]
        top2 = None
        for a, b in PAIRS:
            s = r[a] + r[b]
            top2 = s if top2 is None else jnp.maximum(top2, s)
        if best is None:
            best, gidx = top2, jnp.zeros(top2.shape, I32)
        else:
            upd = top2 > best
            gidx = jnp.where(upd, g, gidx)
            best = jnp.where(upd, top2, best)
    sb, ss = [], []
    for k in range(EPG):
        vb, vs = row(bz, k), row(sc, k)
        for g in range(1, N_GROUPS):
            vb = jnp.where(gidx == g, row(bz, EPG * g + k), vb)
            vs = jnp.where(gidx == g, row(sc, EPG * g + k), vs)
        sb.append(vb)
        ss.append(vs)
    m1, i1 = sb[0], jnp.zeros(sb[0].shape, I32)
    for k in range(1, EPG):
        upd = sb[k] > m1
        i1 = jnp.where(upd, k, i1)
        m1 = jnp.where(upd, sb[k], m1)
    m2, i2 = None, None
    for k in range(EPG):
        cand = jnp.where(i1 == k, -jnp.inf, sb[k])
        if m2 is None:
            m2, i2 = cand, jnp.zeros(cand.shape, I32)
        else:
            upd = cand > m2
            i2 = jnp.where(upd, k, i2)
            m2 = jnp.where(upd, cand, m2)
    lo = jnp.minimum(i1, i2)
    hi = jnp.maximum(i1, i2)
    pair = jnp.where(lo == 0, 0, jnp.where(lo == 1, 3, 5)) + hi - lo - 1
    s_lo, s_hi = ss[0], ss[0]
    for k in range(1, EPG):
        s_lo = jnp.where(lo == k, ss[k], s_lo)
        s_hi = jnp.where(hi == k, ss[k], s_hi)
    den = s_lo + s_hi
    cls_ref[...] = gidx * len(PAIRS) + pair
    wab_ref[0:1, :] = s_lo / den
    wab_ref[1:2, :] = s_hi / den


def _mix_out_kernel(ap_ref, as_ref, w_ref, x_ref, g_ref, b_ref, rwt_ref, rb_ref,
                    x1_ref, cls_ref, wab_ref, *, np_tiles):
    i = pl.program_id(0)

    def finish(a_ref):
        hmix = jnp.dot(a_ref[...].astype(MM), w_ref[...], preferred_element_type=F32)
        x1 = _layer_norm(ALPHA * x_ref[...] + hmix, g_ref[...], b_ref[...])
        x1_ref[...] = x1
        _route(x1, rwt_ref, rb_ref, cls_ref, wab_ref)

    @pl.when(i < np_tiles)
    def _():
        finish(ap_ref)

    @pl.when(i >= np_tiles)
    def _():
        finish(as_ref)


def _mix_out(a_p, a_s, w_mm, x, g, b, rwt, rb, tm):
    n = x.shape[0]
    np_tiles = a_p.shape[0] // tm
    ns_tiles = a_s.shape[0] // tm
    assert (np_tiles + ns_tiles) * tm == n
    kern = functools.partial(_mix_out_kernel, np_tiles=np_tiles)
    full = lambda shape: pl.BlockSpec(shape, lambda i: (0,) * len(shape))
    return pl.pallas_call(
        kern,
        grid=(np_tiles + ns_tiles,),
        in_specs=[
            pl.BlockSpec((tm, D_MODEL), lambda i: (jnp.minimum(i, np_tiles - 1), 0)),
            pl.BlockSpec((tm, D_MODEL), lambda i: (jnp.maximum(i - np_tiles, 0), 0)),
            full((D_MODEL, D_MODEL)),
            pl.BlockSpec((tm, D_MODEL), lambda i: (i, 0)),
            full((1, D_MODEL)), full((1, D_MODEL)),
            full((N_EXPERTS, D_MODEL)), full((N_EXPERTS, 1)),
        ],
        out_specs=[
            pl.BlockSpec((tm, D_MODEL), lambda i: (i, 0)),
            pl.BlockSpec((1, tm), lambda i: (0, i)),
            pl.BlockSpec((2, tm), lambda i: (0, i)),
        ],
        out_shape=[jax.ShapeDtypeStruct((n, D_MODEL), F32),
                   jax.ShapeDtypeStruct((1, n), I32),
                   jax.ShapeDtypeStruct((2, n), F32)],
        compiler_params=_cparams(("arbitrary",)),
        name="mix_out",
    )(a_p, a_s, w_mm, x, g, b, rwt, rb)


def _moe_kernel(nvalid_ref, chga_ref, chgb_ref, ea_ref, eb_ref, rowsrc_ref, rowdst_ref,
                x_hbm, wa_ref, wb_ref, wga_ref, wua_ref, wda_ref, wgb_ref, wub_ref, wdb_ref,
                y_hbm, xbuf, obuf, ga_scr, ua_scr, da_scr, gb_scr, ub_scr, db_scr, gsem, ssem,
                *, tm):
    t = pl.program_id(0)
    nt = pl.num_programs(0)
    slot = t % 2

    def gather_copy(tok, r, sl):
        return pltpu.make_async_copy(x_hbm.at[pl.ds(tok, 1)], xbuf.at[sl, pl.ds(r, 1)], gsem.at[sl])

    def scatter_copy(tok, r):
        return pltpu.make_async_copy(obuf.at[pl.ds(r, 1)], y_hbm.at[pl.ds(tok, 1)], ssem.at[0])

    def start_gather(tile, sl):
        def body(r, carry):
            gather_copy(rowsrc_ref[tile * tm + r], r, sl).start()
            return carry
        lax.fori_loop(0, tm, body, 0, unroll=8)

    def wait_scatter():
        pltpu.make_async_copy(obuf, y_hbm.at[pl.ds(0, tm)], ssem.at[0]).wait()

    @pl.when(t == 0)
    def _():
        start_gather(0, 0)
        obuf[...] = jnp.zeros(obuf.shape, obuf.dtype)
        spare = pltpu.make_async_copy(obuf, y_hbm.at[pl.ds(y_hbm.shape[0] - tm, tm)], ssem.at[0])
        spare.start()
        spare.wait()

    nxt = jnp.minimum(t + 1, nt - 1)

    @pl.when(jnp.logical_and(t + 1 < nt, nvalid_ref[nxt] > 0))
    def _():
        start_gather(t + 1, 1 - slot)

    @pl.when(chga_ref[t] == 1)
    def _():
        ga_scr[...] = wga_ref[0, 0].astype(MM)
        ua_scr[...] = wua_ref[0, 0].astype(MM)
        da_scr[...] = wda_ref[0, 0].astype(MM)

    @pl.when(chgb_ref[t] == 1)
    def _():
        gb_scr[...] = wgb_ref[0, 0].astype(MM)
        ub_scr[...] = wub_ref[0, 0].astype(MM)
        db_scr[...] = wdb_ref[0, 0].astype(MM)

    prev = jnp.maximum(t - 1, 0)
    n_prev = jnp.where(t > 0, nvalid_ref[prev], 0)
    n_cur = nvalid_ref[t]

    @pl.when(n_cur > 0)
    def _():
        pltpu.make_async_copy(x_hbm.at[pl.ds(0, tm)], xbuf.at[slot], gsem.at[slot]).wait()
        x = xbuf[slot].astype(MM)

        def expert(wg, wu, wd, wrow):
            gate = jnp.dot(x, wg[...], preferred_element_type=F32)
            up = jnp.dot(x, wu[...], preferred_element_type=F32)
            hid = (_silu(gate) * up).astype(MM)
            return jnp.dot(hid, wd[...], preferred_element_type=F32) * wrow

        y = expert(ga_scr, ua_scr, da_scr, wa_ref[...]) + expert(gb_scr, ub_scr, db_scr, wb_ref[...])

        @pl.when(n_prev > 0)
        def _():
            wait_scatter()

        obuf[...] = y

        def body(r, carry):
            scatter_copy(rowdst_ref[t * tm + r], r).start()
            return carry
        lax.fori_loop(0, tm, body, 0, unroll=8)

        @pl.when(t == nt - 1)
        def _():
            wait_scatter()

    @pl.when(jnp.logical_and(n_cur == 0, n_prev > 0))
    def _():
        wait_scatter()


def _moe(x1, cls, wab, w_gate, w_up, w_down, layer, tm):
    n = x1.shape[0]
    n_tiles = n // tm + N_CLASSES
    n_rows = n_tiles * tm
    cls = cls.reshape(n)
    onehot = (cls[:, None] == jnp.arange(N_CLASSES, dtype=I32)[None, :]).astype(I32)
    csum = jnp.cumsum(onehot, axis=0)
    counts = csum[-1]
    rank = jnp.take_along_axis(csum, cls[:, None], axis=1)[:, 0] - 1
    tiles_c = (counts + tm - 1) // tm
    tile_end_c = jnp.cumsum(tiles_c)
    tile_start_c = tile_end_c - tiles_c
    total = tile_end_c[-1]
    pos = tile_start_c[cls] * tm + rank
    tid = jnp.arange(n_tiles, dtype=I32)
    t_eff = jnp.minimum(tid, total - 1)
    tile_cls = jnp.minimum(jnp.sum((tile_end_c[None, :] <= t_eff[:, None]).astype(I32), axis=1),
                           N_CLASSES - 1)
    left = counts[tile_cls] - (tid - tile_start_c[tile_cls]) * tm
    nvalid = jnp.where(tid < total, jnp.clip(left, 0, tm), 0).astype(I32)
    row_id = jnp.arange(n_rows, dtype=I32)
    row_ok = row_id % tm < nvalid[row_id // tm]
    row_src = jnp.zeros((n_rows,), I32).at[pos].set(jnp.arange(n, dtype=I32))
    row_dst = jnp.where(row_ok, row_src, n + row_id % tm)
    wa_row = jnp.where(row_ok, wab[0][row_src], 0.0).reshape(n_rows, 1)
    wb_row = jnp.where(row_ok, wab[1][row_src], 0.0).reshape(n_rows, 1)
    pair_lo = jnp.array([p[0] for p in PAIRS], I32)
    pair_hi = jnp.array([p[1] for p in PAIRS], I32)
    ea = EPG * (tile_cls // len(PAIRS)) + pair_lo[tile_cls % len(PAIRS)]
    eb = EPG * (tile_cls // len(PAIRS)) + pair_hi[tile_cls % len(PAIRS)]
    first = tid == 0
    chga = jnp.logical_or(first, ea != jnp.roll(ea, 1)).astype(I32)
    chgb = jnp.logical_or(first, eb != jnp.roll(eb, 1)).astype(I32)

    kern = functools.partial(_moe_kernel, tm=tm)
    wspec_a = lambda shape: pl.BlockSpec(
        shape, lambda t, nv, ca, cb, ea_, eb_, rs, rd: (layer, ea_[t], 0, 0))
    wspec_b = lambda shape: pl.BlockSpec(
        shape, lambda t, nv, ca, cb, ea_, eb_, rs, rd: (layer, eb_[t], 0, 0))
    rowspec = pl.BlockSpec((tm, 1), lambda t, nv, ca, cb, ea_, eb_, rs, rd: (t, 0))
    gu = (1, 1, D_MODEL, D_EXPERT)
    dn = (1, 1, D_EXPERT, D_MODEL)
    grid_spec = pltpu.PrefetchScalarGridSpec(
        num_scalar_prefetch=7,
        grid=(n_tiles,),
        in_specs=[pl.BlockSpec(memory_space=pl.ANY), rowspec, rowspec,
                  wspec_a(gu), wspec_a(gu), wspec_a(dn),
                  wspec_b(gu), wspec_b(gu), wspec_b(dn)],
        out_specs=pl.BlockSpec(memory_space=pl.ANY),
        scratch_shapes=[
            pltpu.VMEM((2, tm, D_MODEL), F32),
            pltpu.VMEM((tm, D_MODEL), F32),
            pltpu.VMEM((D_MODEL, D_EXPERT), MM), pltpu.VMEM((D_MODEL, D_EXPERT), MM),
            pltpu.VMEM((D_EXPERT, D_MODEL), MM),
            pltpu.VMEM((D_MODEL, D_EXPERT), MM), pltpu.VMEM((D_MODEL, D_EXPERT), MM),
            pltpu.VMEM((D_EXPERT, D_MODEL), MM),
            pltpu.SemaphoreType.DMA((2,)),
            pltpu.SemaphoreType.DMA((1,)),
        ],
    )
    return pl.pallas_call(
        kern,
        grid_spec=grid_spec,
        out_shape=jax.ShapeDtypeStruct((n + tm, D_MODEL), F32),
        compiler_params=_cparams(("arbitrary",)),
        name="moe",
    )(nvalid, chga, chgb, ea, eb, row_src, row_dst,
      x1, wa_row, wb_row, w_gate, w_up, w_down, w_gate, w_up, w_down)


def _ffn_out_kernel(x_ref, y_ref, g_ref, b_ref, o_ref):
    o_ref[...] = _layer_norm(ALPHA * x_ref[...] + y_ref[...], g_ref[...], b_ref[...])


def _ffn_out(x1, y, g, b, row0, nrows, tm):
    off = row0 // tm
    rowspec = pl.BlockSpec((tm, D_MODEL), lambda i: (i + off, 0))
    vec = pl.BlockSpec((1, D_MODEL), lambda i: (0, 0))
    return pl.pallas_call(
        _ffn_out_kernel,
        grid=(nrows // tm,),
        in_specs=[rowspec, rowspec, vec, vec],
        out_specs=pl.BlockSpec((tm, D_MODEL), lambda i: (i, 0)),
        out_shape=jax.ShapeDtypeStruct((nrows, D_MODEL), F32),
        compiler_params=_cparams(("parallel",)),
        name="ffn_out",
    )(x1, y, g, b)


def _pad_hist(hist, rows):
    b, r, c = hist.shape
    return jnp.concatenate([jnp.zeros((b, rows - r, c), hist.dtype), hist], axis=1)


def _new_hist(hist, cur, c0, c1, keep):
    t = cur.shape[1]
    if t >= keep:
        return cur[:, t - keep:, c0:c1]
    return jnp.concatenate([hist[:, t:], cur[:, :, c0:c1]], axis=1)


def kernel(x_prompt, x_sample, state_delta, state_delta_conv, state_pool, state_lru, state_lru_conv,
           ab_w_in, a_conv_w, a_log_decay, a_dt_bias, a_norm_w, b_w_group, b_scale, ab_w_out,
           c_w_in, c_conv_w, c_conv_b, c_w_r, c_b_r, c_w_i, c_b_i, c_lambda, c_w_out,
           ln_mix_g, ln_mix_b, ln_ffn_g, ln_ffn_b, router_w, router_bias,
           moe_w_gate, moe_w_up, moe_w_down):
    bp, tp, d = x_prompt.shape
    bs, ts, _ = x_sample.shape
    n_p, n_s = bp * tp, bs * ts
    n = n_p + n_s
    tm = TOKEN_TILE
    assert n_p % tm == 0 and n_s % tm == 0

    x = jnp.concatenate([x_prompt.reshape(n_p, d), x_sample.reshape(n_s, d)], axis=0)
    groups = (
        dict(row0=0, batch=bp, seq=tp, hist_valid=0, fresh=True,
             ev=dict(nb_blk=1, rows=min(tp, 256)), od=dict(nb_blk=min(bp, 8), rows=min(tp, 64))),
        dict(row0=n_p, batch=bs, seq=ts, hist_valid=B_HIST, fresh=False,
             ev=dict(nb_blk=min(bs, 8), rows=ts), od=dict(nb_blk=min(bs, 32), rows=ts)),
    )
    rwt = router_w.T
    rb = router_bias.reshape(N_EXPERTS, 1)
    row = lambda v: v.reshape(1, -1)

    new = {k: ([], []) for k in ("delta", "dconv", "pool", "lru", "lconv")}
    for layer in range(DEPTH):
        j = layer // 2
        mixed = []
        if layer % 2 == 0:
            w = ab_w_in[j]
            c1 = EV_Z0 + A_V
            w_perm = jnp.concatenate(
                [w[:, :c1], w[:, c1 + 2 * A_HEADS:], w[:, c1:c1 + 2 * A_HEADS],
                 jnp.zeros((d, LANES - 2 * A_HEADS), w.dtype)], axis=1).astype(MM)
            gvec = jnp.zeros((2, LANES), F32)
            gvec = gvec.at[0, A_HEADS:2 * A_HEADS].set(a_log_decay[j])
            gvec = gvec.at[1, A_HEADS:2 * A_HEADS].set(a_dt_bias[j])
            for gi, g in enumerate(groups):
                b_, t_ = g["batch"], g["seq"]
                proj = _proj(x, w_perm, g["row0"], b_ * t_, tm)
                p3 = proj.reshape(b_, t_, EV_W)
                if g["fresh"]:
                    dconv = jnp.zeros((b_, 3, A_CONV_CH), F32)
                    delta = jnp.zeros((b_, A_HEADS, A_DK, A_DV), F32)
                    pool = jnp.zeros((b_, B_HIST, B_WIDTH), F32)
                else:
                    dconv, delta, pool = state_delta_conv[j], state_delta[j], state_pool[j]
                o, s_new = _even_mixer(
                    proj, _pad_hist(dconv, SUBLANES), delta, _pad_hist(pool, 2 * SUBLANES),
                    a_conv_w[j], gvec, row(a_norm_w[j]), b_w_group[j].astype(MM), row(b_scale[j]),
                    batch=b_, seq=t_, hist_valid=g["hist_valid"], **g["ev"])
                mixed.append(o)
                new["delta"][gi].append(s_new)
                new["dconv"][gi].append(_new_hist(dconv, p3, 0, A_CONV_CH, 3))
                new["pool"][gi].append(_new_hist(pool, p3, EV_U0, EV_U0 + B_WIDTH, B_HIST))
            w_out = ab_w_out[j].astype(MM)
        else:
            w_mm = c_w_in[j].astype(MM)
            for gi, g in enumerate(groups):
                b_, t_ = g["batch"], g["seq"]
                p3 = _proj(x, w_mm, g["row0"], b_ * t_, tm).reshape(b_, t_, 2 * C_WIDTH)
                if g["fresh"]:
                    lconv = jnp.zeros((b_, 3, C_WIDTH), F32)
                    lru = jnp.zeros((b_, C_WIDTH), F32)
                else:
                    lconv, lru = state_lru_conv[j], state_lru[j]
                o3, h_last = _odd_mixer(
                    p3, _pad_hist(lconv, SUBLANES), lru, c_conv_w[j], row(c_conv_b[j]),
                    c_w_r[j].astype(MM), row(c_b_r[j]), c_w_i[j].astype(MM), row(c_b_i[j]),
                    row(c_lambda[j]), **g["od"])
                mixed.append(o3.reshape(b_ * t_, C_WIDTH))
                new["lru"][gi].append(h_last)
                new["lconv"][gi].append(_new_hist(lconv, p3, C_WIDTH, 2 * C_WIDTH, 3))
            w_out = c_w_out[j].astype(MM)

        x1, cls, wab = _mix_out(mixed[0], mixed[1], w_out, x, row(ln_mix_g[layer]),
                                row(ln_mix_b[layer]), rwt, rb, tm)
        y = _moe(x1, cls, wab, moe_w_gate, moe_w_up, moe_w_down, layer, tm)
        if layer < DEPTH - 1:
            x = _ffn_out(x1, y, row(ln_ffn_g[layer]), row(ln_ffn_b[layer]), 0, n, tm)
        else:
            y_p = _ffn_out(x1, y, row(ln_ffn_g[layer]), row(ln_ffn_b[layer]), 0, n_p, tm)
            y_s = _ffn_out(x1, y, row(ln_ffn_g[layer]), row(ln_ffn_b[layer]), n_p, n_s, tm)

    stack = lambda key, gi: jnp.stack(new[key][gi])
    return (y_p.reshape(bp, tp, d), y_s.reshape(bs, ts, d),
            stack("delta", 0), stack("dconv", 0), stack("pool", 0), stack("lru", 0), stack("lconv", 0),
            stack("delta", 1), stack("dconv", 1), stack("pool", 1), stack("lru", 1), stack("lconv", 1))
```

```python
import functools

import jax
import jax.numpy as jnp
from jax import lax
from jax.experimental import pallas as pl
from jax.experimental.pallas import tpu as pltpu

F32 = jnp.float32
I32 = jnp.int32
MM = jnp.bfloat16
HIGHEST = lax.Precision.HIGHEST

D_MODEL = 1024
DEPTH = 4
A_HEADS = 4
A_DK = 128
A_DV = 128
A_QK = A_HEADS * A_DK
A_V = A_HEADS * A_DV
A_CONV_CH = 2 * A_QK + A_V
A_CHUNK = 64
B_GROUPS = 4
B_GW = 128
B_WINDOWS = (2, 4, 8, 16)
B_HIST = 15
B_WIDTH = B_GROUPS * B_GW
C_WIDTH = D_MODEL
C_HEADS = 4
C_HW = C_WIDTH // C_HEADS
C_GATE = 8.0
N_EXPERTS = 16
N_GROUPS = 4
EPG = N_EXPERTS // N_GROUPS
D_EXPERT = 512
ALPHA = (2 * DEPTH) ** 0.25
LN_EPS = 1e-5
RMS_EPS = 1e-6

LANES = 128
SUBLANES = 8
VMEM_LIMIT = 56 * 1024 * 1024

EV_Z0 = A_CONV_CH
EV_U0 = EV_Z0 + A_V
EV_BA0 = EV_U0 + B_WIDTH
EV_W = EV_BA0 + LANES

PAIRS = ((0, 1), (0, 2), (0, 3), (1, 2), (1, 3), (2, 3))
N_CLASSES = N_GROUPS * len(PAIRS)

TOKEN_TILE = 256
MIX_TILE = 512
CLS_ROWS = 32
KEY_CLS = 1 << 16
X1E_W = D_MODEL + LANES


def _cparams(sem):
    return pltpu.CompilerParams(dimension_semantics=sem, vmem_limit_bytes=VMEM_LIMIT)


def _dot(a, b):
    return jnp.dot(a.astype(MM), b.astype(MM), preferred_element_type=F32)


def _dot_nt(a, b):
    return lax.dot_general(a.astype(MM), b.astype(MM), (((1,), (1,)), ((), ())),
                           preferred_element_type=F32)


def _dot_tn(a, b):
    return lax.dot_general(a.astype(MM), b.astype(MM), (((0,), (0,)), ((), ())),
                           preferred_element_type=F32)


def _sigmoid(x):
    return 1.0 / (1.0 + jnp.exp(-x))


def _sigmoid_t(x):
    return 0.5 * jnp.tanh(0.5 * x) + 0.5


def _silu(x):
    return x * _sigmoid(x)


def _softplus(x):
    return jnp.maximum(x, 0.0) + jnp.log1p(jnp.exp(-jnp.abs(x)))


def _gelu_tanh(x):
    return x * (0.5 * (1.0 + jnp.tanh(0.7978845608028654 * (x + 0.044715 * (x * x * x)))))


def _layer_norm(v, g, b):
    mu = jnp.mean(v, axis=-1, keepdims=True)
    d = v - mu
    var = jnp.mean(d * d, axis=-1, keepdims=True)
    return d * lax.rsqrt(var + LN_EPS) * g + b


def _proj_kernel(x_ref, w_ref, o_ref):
    o_ref[...] = jnp.dot(x_ref[...].astype(MM), w_ref[...], preferred_element_type=F32)


def _proj(x, w_mm, row0, nrows, tm):
    k, width = w_mm.shape
    off = row0 // tm
    return pl.pallas_call(
        _proj_kernel,
        grid=(nrows // tm,),
        in_specs=[pl.BlockSpec((tm, k), lambda i: (i + off, 0)),
                  pl.BlockSpec((k, width), lambda i: (0, 0))],
        out_specs=pl.BlockSpec((tm, width), lambda i: (i, 0)),
        out_shape=jax.ShapeDtypeStruct((nrows, width), F32),
        compiler_params=_cparams(("parallel",)),
        name="proj",
    )(x, w_mm)


def _even_kernel(p_ref, convh_ref, s0_ref, poolh_ref, convw_ref, gvec_ref, normw_ref,
                 wgrp_ref, scale_ref, o_ref, snew_ref, ext_scr, qkv_scr, pext_scr, s_scr,
                 *, nb_blk, rows, chunk, hist_valid, carry_hist):
    i = pl.program_id(1)
    n_chunks = rows // chunk
    n_neumann = max((chunk - 1).bit_length() - 1, 0)
    hb = 2 * SUBLANES

    @pl.when(i == 0)
    def _():
        s_scr[...] = s0_ref[...]
        ext_scr[:, 0:SUBLANES, :] = convh_ref[...]
        pext_scr[:, 0:hb, :] = poolh_ref[...]

    rid = lax.broadcasted_iota(I32, (chunk, chunk), 0)
    cid = lax.broadcasted_iota(I32, (chunk, chunk), 1)
    incl = rid >= cid
    strict = rid > cid
    eye = (rid == cid).astype(F32)
    ltri = incl.astype(F32)
    lane = lax.broadcasted_iota(I32, (chunk, LANES), 1)
    trow = lax.broadcasted_iota(I32, (rows, LANES), 0)
    pos = (i * rows + trow + (1 + hist_valid)).astype(F32)
    neg_decay_rate = -jnp.exp(gvec_ref[0:1, :])
    dt_bias = gvec_ref[1:2, :]

    for nb in range(nb_blk):
        r0 = nb * rows
        ext_scr[nb, SUBLANES:SUBLANES + rows, :] = p_ref[r0:r0 + rows, 0:A_CONV_CH]
        for ct in range(A_CONV_CH // LANES):
            cs = slice(ct * LANES, (ct + 1) * LANES)
            acc = None
            for j in range(4):
                lo = SUBLANES - 3 + j
                term = ext_scr[nb, lo:lo + rows, cs] * convw_ref[j:j + 1, cs]
                acc = term if acc is None else acc + term
            t = _silu(acc)
            if ct < 2 * A_HEADS:
                t = t * lax.rsqrt(jnp.sum(t * t, axis=-1, keepdims=True) + 1e-6)
                if ct < A_HEADS:
                    t = t * (A_DK ** -0.5)
            qkv_scr[r0:r0 + rows, cs] = t
        if carry_hist:
            ext_scr[nb, 0:SUBLANES, :] = ext_scr[nb, rows:rows + SUBLANES, :]

        pext_scr[nb, hb:hb + rows, :] = p_ref[r0:r0 + rows, EV_U0:EV_U0 + B_WIDTH]
        for gi, w in enumerate(B_WINDOWS):
            cs = slice(gi * B_GW, (gi + 1) * B_GW)
            cur = pext_scr[nb, hb:hb + rows, cs]
            tot = cur
            for j in range(1, w):
                tot = tot + pext_scr[nb, hb - j:hb - j + rows, cs]
            pooled = tot / jnp.minimum(pos, float(w)) - cur
            ob = _dot(pooled, wgrp_ref[gi]) * scale_ref[:, cs]
            o_ref[r0:r0 + rows, A_V + gi * B_GW:A_V + (gi + 1) * B_GW] = ob
        if carry_hist:
            pext_scr[nb, 0:hb, :] = pext_scr[nb, rows:rows + hb, :]

    chunks = [(nb, c) for nb in range(nb_blk) for c in range(n_chunks)]
    probs = [(nb, c, h) for nb, c in chunks for h in range(A_HEADS)]
    rs = {(nb, c): slice(nb * rows + c * chunk, nb * rows + (c + 1) * chunk) for nb, c in chunks}
    gcol = lambda arr, h: arr[:, A_HEADS + h:A_HEADS + h + 1]
    qf = lambda p: qkv_scr[rs[p[:2]], p[2] * LANES:(p[2] + 1) * LANES]
    kf = lambda p: qkv_scr[rs[p[:2]], A_QK + p[2] * LANES:A_QK + (p[2] + 1) * LANES]
    vf = lambda p: qkv_scr[rs[p[:2]], 2 * A_QK + p[2] * LANES:2 * A_QK + (p[2] + 1) * LANES]

    ba = {ck: p_ref[rs[ck], EV_BA0:EV_BA0 + LANES] for ck in chunks}
    beta_full = {ck: _sigmoid(ba[ck]) for ck in chunks}
    g_full = {ck: neg_decay_rate * _softplus(ba[ck] + dt_bias) for ck in chunks}
    gc_full = {ck: jnp.dot(ltri, g_full[ck], precision=HIGHEST, preferred_element_type=F32)
               for ck in chunks}
    eg_full = {ck: jnp.exp(gc_full[ck]) for ck in chunks}
    gl_full = {ck: gc_full[ck][chunk - 1:chunk, :] for ck in chunks}
    ekd_full = {ck: jnp.exp(gl_full[ck] - gc_full[ck]) for ck in chunks}
    egl_full = {ck: jnp.exp(gl_full[ck]) for ck in chunks}
    sel = [(lane == A_HEADS + h).astype(F32) for h in range(A_HEADS)]
    grow = {p: lax.dot_general(sel[p[2]], gc_full[p[:2]], (((1,), (1,)), ((), ())),
                               precision=HIGHEST, preferred_element_type=F32) for p in probs}
    decay = {p: jnp.where(incl, jnp.exp(jnp.where(incl, gcol(gc_full[p[:2]], p[2]) - grow[p], 0.0)), 0.0)
             for p in probs}
    beta = {p: beta_full[p[:2]][:, p[2]:p[2] + 1] for p in probs}
    kk = {p: _dot_nt(kf(p) * beta[p], kf(p)) for p in probs}
    qk = {p: _dot_nt(qf(p), kf(p)) for p in probs}
    a_intra = {p: jnp.where(incl, qk[p] * decay[p], 0.0) for p in probs}
    power = {p: -jnp.where(strict, kk[p] * decay[p], 0.0) for p in probs}
    tinv = {p: eye + power[p] for p in probs}
    for _ in range(n_neumann):
        power = {p: _dot(power[p], power[p]) for p in probs}
        tinv = {p: tinv[p] + _dot(tinv[p], power[p]) for p in probs}
    u_in = {p: _dot(tinv[p], vf(p) * beta[p]) for p in probs}
    w_in = {p: _dot(tinv[p], kf(p) * beta[p] * gcol(eg_full[p[:2]], p[2])) for p in probs}

    for c in range(n_chunks):
        cp = [(nb, c, h) for nb in range(nb_blk) for h in range(A_HEADS)]
        s_old = {p: s_scr[p[0], p[2]] for p in cp}
        wq = {p: _dot(jnp.concatenate([w_in[p], qf(p) * gcol(eg_full[p[:2]], p[2])], axis=0), s_old[p])
              for p in cp}
        u_new = {p: u_in[p] - wq[p][0:chunk] for p in cp}
        au = {p: _dot(a_intra[p], u_new[p]) for p in cp}
        ku = {p: _dot_tn(kf(p) * gcol(ekd_full[p[:2]], p[2]), u_new[p]) for p in cp}
        for p in cp:
            nb, _, h = p
            s_scr[nb, h] = s_old[p] * gcol(egl_full[p[:2]], h) + ku[p]
            o = wq[p][chunk:2 * chunk] + au[p]
            o = o * lax.rsqrt(jnp.mean(o * o, axis=-1, keepdims=True) + RMS_EPS) * normw_ref[...]
            z = p_ref[rs[p[:2]], EV_Z0 + h * LANES:EV_Z0 + (h + 1) * LANES]
            o_ref[rs[p[:2]], h * LANES:(h + 1) * LANES] = o * _silu(z)

    @pl.when(i == pl.num_programs(1) - 1)
    def _():
        snew_ref[...] = s_scr[...]


def _even_mixer(proj, convh8, s0, poolh16, convw, gvec, normw, wgrp_mm, scale,
                *, batch, seq, nb_blk, rows, hist_valid):
    chunk = min(A_CHUNK, seq)
    assert seq % rows == 0 and rows % chunk == 0 and batch % nb_blk == 0
    assert nb_blk == 1 or rows == seq
    t_blocks = seq // rows
    blk = nb_blk * rows
    hb = 2 * SUBLANES
    kern = functools.partial(_even_kernel, nb_blk=nb_blk, rows=rows, chunk=chunk,
                             hist_valid=hist_valid, carry_hist=t_blocks > 1)
    full = lambda shape: pl.BlockSpec(shape, lambda b, i: (0,) * len(shape))
    return pl.pallas_call(
        kern,
        grid=(batch // nb_blk, t_blocks),
        in_specs=[
            pl.BlockSpec((blk, EV_W), lambda b, i: (b * t_blocks + i, 0)),
            pl.BlockSpec((nb_blk, SUBLANES, A_CONV_CH), lambda b, i: (b, 0, 0)),
            pl.BlockSpec((nb_blk, A_HEADS, A_DK, A_DV), lambda b, i: (b, 0, 0, 0)),
            pl.BlockSpec((nb_blk, hb, B_WIDTH), lambda b, i: (b, 0, 0)),
            full((4, A_CONV_CH)), full((2, LANES)), full((1, A_DV)),
            full((B_GROUPS, B_GW, B_GW)), full((1, B_WIDTH)),
        ],
        out_specs=[
            pl.BlockSpec((blk, D_MODEL), lambda b, i: (b * t_blocks + i, 0)),
            pl.BlockSpec((nb_blk, A_HEADS, A_DK, A_DV), lambda b, i: (b, 0, 0, 0)),
        ],
        out_shape=[jax.ShapeDtypeStruct((batch * seq, D_MODEL), F32),
                   jax.ShapeDtypeStruct((batch, A_HEADS, A_DK, A_DV), F32)],
        scratch_shapes=[
            pltpu.VMEM((nb_blk, SUBLANES + rows, A_CONV_CH), F32),
            pltpu.VMEM((blk, A_CONV_CH), F32),
            pltpu.VMEM((nb_blk, hb + rows, B_WIDTH), F32),
            pltpu.VMEM((nb_blk, A_HEADS, A_DK, A_DV), F32),
        ],
        compiler_params=_cparams(("parallel", "arbitrary")),
        name="even_mixer",
    )(proj, convh8, s0, poolh16, convw, gvec, normw, wgrp_mm, scale)


def _odd_kernel(p_ref, convh_ref, h0_ref, convw_ref, convb_ref, wr_ref, br_ref, wi_ref, bi_ref,
                lam_ref, o_ref, hlast_ref, ext_scr, a_scr, b_scr, hs_scr, h_scr,
                *, nb_blk, rows, carry_hist):
    i = pl.program_id(1)
    blk = nb_blk * rows
    pitch = a_scr.shape[1] // nb_blk

    @pl.when(i == 0)
    def _():
        h_scr[...] = h0_ref[...]
        ext_scr[:, 0:SUBLANES, :] = convh_ref[...]

    ext_scr[:, SUBLANES:SUBLANES + rows, :] = p_ref[:, :, C_WIDTH:2 * C_WIDTH]
    log_base = -C_GATE * _softplus(-lam_ref[...])
    for h in range(C_HEADS):
        cs = slice(h * C_HW, (h + 1) * C_HW)
        acc = None
        for j in range(4):
            lo = SUBLANES - 3 + j
            term = ext_scr[:, lo:lo + rows, cs] * convw_ref[j:j + 1, cs]
            acc = term if acc is None else acc + term
        xc = (acc + convb_ref[:, cs]).reshape(blk, C_HW)
        r = _sigmoid_t(_dot(xc, wr_ref[h]) + br_ref[:, cs])
        gi = _sigmoid_t(_dot(xc, wi_ref[h]) + bi_ref[:, cs])
        log_a = r * log_base[:, cs]
        a = jnp.exp(log_a)
        th = jnp.tanh(log_a)
        bt = jnp.sqrt(-2.0 * th) * lax.rsqrt(1.0 - th) * gi * xc
        for lt in range(C_HW // LANES):
            for b in range(nb_blk):
                dst = pl.ds(b * pitch, rows)
                src = slice(b * rows, (b + 1) * rows)
                a_scr[h * (C_HW // LANES) + lt, dst, :] = a[src, lt * LANES:(lt + 1) * LANES]
                b_scr[h * (C_HW // LANES) + lt, dst, :] = bt[src, lt * LANES:(lt + 1) * LANES]
    if carry_hist:
        ext_scr[:, 0:SUBLANES, :] = ext_scr[:, rows:rows + SUBLANES, :]

    n_lt = C_WIDTH // LANES

    def step(t, hcur):
        hnew = []
        for lt in range(n_lt):
            a_t = a_scr[lt, pl.ds(t, nb_blk, stride=pitch), :]
            b_t = b_scr[lt, pl.ds(t, nb_blk, stride=pitch), :]
            hn = a_t * hcur[lt] + b_t
            hs_scr[lt, pl.ds(t, nb_blk, stride=pitch), :] = hn
            hnew.append(hn)
        return tuple(hnew)

    h_init = tuple(h_scr[:, lt * LANES:(lt + 1) * LANES] for lt in range(n_lt))
    h_fin = lax.fori_loop(0, rows, step, h_init, unroll=SUBLANES)
    for lt in range(n_lt):
        cs = slice(lt * LANES, (lt + 1) * LANES)
        h_scr[:, cs] = h_fin[lt]
        for b in range(nb_blk):
            o_ref[b, :, cs] = _gelu_tanh(p_ref[b, :, cs]) * hs_scr[lt, pl.ds(b * pitch, rows), :]

    @pl.when(i == pl.num_programs(1) - 1)
    def _():
        hlast_ref[...] = h_scr[...]


def _odd_mixer(proj3, convh8, h0, convw, convb, wr_mm, br, wi_mm, bi, lam, *, nb_blk, rows):
    batch, seq, _ = proj3.shape
    assert seq % rows == 0 and batch % nb_blk == 0 and rows % SUBLANES == 0
    t_blocks = seq // rows
    pitch = rows + SUBLANES
    kern = functools.partial(_odd_kernel, nb_blk=nb_blk, rows=rows, carry_hist=t_blocks > 1)
    full = lambda shape: pl.BlockSpec(shape, lambda b, i: (0,) * len(shape))
    return pl.pallas_call(
        kern,
        grid=(batch // nb_blk, t_blocks),
        in_specs=[
            pl.BlockSpec((nb_blk, rows, 2 * C_WIDTH), lambda b, i: (b, i, 0)),
            pl.BlockSpec((nb_blk, SUBLANES, C_WIDTH), lambda b, i: (b, 0, 0)),
            pl.BlockSpec((nb_blk, C_WIDTH), lambda b, i: (b, 0)),
            full((4, C_WIDTH)), full((1, C_WIDTH)),
            full((C_HEADS, C_HW, C_HW)), full((1, C_WIDTH)),
            full((C_HEADS, C_HW, C_HW)), full((1, C_WIDTH)),
            full((1, C_WIDTH)),
        ],
        out_specs=[
            pl.BlockSpec((nb_blk, rows, C_WIDTH), lambda b, i: (b, i, 0)),
            pl.BlockSpec((nb_blk, C_WIDTH), lambda b, i: (b, 0)),
        ],
        out_shape=[jax.ShapeDtypeStruct((batch, seq, C_WIDTH), F32),
                   jax.ShapeDtypeStruct((batch, C_WIDTH), F32)],
        scratch_shapes=[
            pltpu.VMEM((nb_blk, SUBLANES + rows, C_WIDTH), F32),
            pltpu.VMEM((C_WIDTH // LANES, nb_blk * pitch, LANES), F32),
            pltpu.VMEM((C_WIDTH // LANES, nb_blk * pitch, LANES), F32),
            pltpu.VMEM((C_WIDTH // LANES, nb_blk * pitch, LANES), F32),
            pltpu.VMEM((nb_blk, C_WIDTH), F32),
        ],
        compiler_params=_cparams(("parallel", "arbitrary")),
        name="odd_mixer",
    )(proj3, convh8, h0, convw, convb, wr_mm, br, wi_mm, bi, lam)


def _route(x1, rwt_ref, rb_ref):
    logits = lax.dot_general(rwt_ref[...], x1, (((1,), (1,)), ((), ())),
                             precision=HIGHEST, preferred_element_type=F32)
    sc = _sigmoid(logits)
    bz = sc + rb_ref[...]
    row = lambda arr, e: arr[e:e + 1, :]
    best = None
    gidx = None
    for g in range(N_GROUPS):
        r = [row(bz, EPG * g + k) for k in range(EPG)]
        top2 = None
        for a, b in PAIRS:
            s = r[a] + r[b]
            top2 = s if top2 is None else jnp.maximum(top2, s)
        if best is None:
            best, gidx = top2, jnp.zeros(top2.shape, I32)
        else:
            upd = top2 > best
            gidx = jnp.where(upd, g, gidx)
            best = jnp.where(upd, top2, best)
    sb, ss = [], []
    for k in range(EPG):
        vb, vs = row(bz, k), row(sc, k)
        for g in range(1, N_GROUPS):
            vb = jnp.where(gidx == g, row(bz, EPG * g + k), vb)
            vs = jnp.where(gidx == g, row(sc, EPG * g + k), vs)
        sb.append(vb)
        ss.append(vs)
    m1, i1 = sb[0], jnp.zeros(sb[0].shape, I32)
    for k in range(1, EPG):
        upd = sb[k] > m1
        i1 = jnp.where(upd, k, i1)
        m1 = jnp.where(upd, sb[k], m1)
    m2, i2 = None, None
    for k in range(EPG):
        cand = jnp.where(i1 == k, -jnp.inf, sb[k])
        if m2 is None:
            m2, i2 = cand, jnp.zeros(cand.shape, I32)
        else:
            upd = cand > m2
            i2 = jnp.where(upd, k, i2)
            m2 = jnp.where(upd, cand, m2)
    lo = jnp.minimum(i1, i2)
    hi = jnp.maximum(i1, i2)
    pair = jnp.where(lo == 0, 0, jnp.where(lo == 1, 3, 5)) + hi - lo - 1
    s_lo, s_hi = ss[0], ss[0]
    for k in range(1, EPG):
        s_lo = jnp.where(lo == k, ss[k], s_lo)
        s_hi = jnp.where(hi == k, ss[k], s_hi)
    den = s_lo + s_hi
    return gidx * len(PAIRS) + pair, s_lo / den, s_hi / den


def _mix_out_kernel(ap_ref, as_ref, w_ref, x_ref, g_ref, b_ref, rwt_ref, rb_ref, upper_ref,
                    x1e_ref, key_ref, cnt_ref, run_scr, *, np_tiles):
    i = pl.program_id(0)
    tm = x_ref.shape[0]

    @pl.when(i == 0)
    def _():
        run_scr[...] = jnp.zeros(run_scr.shape, F32)

    def finish(a_ref):
        hmix = jnp.dot(a_ref[...].astype(MM), w_ref[...], preferred_element_type=F32)
        x1 = _layer_norm(ALPHA * x_ref[...] + hmix, g_ref[...], b_ref[...])
        cls, wa, wb = _route(x1, rwt_ref, rb_ref)
        crow = lax.broadcasted_iota(I32, (CLS_ROWS, tm), 0)
        onehot = (crow == cls).astype(F32)
        prefix = jnp.dot(onehot.astype(MM), upper_ref[...], preferred_element_type=F32)
        run = run_scr[:, 0:1]
        rank = jnp.sum(onehot * (prefix + (run - 1.0)), axis=0, keepdims=True)
        run_new = run + jnp.sum(onehot, axis=1, keepdims=True)
        run_scr[...] = jnp.broadcast_to(run_new, run_scr.shape)
        cnt_ref[...] = jnp.broadcast_to(run_new, cnt_ref.shape).astype(I32)
        key_ref[...] = cls * KEY_CLS + rank.astype(I32)
        wrow = lax.broadcasted_iota(I32, (LANES, tm), 0)
        wpad = jnp.where(wrow == 0, wa, jnp.where(wrow == 1, wb, 0.0))
        x1e_ref[:, 0:D_MODEL] = x1
        x1e_ref[:, D_MODEL:D_MODEL + LANES] = wpad.T

    @pl.when(i < np_tiles)
    def _():
        finish(ap_ref)

    @pl.when(i >= np_tiles)
    def _():
        finish(as_ref)


def _mix_out(a_p, a_s, w_mm, x, g, b, rwt, rb, tm):
    n = x.shape[0]
    np_tiles = a_p.shape[0] // tm
    ns_tiles = a_s.shape[0] // tm
    assert (np_tiles + ns_tiles) * tm == n
    kern = functools.partial(_mix_out_kernel, np_tiles=np_tiles)
    full = lambda shape: pl.BlockSpec(shape, lambda i: (0,) * len(shape))
    upper = (jnp.arange(tm)[:, None] <= jnp.arange(tm)[None, :]).astype(MM)
    return pl.pallas_call(
        kern,
        grid=(np_tiles + ns_tiles,),
        in_specs=[
            pl.BlockSpec((tm, D_MODEL), lambda i: (jnp.minimum(i, np_tiles - 1), 0)),
            pl.BlockSpec((tm, D_MODEL), lambda i: (jnp.maximum(i - np_tiles, 0), 0)),
            full((D_MODEL, D_MODEL)),
            pl.BlockSpec((tm, D_MODEL), lambda i: (i, 0)),
            full((1, D_MODEL)), full((1, D_MODEL)),
            full((N_EXPERTS, D_MODEL)), full((N_EXPERTS, 1)),
            full((tm, tm)),
        ],
        out_specs=[
            pl.BlockSpec((tm, X1E_W), lambda i: (i, 0)),
            pl.BlockSpec((1, tm), lambda i: (0, i)),
            full((CLS_ROWS, LANES)),
        ],
        out_shape=[jax.ShapeDtypeStruct((n, X1E_W), F32),
                   jax.ShapeDtypeStruct((1, n), I32),
                   jax.ShapeDtypeStruct((CLS_ROWS, LANES), I32)],
        scratch_shapes=[pltpu.VMEM((CLS_ROWS, LANES), F32)],
        compiler_params=_cparams(("arbitrary",)),
        name="mix_out",
    )(a_p, a_s, w_mm, x, g, b, rwt, rb, upper)


def _moe_kernel(nvalid_ref, chga_ref, chgb_ref, ea_ref, eb_ref, pos_ref,
                x_hbm, wga_ref, wua_ref, wda_ref, wgb_ref, wub_ref, wdb_ref,
                y_hbm, xbuf0, xbuf1, obuf0, obuf1, ga_scr, ua_scr, da_scr, gb_scr, ub_scr, db_scr,
                rowsrc, gsem, ssem, *, tm, n_tok):
    t = pl.program_id(0)
    xbufs, obufs = (xbuf0, xbuf1), (obuf0, obuf1)

    def gather_copy(tok, r, sl):
        return pltpu.make_async_copy(x_hbm.at[pl.ds(tok, 1)], xbufs[sl].at[pl.ds(r, 1)], gsem.at[sl])

    def scatter_copy(tok, r, sl):
        return pltpu.make_async_copy(obufs[sl].at[pl.ds(r, 1)], y_hbm.at[pl.ds(tok, 1)], ssem.at[sl])

    def src_token(tile, n_tile, r):
        return rowsrc[jnp.where(r < n_tile, tile * tm + r, 0)]

    def issue_gather(tile, sl, r):
        gather_copy(src_token(tile, nvalid_ref[tile], r), r, sl).start()

    def issue_scatter(tile, n_tile, sl, r):
        dst = jnp.where(r < n_tile, src_token(tile, n_tile, r), n_tok + sl * tm + r)
        scatter_copy(dst, r, sl).start()

    def wait_gather(sl):
        pltpu.make_async_copy(x_hbm.at[pl.ds(0, tm)], xbufs[sl], gsem.at[sl]).wait()

    def wait_scatter(sl):
        pltpu.make_async_copy(obufs[sl], y_hbm.at[pl.ds(0, tm)], ssem.at[sl]).wait()

    @pl.when(t == 0)
    def _():
        def build(i, carry):
            rowsrc[pos_ref[i]] = i
            return carry
        lax.fori_loop(0, n_tok, build, 0, unroll=16)

        def body(r, carry):
            issue_gather(0, 0, r)
            return carry
        lax.fori_loop(0, tm, body, 0, unroll=8)
        obuf0[...] = jnp.zeros(obuf0.shape, obuf0.dtype)
        obuf1[...] = jnp.zeros(obuf1.shape, obuf1.dtype)
        pltpu.make_async_copy(obuf0, y_hbm.at[pl.ds(n_tok, tm)], ssem.at[0]).start()

    @pl.when(chga_ref[t] == 1)
    def _():
        ga_scr[...] = wga_ref[0, 0].astype(MM)
        ua_scr[...] = wua_ref[0, 0].astype(MM)
        da_scr[...] = wda_ref[0, 0].astype(MM)

    @pl.when(chgb_ref[t] == 1)
    def _():
        gb_scr[...] = wgb_ref[0, 0].astype(MM)
        ub_scr[...] = wub_ref[0, 0].astype(MM)
        db_scr[...] = wdb_ref[0, 0].astype(MM)

    prev = jnp.maximum(t - 1, 0)
    n_prev = jnp.where(t > 0, nvalid_ref[prev], 0)
    n_cur = nvalid_ref[t]

    def tile_block(sl):
        wait_gather(sl)
        wait_scatter(sl)
        x = xbufs[sl][:, 0:D_MODEL].astype(MM)

        def expert(wg, wu, wd, wrow):
            gate = jnp.dot(x, wg[...], preferred_element_type=F32)
            up = jnp.dot(x, wu[...], preferred_element_type=F32)
            hid = (_silu(gate) * up).astype(MM)
            return jnp.dot(hid, wd[...], preferred_element_type=F32) * wrow

        obufs[sl][...] = (expert(ga_scr, ua_scr, da_scr, xbufs[sl][:, D_MODEL:D_MODEL + 1])
                          + expert(gb_scr, ub_scr, db_scr, xbufs[sl][:, D_MODEL + 1:D_MODEL + 2]))
        for r in range(tm):
            issue_gather(t + 1, 1 - sl, r)
            issue_scatter(prev, n_prev, 1 - sl, r)

    @pl.when(jnp.logical_and(n_cur > 0, t % 2 == 0))
    def _():
        tile_block(0)

    @pl.when(jnp.logical_and(n_cur > 0, t % 2 == 1))
    def _():
        tile_block(1)

    def drain(sl):
        wait_gather(sl)

        def body(r, carry):
            issue_scatter(prev, n_prev, 1 - sl, r)
            return carry
        lax.fori_loop(0, tm, body, 0, unroll=8)
        wait_scatter(sl)
        wait_scatter(1 - sl)

    @pl.when(jnp.logical_and(jnp.logical_and(n_cur == 0, n_prev > 0), t % 2 == 0))
    def _():
        drain(0)

    @pl.when(jnp.logical_and(jnp.logical_and(n_cur == 0, n_prev > 0), t % 2 == 1))
    def _():
        drain(1)


def _moe(x1e, key, cnt, w_gate, w_up, w_down, layer, tm):
    n = x1e.shape[0]
    n_tiles = n // tm + N_CLASSES
    n_rows = n_tiles * tm
    counts = cnt[:N_CLASSES, 0]
    tiles_c = (counts + tm - 1) // tm
    tile_end_c = jnp.cumsum(tiles_c)
    tile_start_c = tile_end_c - tiles_c
    total = tile_end_c[-1]
    tid = jnp.arange(n_tiles, dtype=I32)
    t_eff = jnp.minimum(tid, total - 1)
    tile_cls = jnp.minimum(jnp.sum((tile_end_c[None, :] <= t_eff[:, None]).astype(I32), axis=1),
                           N_CLASSES - 1)
    sel = (tile_cls[:, None] == jnp.arange(N_CLASSES, dtype=I32)[None, :]).astype(I32)
    left = jnp.sum(sel * counts[None, :], axis=1) - (tid - jnp.sum(sel * tile_start_c[None, :], axis=1)) * tm
    nvalid = jnp.where(tid < total, jnp.clip(left, 0, tm), 0).astype(I32)
    pair_id = tile_cls % len(PAIRS)
    pair_lo = sum(jnp.where(pair_id == k, p[0], 0) for k, p in enumerate(PAIRS))
    pair_hi = sum(jnp.where(pair_id == k, p[1], 0) for k, p in enumerate(PAIRS))
    ea = (EPG * (tile_cls // len(PAIRS)) + pair_lo).astype(I32)
    eb = (EPG * (tile_cls // len(PAIRS)) + pair_hi).astype(I32)
    first = tid == 0
    chga = jnp.logical_or(first, ea != jnp.roll(ea, 1)).astype(I32)
    chgb = jnp.logical_or(first, eb != jnp.roll(eb, 1)).astype(I32)
    key = key.reshape(n)
    tok_cls = key // KEY_CLS
    tok_sel = (tok_cls[:, None] == jnp.arange(N_CLASSES, dtype=I32)[None, :]).astype(I32)
    pos = (jnp.sum(tok_sel * tile_start_c[None, :], axis=1) * tm + key % KEY_CLS).astype(I32)

    kern = functools.partial(_moe_kernel, tm=tm, n_tok=n)
    wspec_a = lambda shape: pl.BlockSpec(
        shape, lambda t, nv, ca, cb, ea_, eb_, ps: (layer, ea_[t], 0, 0))
    wspec_b = lambda shape: pl.BlockSpec(
        shape, lambda t, nv, ca, cb, ea_, eb_, ps: (layer, eb_[t], 0, 0))
    gu = (1, 1, D_MODEL, D_EXPERT)
    dn = (1, 1, D_EXPERT, D_MODEL)
    grid_spec = pltpu.PrefetchScalarGridSpec(
        num_scalar_prefetch=6,
        grid=(n_tiles,),
        in_specs=[pl.BlockSpec(memory_space=pl.ANY),
                  wspec_a(gu), wspec_a(gu), wspec_a(dn),
                  wspec_b(gu), wspec_b(gu), wspec_b(dn)],
        out_specs=pl.BlockSpec(memory_space=pl.ANY),
        scratch_shapes=[
            pltpu.VMEM((tm, X1E_W), F32), pltpu.VMEM((tm, X1E_W), F32),
            pltpu.VMEM((tm, D_MODEL), F32), pltpu.VMEM((tm, D_MODEL), F32),
            pltpu.VMEM((D_MODEL, D_EXPERT), MM), pltpu.VMEM((D_MODEL, D_EXPERT), MM),
            pltpu.VMEM((D_EXPERT, D_MODEL), MM),
            pltpu.VMEM((D_MODEL, D_EXPERT), MM), pltpu.VMEM((D_MODEL, D_EXPERT), MM),
            pltpu.VMEM((D_EXPERT, D_MODEL), MM),
            pltpu.SMEM((n_rows,), I32),
            pltpu.SemaphoreType.DMA((2,)),
            pltpu.SemaphoreType.DMA((2,)),
        ],
    )
    return pl.pallas_call(
        kern,
        grid_spec=grid_spec,
        out_shape=jax.ShapeDtypeStruct((n + 2 * tm, D_MODEL), F32),
        compiler_params=_cparams(("arbitrary",)),
        name="moe",
    )(nvalid, chga, chgb, ea, eb, pos,
      x1e, w_gate, w_up, w_down, w_gate, w_up, w_down)


def _ffn_out_kernel(x_ref, y_ref, g_ref, b_ref, o_ref):
    o_ref[...] = _layer_norm(ALPHA * x_ref[...] + y_ref[...], g_ref[...], b_ref[...])


def _ffn_out(x1, y, g, b, row0, nrows, tm):
    off = row0 // tm
    rowspec = pl.BlockSpec((tm, D_MODEL), lambda i: (i + off, 0))
    vec = pl.BlockSpec((1, D_MODEL), lambda i: (0, 0))
    return pl.pallas_call(
        _ffn_out_kernel,
        grid=(nrows // tm,),
        in_specs=[rowspec, rowspec, vec, vec],
        out_specs=pl.BlockSpec((tm, D_MODEL), lambda i: (i, 0)),
        out_shape=jax.ShapeDtypeStruct((nrows, D_MODEL), F32),
        compiler_params=_cparams(("parallel",)),
        name="ffn_out",
    )(x1, y, g, b)


def _pad_hist(hist, rows):
    b, r, c = hist.shape
    return jnp.concatenate([jnp.zeros((b, rows - r, c), hist.dtype), hist], axis=1)


def _new_hist(hist, cur, c0, c1, keep):
    t = cur.shape[1]
    if t >= keep:
        return cur[:, t - keep:, c0:c1]
    return jnp.concatenate([hist[:, t:], cur[:, :, c0:c1]], axis=1)


def kernel(x_prompt, x_sample, state_delta, state_delta_conv, state_pool, state_lru, state_lru_conv,
           ab_w_in, a_conv_w, a_log_decay, a_dt_bias, a_norm_w, b_w_group, b_scale, ab_w_out,
           c_w_in, c_conv_w, c_conv_b, c_w_r, c_b_r, c_w_i, c_b_i, c_lambda, c_w_out,
           ln_mix_g, ln_mix_b, ln_ffn_g, ln_ffn_b, router_w, router_bias,
           moe_w_gate, moe_w_up, moe_w_down):
    bp, tp, d = x_prompt.shape
    bs, ts, _ = x_sample.shape
    n_p, n_s = bp * tp, bs * ts
    n = n_p + n_s
    tm = TOKEN_TILE
    tmix = MIX_TILE
    assert n_p % tmix == 0 and n_s % tmix == 0 and tmix % tm == 0

    x = jnp.concatenate([x_prompt.reshape(n_p, d), x_sample.reshape(n_s, d)], axis=0)
    groups = (
        dict(row0=0, batch=bp, seq=tp, hist_valid=0, fresh=True,
             ev=dict(nb_blk=1, rows=min(tp, 256)), od=dict(nb_blk=min(bp, 8), rows=min(tp, 64))),
        dict(row0=n_p, batch=bs, seq=ts, hist_valid=B_HIST, fresh=False,
             ev=dict(nb_blk=min(bs, 8), rows=ts), od=dict(nb_blk=min(bs, 32), rows=ts)),
    )
    rwt = router_w.T
    rb = router_bias.reshape(N_EXPERTS, 1)
    row = lambda v: v.reshape(1, -1)

    new = {k: ([], []) for k in ("delta", "dconv", "pool", "lru", "lconv")}
    for layer in range(DEPTH):
        j = layer // 2
        mixed = []
        if layer % 2 == 0:
            w = ab_w_in[j]
            c1 = EV_Z0 + A_V
            w_perm = jnp.concatenate(
                [w[:, :c1], w[:, c1 + 2 * A_HEADS:], w[:, c1:c1 + 2 * A_HEADS],
                 jnp.zeros((d, LANES - 2 * A_HEADS), w.dtype)], axis=1).astype(MM)
            gvec = jnp.zeros((2, LANES), F32)
            gvec = gvec.at[0, A_HEADS:2 * A_HEADS].set(a_log_decay[j])
            gvec = gvec.at[1, A_HEADS:2 * A_HEADS].set(a_dt_bias[j])
            for gi, g in enumerate(groups):
                b_, t_ = g["batch"], g["seq"]
                proj = _proj(x, w_perm, g["row0"], b_ * t_, tm)
                p3 = proj.reshape(b_, t_, EV_W)
                if g["fresh"]:
                    dconv = jnp.zeros((b_, 3, A_CONV_CH), F32)
                    delta = jnp.zeros((b_, A_HEADS, A_DK, A_DV), F32)
                    pool = jnp.zeros((b_, B_HIST, B_WIDTH), F32)
                else:
                    dconv, delta, pool = state_delta_conv[j], state_delta[j], state_pool[j]
                o, s_new = _even_mixer(
                    proj, _pad_hist(dconv, SUBLANES), delta, _pad_hist(pool, 2 * SUBLANES),
                    a_conv_w[j], gvec, row(a_norm_w[j]), b_w_group[j].astype(MM), row(b_scale[j]),
                    batch=b_, seq=t_, hist_valid=g["hist_valid"], **g["ev"])
                mixed.append(o)
                new["delta"][gi].append(s_new)
                new["dconv"][gi].append(_new_hist(dconv, p3, 0, A_CONV_CH, 3))
                new["pool"][gi].append(_new_hist(pool, p3, EV_U0, EV_U0 + B_WIDTH, B_HIST))
            w_out = ab_w_out[j].astype(MM)
        else:
            w_mm = c_w_in[j].astype(MM)
            for gi, g in enumerate(groups):
                b_, t_ = g["batch"], g["seq"]
                p3 = _proj(x, w_mm, g["row0"], b_ * t_, tm).reshape(b_, t_, 2 * C_WIDTH)
                if g["fresh"]:
                    lconv = jnp.zeros((b_, 3, C_WIDTH), F32)
                    lru = jnp.zeros((b_, C_WIDTH), F32)
                else:
                    lconv, lru = state_lru_conv[j], state_lru[j]
                o3, h_last = _odd_mixer(
                    p3, _pad_hist(lconv, SUBLANES), lru, c_conv_w[j], row(c_conv_b[j]),
                    c_w_r[j].astype(MM), row(c_b_r[j]), c_w_i[j].astype(MM), row(c_b_i[j]),
                    row(c_lambda[j]), **g["od"])
                mixed.append(o3.reshape(b_ * t_, C_WIDTH))
                new["lru"][gi].append(h_last)
                new["lconv"][gi].append(_new_hist(lconv, p3, C_WIDTH, 2 * C_WIDTH, 3))
            w_out = c_w_out[j].astype(MM)

        x1e, key, cnt = _mix_out(mixed[0], mixed[1], w_out, x, row(ln_mix_g[layer]),
                                 row(ln_mix_b[layer]), rwt, rb, tmix)
        y = _moe(x1e, key, cnt, moe_w_gate, moe_w_up, moe_w_down, layer, tm)
        if layer < DEPTH - 1:
            x = _ffn_out(x1e, y, row(ln_ffn_g[layer]), row(ln_ffn_b[layer]), 0, n, tmix)
        else:
            y_p = _ffn_out(x1e, y, row(ln_ffn_g[layer]), row(ln_ffn_b[layer]), 0, n_p, tmix)
            y_s = _ffn_out(x1e, y, row(ln_ffn_g[layer]), row(ln_ffn_b[layer]), n_p, n_s, tmix)

    stack = lambda key, gi: jnp.stack(new[key][gi])
    return (y_p.reshape(bp, tp, d), y_s.reshape(bs, ts, d),
            stack("delta", 0), stack("dconv", 0), stack("pool", 0), stack("lru", 0), stack("lconv", 0),
            stack("delta", 1), stack("dconv", 1), stack("pool", 1), stack("lru", 1), stack("lconv", 1))
```

```python
import functools

import jax
import jax.numpy as jnp
from jax import lax
from jax.experimental import pallas as pl
from jax.experimental.pallas import tpu as pltpu

F32 = jnp.float32
I32 = jnp.int32
MM = jnp.bfloat16
HIGHEST = lax.Precision.HIGHEST

D_MODEL = 1024
DEPTH = 4
A_HEADS = 4
A_DK = 128
A_DV = 128
A_QK = A_HEADS * A_DK
A_V = A_HEADS * A_DV
A_CONV_CH = 2 * A_QK + A_V
A_CHUNK = 64
B_GROUPS = 4
B_GW = 128
B_WINDOWS = (2, 4, 8, 16)
B_HIST = 15
B_WIDTH = B_GROUPS * B_GW
C_WIDTH = D_MODEL
C_HEADS = 4
C_HW = C_WIDTH // C_HEADS
C_GATE = 8.0
N_EXPERTS = 16
N_GROUPS = 4
EPG = N_EXPERTS // N_GROUPS
D_EXPERT = 512
ALPHA = (2 * DEPTH) ** 0.25
LN_EPS = 1e-5
RMS_EPS = 1e-6

LANES = 128
SUBLANES = 8
VMEM_LIMIT = 56 * 1024 * 1024

EV_Z0 = A_CONV_CH
EV_U0 = EV_Z0 + A_V
EV_BA0 = EV_U0 + B_WIDTH
EV_W = EV_BA0 + LANES

PAIRS = ((0, 1), (0, 2), (0, 3), (1, 2), (1, 3), (2, 3))
N_CLASSES = N_GROUPS * len(PAIRS)

TOKEN_TILE = 256
MIX_TILE = 512
CLS_ROWS = 32
KEY_CLS = 1 << 16
X1E_W = D_MODEL + LANES


def _cparams(sem):
    return pltpu.CompilerParams(dimension_semantics=sem, vmem_limit_bytes=VMEM_LIMIT)


def _dot(a, b):
    return jnp.dot(a.astype(MM), b.astype(MM), preferred_element_type=F32)


def _dot_nt(a, b):
    return lax.dot_general(a.astype(MM), b.astype(MM), (((1,), (1,)), ((), ())),
                           preferred_element_type=F32)


def _dot_tn(a, b):
    return lax.dot_general(a.astype(MM), b.astype(MM), (((0,), (0,)), ((), ())),
                           preferred_element_type=F32)


def _sigmoid(x):
    return 1.0 / (1.0 + jnp.exp(-x))


def _sigmoid_t(x):
    return 0.5 * jnp.tanh(0.5 * x) + 0.5


def _silu(x):
    return x * _sigmoid(x)


def _softplus(x):
    return jnp.maximum(x, 0.0) + jnp.log1p(jnp.exp(-jnp.abs(x)))


def _gelu_tanh(x):
    return x * (0.5 * (1.0 + jnp.tanh(0.7978845608028654 * (x + 0.044715 * (x * x * x)))))


def _layer_norm(v, g, b):
    mu = jnp.mean(v, axis=-1, keepdims=True)
    d = v - mu
    var = jnp.mean(d * d, axis=-1, keepdims=True)
    return d * lax.rsqrt(var + LN_EPS) * g + b


def _proj_kernel(x_ref, w_ref, o_ref):
    o_ref[...] = jnp.dot(x_ref[...].astype(MM), w_ref[...], preferred_element_type=F32)


def _proj(x, w_mm, row0, nrows, tm):
    k, width = w_mm.shape
    off = row0 // tm
    return pl.pallas_call(
        _proj_kernel,
        grid=(nrows // tm,),
        in_specs=[pl.BlockSpec((tm, k), lambda i: (i + off, 0)),
                  pl.BlockSpec((k, width), lambda i: (0, 0))],
        out_specs=pl.BlockSpec((tm, width), lambda i: (i, 0)),
        out_shape=jax.ShapeDtypeStruct((nrows, width), F32),
        compiler_params=_cparams(("parallel",)),
        name="proj",
    )(x, w_mm)


def _even_kernel(p_ref, convh_ref, s0_ref, poolh_ref, convw_ref, gvec_ref, normw_ref,
                 wgrp_ref, scale_ref, o_ref, snew_ref, ext_scr, qkv_scr, pext_scr, s_scr,
                 *, nb_blk, rows, chunk, hist_valid, carry_hist):
    i = pl.program_id(1)
    n_chunks = rows // chunk
    n_neumann = max((chunk - 1).bit_length() - 1, 0)
    hb = 2 * SUBLANES

    @pl.when(i == 0)
    def _():
        s_scr[...] = s0_ref[...]
        ext_scr[:, 0:SUBLANES, :] = convh_ref[...]
        pext_scr[:, 0:hb, :] = poolh_ref[...]

    rid = lax.broadcasted_iota(I32, (chunk, chunk), 0)
    cid = lax.broadcasted_iota(I32, (chunk, chunk), 1)
    incl = rid >= cid
    strict = rid > cid
    eye = (rid == cid).astype(F32)
    ltri = incl.astype(F32)
    lane = lax.broadcasted_iota(I32, (chunk, LANES), 1)
    trow = lax.broadcasted_iota(I32, (rows, LANES), 0)
    pos = (i * rows + trow + (1 + hist_valid)).astype(F32)
    neg_decay_rate = -jnp.exp(gvec_ref[0:1, :])
    dt_bias = gvec_ref[1:2, :]

    for nb in range(nb_blk):
        r0 = nb * rows
        ext_scr[nb, SUBLANES:SUBLANES + rows, :] = p_ref[r0:r0 + rows, 0:A_CONV_CH]
        for ct in range(A_CONV_CH // LANES):
            cs = slice(ct * LANES, (ct + 1) * LANES)
            acc = None
            for j in range(4):
                lo = SUBLANES - 3 + j
                term = ext_scr[nb, lo:lo + rows, cs] * convw_ref[j:j + 1, cs]
                acc = term if acc is None else acc + term
            t = _silu(acc)
            if ct < 2 * A_HEADS:
                t = t * lax.rsqrt(jnp.sum(t * t, axis=-1, keepdims=True) + 1e-6)
                if ct < A_HEADS:
                    t = t * (A_DK ** -0.5)
            qkv_scr[r0:r0 + rows, cs] = t
        if carry_hist:
            ext_scr[nb, 0:SUBLANES, :] = ext_scr[nb, rows:rows + SUBLANES, :]

        pext_scr[nb, hb:hb + rows, :] = p_ref[r0:r0 + rows, EV_U0:EV_U0 + B_WIDTH]
        for gi, w in enumerate(B_WINDOWS):
            cs = slice(gi * B_GW, (gi + 1) * B_GW)
            cur = pext_scr[nb, hb:hb + rows, cs]
            tot = cur
            for j in range(1, w):
                tot = tot + pext_scr[nb, hb - j:hb - j + rows, cs]
            pooled = tot / jnp.minimum(pos, float(w)) - cur
            ob = _dot(pooled, wgrp_ref[gi]) * scale_ref[:, cs]
            o_ref[r0:r0 + rows, A_V + gi * B_GW:A_V + (gi + 1) * B_GW] = ob
        if carry_hist:
            pext_scr[nb, 0:hb, :] = pext_scr[nb, rows:rows + hb, :]

    chunks = [(nb, c) for nb in range(nb_blk) for c in range(n_chunks)]
    probs = [(nb, c, h) for nb, c in chunks for h in range(A_HEADS)]
    rs = {(nb, c): slice(nb * rows + c * chunk, nb * rows + (c + 1) * chunk) for nb, c in chunks}
    gcol = lambda arr, h: arr[:, A_HEADS + h:A_HEADS + h + 1]
    qf = lambda p: qkv_scr[rs[p[:2]], p[2] * LANES:(p[2] + 1) * LANES]
    kf = lambda p: qkv_scr[rs[p[:2]], A_QK + p[2] * LANES:A_QK + (p[2] + 1) * LANES]
    vf = lambda p: qkv_scr[rs[p[:2]], 2 * A_QK + p[2] * LANES:2 * A_QK + (p[2] + 1) * LANES]

    ba = {ck: p_ref[rs[ck], EV_BA0:EV_BA0 + LANES] for ck in chunks}
    beta_full = {ck: _sigmoid(ba[ck]) for ck in chunks}
    g_full = {ck: neg_decay_rate * _softplus(ba[ck] + dt_bias) for ck in chunks}
    gc_full = {ck: jnp.dot(ltri, g_full[ck], precision=HIGHEST, preferred_element_type=F32)
               for ck in chunks}
    eg_full = {ck: jnp.exp(gc_full[ck]) for ck in chunks}
    gl_full = {ck: gc_full[ck][chunk - 1:chunk, :] for ck in chunks}
    ekd_full = {ck: jnp.exp(gl_full[ck] - gc_full[ck]) for ck in chunks}
    egl_full = {ck: jnp.exp(gl_full[ck]) for ck in chunks}
    sel = [(lane == A_HEADS + h).astype(F32) for h in range(A_HEADS)]
    grow = {p: lax.dot_general(sel[p[2]], gc_full[p[:2]], (((1,), (1,)), ((), ())),
                               precision=HIGHEST, preferred_element_type=F32) for p in probs}
    decay = {p: jnp.where(incl, jnp.exp(jnp.where(incl, gcol(gc_full[p[:2]], p[2]) - grow[p], 0.0)), 0.0)
             for p in probs}
    beta = {p: beta_full[p[:2]][:, p[2]:p[2] + 1] for p in probs}
    kk = {p: _dot_nt(kf(p) * beta[p], kf(p)) for p in probs}
    qk = {p: _dot_nt(qf(p), kf(p)) for p in probs}
    a_intra = {p: jnp.where(incl, qk[p] * decay[p], 0.0) for p in probs}
    power = {p: -jnp.where(strict, kk[p] * decay[p], 0.0) for p in probs}
    tinv = {p: eye + power[p] for p in probs}
    for _ in range(n_neumann):
        power = {p: _dot(power[p], power[p]) for p in probs}
        tinv = {p: tinv[p] + _dot(tinv[p], power[p]) for p in probs}
    u_in = {p: _dot(tinv[p], vf(p) * beta[p]) for p in probs}
    w_in = {p: _dot(tinv[p], kf(p) * beta[p] * gcol(eg_full[p[:2]], p[2])) for p in probs}

    for c in range(n_chunks):
        cp = [(nb, c, h) for nb in range(nb_blk) for h in range(A_HEADS)]
        s_old = {p: s_scr[p[0], p[2]] for p in cp}
        wq = {p: _dot(jnp.concatenate([w_in[p], qf(p) * gcol(eg_full[p[:2]], p[2])], axis=0), s_old[p])
              for p in cp}
        u_new = {p: u_in[p] - wq[p][0:chunk] for p in cp}
        au = {p: _dot(a_intra[p], u_new[p]) for p in cp}
        ku = {p: _dot_tn(kf(p) * gcol(ekd_full[p[:2]], p[2]), u_new[p]) for p in cp}
        for p in cp:
            nb, _, h = p
            s_scr[nb, h] = s_old[p] * gcol(egl_full[p[:2]], h) + ku[p]
            o = wq[p][chunk:2 * chunk] + au[p]
            o = o * lax.rsqrt(jnp.mean(o * o, axis=-1, keepdims=True) + RMS_EPS) * normw_ref[...]
            z = p_ref[rs[p[:2]], EV_Z0 + h * LANES:EV_Z0 + (h + 1) * LANES]
            o_ref[rs[p[:2]], h * LANES:(h + 1) * LANES] = o * _silu(z)

    @pl.when(i == pl.num_programs(1) - 1)
    def _():
        snew_ref[...] = s_scr[...]


def _even_mixer(proj, convh8, s0, poolh16, convw, gvec, normw, wgrp_mm, scale,
                *, batch, seq, nb_blk, rows, hist_valid):
    chunk = min(A_CHUNK, seq)
    assert seq % rows == 0 and rows % chunk == 0 and batch % nb_blk == 0
    assert nb_blk == 1 or rows == seq
    t_blocks = seq // rows
    blk = nb_blk * rows
    hb = 2 * SUBLANES
    kern = functools.partial(_even_kernel, nb_blk=nb_blk, rows=rows, chunk=chunk,
                             hist_valid=hist_valid, carry_hist=t_blocks > 1)
    full = lambda shape: pl.BlockSpec(shape, lambda b, i: (0,) * len(shape))
    return pl.pallas_call(
        kern,
        grid=(batch // nb_blk, t_blocks),
        in_specs=[
            pl.BlockSpec((blk, EV_W), lambda b, i: (b * t_blocks + i, 0)),
            pl.BlockSpec((nb_blk, SUBLANES, A_CONV_CH), lambda b, i: (b, 0, 0)),
            pl.BlockSpec((nb_blk, A_HEADS, A_DK, A_DV), lambda b, i: (b, 0, 0, 0)),
            pl.BlockSpec((nb_blk, hb, B_WIDTH), lambda b, i: (b, 0, 0)),
            full((4, A_CONV_CH)), full((2, LANES)), full((1, A_DV)),
            full((B_GROUPS, B_GW, B_GW)), full((1, B_WIDTH)),
        ],
        out_specs=[
            pl.BlockSpec((blk, D_MODEL), lambda b, i: (b * t_blocks + i, 0)),
            pl.BlockSpec((nb_blk, A_HEADS, A_DK, A_DV), lambda b, i: (b, 0, 0, 0)),
        ],
        out_shape=[jax.ShapeDtypeStruct((batch * seq, D_MODEL), F32),
                   jax.ShapeDtypeStruct((batch, A_HEADS, A_DK, A_DV), F32)],
        scratch_shapes=[
            pltpu.VMEM((nb_blk, SUBLANES + rows, A_CONV_CH), F32),
            pltpu.VMEM((blk, A_CONV_CH), F32),
            pltpu.VMEM((nb_blk, hb + rows, B_WIDTH), F32),
            pltpu.VMEM((nb_blk, A_HEADS, A_DK, A_DV), F32),
        ],
        compiler_params=_cparams(("parallel", "arbitrary")),
        name="even_mixer",
    )(proj, convh8, s0, poolh16, convw, gvec, normw, wgrp_mm, scale)


def _odd_kernel(p_ref, convh_ref, h0_ref, convw_ref, convb_ref, wr_ref, br_ref, wi_ref, bi_ref,
                lam_ref, o_ref, hlast_ref, ext_scr, a_scr, b_scr, hs_scr, h_scr,
                *, nb_blk, rows, carry_hist):
    i = pl.program_id(1)
    blk = nb_blk * rows
    pitch = a_scr.shape[1] // nb_blk

    @pl.when(i == 0)
    def _():
        h_scr[...] = h0_ref[...]
        ext_scr[:, 0:SUBLANES, :] = convh_ref[...]

    ext_scr[:, SUBLANES:SUBLANES + rows, :] = p_ref[:, :, C_WIDTH:2 * C_WIDTH]
    log_base = -C_GATE * _softplus(-lam_ref[...])
    for h in range(C_HEADS):
        cs = slice(h * C_HW, (h + 1) * C_HW)
        acc = None
        for j in range(4):
            lo = SUBLANES - 3 + j
            term = ext_scr[:, lo:lo + rows, cs] * convw_ref[j:j + 1, cs]
            acc = term if acc is None else acc + term
        xc = (acc + convb_ref[:, cs]).reshape(blk, C_HW)
        r = _sigmoid_t(_dot(xc, wr_ref[h]) + br_ref[:, cs])
        gi = _sigmoid_t(_dot(xc, wi_ref[h]) + bi_ref[:, cs])
        log_a = r * log_base[:, cs]
        a = jnp.exp(log_a)
        th = jnp.tanh(log_a)
        bt = jnp.sqrt(-2.0 * th) * lax.rsqrt(1.0 - th) * gi * xc
        for lt in range(C_HW // LANES):
            for b in range(nb_blk):
                dst = pl.ds(b * pitch, rows)
                src = slice(b * rows, (b + 1) * rows)
                a_scr[h * (C_HW // LANES) + lt, dst, :] = a[src, lt * LANES:(lt + 1) * LANES]
                b_scr[h * (C_HW // LANES) + lt, dst, :] = bt[src, lt * LANES:(lt + 1) * LANES]
    if carry_hist:
        ext_scr[:, 0:SUBLANES, :] = ext_scr[:, rows:rows + SUBLANES, :]

    n_lt = C_WIDTH // LANES

    def step(t, hcur):
        hnew = []
        for lt in range(n_lt):
            a_t = a_scr[lt, pl.ds(t, nb_blk, stride=pitch), :]
            b_t = b_scr[lt, pl.ds(t, nb_blk, stride=pitch), :]
            hn = a_t * hcur[lt] + b_t
            hs_scr[lt, pl.ds(t, nb_blk, stride=pitch), :] = hn
            hnew.append(hn)
        return tuple(hnew)

    h_init = tuple(h_scr[:, lt * LANES:(lt + 1) * LANES] for lt in range(n_lt))
    h_fin = lax.fori_loop(0, rows, step, h_init, unroll=SUBLANES)
    for lt in range(n_lt):
        cs = slice(lt * LANES, (lt + 1) * LANES)
        h_scr[:, cs] = h_fin[lt]
        for b in range(nb_blk):
            o_ref[b, :, cs] = _gelu_tanh(p_ref[b, :, cs]) * hs_scr[lt, pl.ds(b * pitch, rows), :]

    @pl.when(i == pl.num_programs(1) - 1)
    def _():
        hlast_ref[...] = h_scr[...]


def _odd_mixer(proj3, convh8, h0, convw, convb, wr_mm, br, wi_mm, bi, lam, *, nb_blk, rows):
    batch, seq, _ = proj3.shape
    assert seq % rows == 0 and batch % nb_blk == 0 and rows % SUBLANES == 0
    t_blocks = seq // rows
    pitch = rows + SUBLANES
    kern = functools.partial(_odd_kernel, nb_blk=nb_blk, rows=rows, carry_hist=t_blocks > 1)
    full = lambda shape: pl.BlockSpec(shape, lambda b, i: (0,) * len(shape))
    return pl.pallas_call(
        kern,
        grid=(batch // nb_blk, t_blocks),
        in_specs=[
            pl.BlockSpec((nb_blk, rows, 2 * C_WIDTH), lambda b, i: (b, i, 0)),
            pl.BlockSpec((nb_blk, SUBLANES, C_WIDTH), lambda b, i: (b, 0, 0)),
            pl.BlockSpec((nb_blk, C_WIDTH), lambda b, i: (b, 0)),
            full((4, C_WIDTH)), full((1, C_WIDTH)),
            full((C_HEADS, C_HW, C_HW)), full((1, C_WIDTH)),
            full((C_HEADS, C_HW, C_HW)), full((1, C_WIDTH)),
            full((1, C_WIDTH)),
        ],
        out_specs=[
            pl.BlockSpec((nb_blk, rows, C_WIDTH), lambda b, i: (b, i, 0)),
            pl.BlockSpec((nb_blk, C_WIDTH), lambda b, i: (b, 0)),
        ],
        out_shape=[jax.ShapeDtypeStruct((batch, seq, C_WIDTH), F32),
                   jax.ShapeDtypeStruct((batch, C_WIDTH), F32)],
        scratch_shapes=[
            pltpu.VMEM((nb_blk, SUBLANES + rows, C_WIDTH), F32),
            pltpu.VMEM((C_WIDTH // LANES, nb_blk * pitch, LANES), F32),
            pltpu.VMEM((C_WIDTH // LANES, nb_blk * pitch, LANES), F32),
            pltpu.VMEM((C_WIDTH // LANES, nb_blk * pitch, LANES), F32),
            pltpu.VMEM((nb_blk, C_WIDTH), F32),
        ],
        compiler_params=_cparams(("parallel", "arbitrary")),
        name="odd_mixer",
    )(proj3, convh8, h0, convw, convb, wr_mm, br, wi_mm, bi, lam)


def _route(x1, rwt_ref, rb_ref):
    logits = lax.dot_general(rwt_ref[...], x1, (((1,), (1,)), ((), ())),
                             precision=HIGHEST, preferred_element_type=F32)
    sc = _sigmoid(logits)
    bz = sc + rb_ref[...]
    row = lambda arr, e: arr[e:e + 1, :]
    best = None
    gidx = None
    for g in range(N_GROUPS):
        r = [row(bz, EPG * g + k) for k in range(EPG)]
        top2 = None
        for a, b in PAIRS:
            s = r[a] + r[b]
            top2 = s if top2 is None else jnp.maximum(top2, s)
        if best is None:
            best, gidx = top2, jnp.zeros(top2.shape, I32)
        else:
            upd = top2 > best
            gidx = jnp.where(upd, g, gidx)
            best = jnp.where(upd, top2, best)
    sb, ss = [], []
    for k in range(EPG):
        vb, vs = row(bz, k), row(sc, k)
        for g in range(1, N_GROUPS):
            vb = jnp.where(gidx == g, row(bz, EPG * g + k), vb)
            vs = jnp.where(gidx == g, row(sc, EPG * g + k), vs)
        sb.append(vb)
        ss.append(vs)
    m1, i1 = sb[0], jnp.zeros(sb[0].shape, I32)
    for k in range(1, EPG):
        upd = sb[k] > m1
        i1 = jnp.where(upd, k, i1)
        m1 = jnp.where(upd, sb[k], m1)
    m2, i2 = None, None
    for k in range(EPG):
        cand = jnp.where(i1 == k, -jnp.inf, sb[k])
        if m2 is None:
            m2, i2 = cand, jnp.zeros(cand.shape, I32)
        else:
            upd = cand > m2
            i2 = jnp.where(upd, k, i2)
            m2 = jnp.where(upd, cand, m2)
    lo = jnp.minimum(i1, i2)
    hi = jnp.maximum(i1, i2)
    pair = jnp.where(lo == 0, 0, jnp.where(lo == 1, 3, 5)) + hi - lo - 1
    s_lo, s_hi = ss[0], ss[0]
    for k in range(1, EPG):
        s_lo = jnp.where(lo == k, ss[k], s_lo)
        s_hi = jnp.where(hi == k, ss[k], s_hi)
    den = s_lo + s_hi
    return gidx * len(PAIRS) + pair, s_lo / den, s_hi / den


def _mix_out_kernel(ap_ref, as_ref, w_ref, x_ref, g_ref, b_ref, rwt_ref, rb_ref, upper_ref,
                    x1e_ref, key_ref, cnt_ref, run_scr, *, np_tiles):
    i = pl.program_id(0)
    tm = x_ref.shape[0]

    @pl.when(i == 0)
    def _():
        run_scr[...] = jnp.zeros(run_scr.shape, F32)

    def finish(a_ref):
        hmix = jnp.dot(a_ref[...].astype(MM), w_ref[...], preferred_element_type=F32)
        x1 = _layer_norm(ALPHA * x_ref[...] + hmix, g_ref[...], b_ref[...])
        cls, wa, wb = _route(x1, rwt_ref, rb_ref)
        crow = lax.broadcasted_iota(I32, (CLS_ROWS, tm), 0)
        onehot = (crow == cls).astype(F32)
        prefix = jnp.dot(onehot.astype(MM), upper_ref[...], preferred_element_type=F32)
        run = run_scr[:, 0:1]
        rank = jnp.sum(onehot * (prefix + (run - 1.0)), axis=0, keepdims=True)
        run_new = run + jnp.sum(onehot, axis=1, keepdims=True)
        run_scr[...] = jnp.broadcast_to(run_new, run_scr.shape)
        cnt_ref[...] = jnp.broadcast_to(run_new, cnt_ref.shape).astype(I32)
        key_ref[...] = cls * KEY_CLS + rank.astype(I32)
        wrow = lax.broadcasted_iota(I32, (LANES, tm), 0)
        wpad = jnp.where(wrow == 0, wa, jnp.where(wrow == 1, wb, 0.0))
        x1e_ref[:, 0:D_MODEL] = x1
        x1e_ref[:, D_MODEL:D_MODEL + LANES] = wpad.T

    @pl.when(i < np_tiles)
    def _():
        finish(ap_ref)

    @pl.when(i >= np_tiles)
    def _():
        finish(as_ref)


def _mix_out(a_p, a_s, w_mm, x, g, b, rwt, rb, tm):
    n = a_p.shape[0] + a_s.shape[0]
    np_tiles = a_p.shape[0] // tm
    ns_tiles = a_s.shape[0] // tm
    assert (np_tiles + ns_tiles) * tm == n and x.shape[0] >= n
    kern = functools.partial(_mix_out_kernel, np_tiles=np_tiles)
    full = lambda shape: pl.BlockSpec(shape, lambda i: (0,) * len(shape))
    upper = (jnp.arange(tm)[:, None] <= jnp.arange(tm)[None, :]).astype(MM)
    return pl.pallas_call(
        kern,
        grid=(np_tiles + ns_tiles,),
        in_specs=[
            pl.BlockSpec((tm, D_MODEL), lambda i: (jnp.minimum(i, np_tiles - 1), 0)),
            pl.BlockSpec((tm, D_MODEL), lambda i: (jnp.maximum(i - np_tiles, 0), 0)),
            full((D_MODEL, D_MODEL)),
            pl.BlockSpec((tm, D_MODEL), lambda i: (i, 0)),
            full((1, D_MODEL)), full((1, D_MODEL)),
            full((N_EXPERTS, D_MODEL)), full((N_EXPERTS, 1)),
            full((tm, tm)),
        ],
        out_specs=[
            pl.BlockSpec((tm, X1E_W), lambda i: (i, 0)),
            pl.BlockSpec((1, tm), lambda i: (0, i)),
            full((CLS_ROWS, LANES)),
        ],
        out_shape=[jax.ShapeDtypeStruct((n, X1E_W), F32),
                   jax.ShapeDtypeStruct((1, n), I32),
                   jax.ShapeDtypeStruct((CLS_ROWS, LANES), I32)],
        scratch_shapes=[pltpu.VMEM((CLS_ROWS, LANES), F32)],
        compiler_params=_cparams(("arbitrary",)),
        name="mix_out",
    )(a_p, a_s, w_mm, x, g, b, rwt, rb, upper)


def _moe_kernel(nvalid_ref, chga_ref, chgb_ref, ea_ref, eb_ref, pos_ref,
                x_hbm, g_ref, b_ref, wga_ref, wua_ref, wda_ref, wgb_ref, wub_ref, wdb_ref,
                y_hbm, xbuf0, xbuf1, xbuf2, obuf0, obuf1, obuf2,
                ga_scr, ua_scr, da_scr, gb_scr, ub_scr, db_scr, rowsrc, gsem, ssem, *, tm, n_tok):
    t = pl.program_id(0)
    nt = pl.num_programs(0)
    xbufs, obufs = (xbuf0, xbuf1, xbuf2), (obuf0, obuf1, obuf2)

    def gather_copy(tok, r, sl):
        return pltpu.make_async_copy(x_hbm.at[pl.ds(tok, 1)], xbufs[sl].at[pl.ds(r, 1)], gsem.at[sl])

    def scatter_copy(tok, r, sl):
        return pltpu.make_async_copy(obufs[sl].at[pl.ds(r, 1)], y_hbm.at[pl.ds(tok, 1)], ssem.at[sl])

    def src_token(tile, n_tile, r):
        return rowsrc[jnp.where(r < n_tile, tile * tm + r, 0)]

    def tile_rows(tile):
        return jnp.where(tile < nt, nvalid_ref[jnp.minimum(tile, nt - 1)], 0)

    def issue_gather(tile, n_tile, sl, r):
        gather_copy(src_token(tile, n_tile, r), r, sl).start()

    def issue_scatter(tile, n_tile, sl, r):
        dst = jnp.where(r < n_tile, src_token(tile, n_tile, r), n_tok + sl * tm + r)
        scatter_copy(dst, r, sl).start()

    def wait_gather(sl):
        pltpu.make_async_copy(x_hbm.at[pl.ds(0, tm)], xbufs[sl], gsem.at[sl]).wait()

    def wait_scatter(sl):
        pltpu.make_async_copy(obufs[sl], y_hbm.at[pl.ds(0, tm)], ssem.at[sl]).wait()

    @pl.when(t == 0)
    def _():
        def build(i, carry):
            rowsrc[pos_ref[i]] = i
            return carry
        lax.fori_loop(0, n_tok, build, 0, unroll=16)

        def body(r, carry):
            issue_gather(0, tile_rows(0), 0, r)
            issue_gather(1, tile_rows(1), 1, r)
            return carry
        lax.fori_loop(0, tm, body, 0, unroll=8)
        for sl in range(3):
            obufs[sl][...] = jnp.zeros(obufs[sl].shape, obufs[sl].dtype)
        for sl in range(2):
            pltpu.make_async_copy(obufs[sl], y_hbm.at[pl.ds(n_tok + sl * tm, tm)], ssem.at[sl]).start()

    @pl.when(chga_ref[t] == 1)
    def _():
        ga_scr[...] = wga_ref[0, 0].astype(MM)
        ua_scr[...] = wua_ref[0, 0].astype(MM)
        da_scr[...] = wda_ref[0, 0].astype(MM)

    @pl.when(chgb_ref[t] == 1)
    def _():
        gb_scr[...] = wgb_ref[0, 0].astype(MM)
        ub_scr[...] = wub_ref[0, 0].astype(MM)
        db_scr[...] = wdb_ref[0, 0].astype(MM)

    prev = jnp.maximum(t - 1, 0)
    n_prev = jnp.where(t > 0, nvalid_ref[prev], 0)
    n_cur = nvalid_ref[t]

    n_ahead = tile_rows(t + 2)

    def tile_block(sl):
        other = (sl + 2) % 3
        wait_gather(sl)
        wait_scatter(sl)
        x1 = xbufs[sl][:, 0:D_MODEL]
        x = x1.astype(MM)

        def expert(wg, wu, wd, wrow):
            gate = jnp.dot(x, wg[...], preferred_element_type=F32)
            up = jnp.dot(x, wu[...], preferred_element_type=F32)
            hid = (_silu(gate) * up).astype(MM)
            return jnp.dot(hid, wd[...], preferred_element_type=F32) * wrow

        f = (expert(ga_scr, ua_scr, da_scr, xbufs[sl][:, D_MODEL:D_MODEL + 1])
             + expert(gb_scr, ub_scr, db_scr, xbufs[sl][:, D_MODEL + 1:D_MODEL + 2]))
        obufs[sl][...] = _layer_norm(ALPHA * x1 + f, g_ref[...], b_ref[...])
        for r in range(tm):
            issue_gather(t + 2, n_ahead, other, r)
            issue_scatter(prev, n_prev, other, r)

    def drain(sl):
        other = (sl + 2) % 3
        wait_gather(sl)
        wait_gather((sl + 1) % 3)

        def body(r, carry):
            issue_scatter(prev, n_prev, other, r)
            return carry
        lax.fori_loop(0, tm, body, 0, unroll=8)
        for k in range(3):
            wait_scatter(k)

    for sl in range(3):
        pl.when(jnp.logical_and(n_cur > 0, t % 3 == sl))(functools.partial(tile_block, sl))
        pl.when(jnp.logical_and(jnp.logical_and(n_cur == 0, n_prev > 0), t % 3 == sl))(
            functools.partial(drain, sl))


def _moe(x1e, key, cnt, ln_g, ln_b, w_gate, w_up, w_down, layer, tm):
    n = x1e.shape[0]
    n_tiles = n // tm + N_CLASSES
    n_rows = n_tiles * tm
    counts = cnt[:N_CLASSES, 0]
    tiles_c = (counts + tm - 1) // tm
    tile_end_c = jnp.cumsum(tiles_c)
    tile_start_c = tile_end_c - tiles_c
    total = tile_end_c[-1]
    tid = jnp.arange(n_tiles, dtype=I32)
    t_eff = jnp.minimum(tid, total - 1)
    tile_cls = jnp.minimum(jnp.sum((tile_end_c[None, :] <= t_eff[:, None]).astype(I32), axis=1),
                           N_CLASSES - 1)
    sel = (tile_cls[:, None] == jnp.arange(N_CLASSES, dtype=I32)[None, :]).astype(I32)
    left = jnp.sum(sel * counts[None, :], axis=1) - (tid - jnp.sum(sel * tile_start_c[None, :], axis=1)) * tm
    nvalid = jnp.where(tid < total, jnp.clip(left, 0, tm), 0).astype(I32)
    pair_id = tile_cls % len(PAIRS)
    pair_lo = sum(jnp.where(pair_id == k, p[0], 0) for k, p in enumerate(PAIRS))
    pair_hi = sum(jnp.where(pair_id == k, p[1], 0) for k, p in enumerate(PAIRS))
    ea = (EPG * (tile_cls // len(PAIRS)) + pair_lo).astype(I32)
    eb = (EPG * (tile_cls // len(PAIRS)) + pair_hi).astype(I32)
    first = tid == 0
    chga = jnp.logical_or(first, ea != jnp.roll(ea, 1)).astype(I32)
    chgb = jnp.logical_or(first, eb != jnp.roll(eb, 1)).astype(I32)
    key = key.reshape(n)
    tok_cls = key // KEY_CLS
    tok_sel = (tok_cls[:, None] == jnp.arange(N_CLASSES, dtype=I32)[None, :]).astype(I32)
    pos = (jnp.sum(tok_sel * tile_start_c[None, :], axis=1) * tm + key % KEY_CLS).astype(I32)

    kern = functools.partial(_moe_kernel, tm=tm, n_tok=n)
    wspec_a = lambda shape: pl.BlockSpec(
        shape, lambda t, nv, ca, cb, ea_, eb_, ps: (layer, ea_[t], 0, 0))
    wspec_b = lambda shape: pl.BlockSpec(
        shape, lambda t, nv, ca, cb, ea_, eb_, ps: (layer, eb_[t], 0, 0))
    vec = pl.BlockSpec((1, D_MODEL), lambda t, nv, ca, cb, ea_, eb_, ps: (0, 0))
    gu = (1, 1, D_MODEL, D_EXPERT)
    dn = (1, 1, D_EXPERT, D_MODEL)
    grid_spec = pltpu.PrefetchScalarGridSpec(
        num_scalar_prefetch=6,
        grid=(n_tiles,),
        in_specs=[pl.BlockSpec(memory_space=pl.ANY), vec, vec,
                  wspec_a(gu), wspec_a(gu), wspec_a(dn),
                  wspec_b(gu), wspec_b(gu), wspec_b(dn)],
        out_specs=pl.BlockSpec(memory_space=pl.ANY),
        scratch_shapes=[
            pltpu.VMEM((tm, X1E_W), F32), pltpu.VMEM((tm, X1E_W), F32), pltpu.VMEM((tm, X1E_W), F32),
            pltpu.VMEM((tm, D_MODEL), F32), pltpu.VMEM((tm, D_MODEL), F32),
            pltpu.VMEM((tm, D_MODEL), F32),
            pltpu.VMEM((D_MODEL, D_EXPERT), MM), pltpu.VMEM((D_MODEL, D_EXPERT), MM),
            pltpu.VMEM((D_EXPERT, D_MODEL), MM),
            pltpu.VMEM((D_MODEL, D_EXPERT), MM), pltpu.VMEM((D_MODEL, D_EXPERT), MM),
            pltpu.VMEM((D_EXPERT, D_MODEL), MM),
            pltpu.SMEM((n_rows,), I32),
            pltpu.SemaphoreType.DMA((3,)),
            pltpu.SemaphoreType.DMA((3,)),
        ],
    )
    return pl.pallas_call(
        kern,
        grid_spec=grid_spec,
        out_shape=jax.ShapeDtypeStruct((n + 3 * tm, D_MODEL), F32),
        compiler_params=_cparams(("arbitrary",)),
        name="moe",
    )(nvalid, chga, chgb, ea, eb, pos,
      x1e, ln_g, ln_b, w_gate, w_up, w_down, w_gate, w_up, w_down)


def _pad_hist(hist, rows):
    b, r, c = hist.shape
    return jnp.concatenate([jnp.zeros((b, rows - r, c), hist.dtype), hist], axis=1)


def _new_hist(hist, cur, c0, c1, keep):
    t = cur.shape[1]
    if t >= keep:
        return cur[:, t - keep:, c0:c1]
    return jnp.concatenate([hist[:, t:], cur[:, :, c0:c1]], axis=1)


def kernel(x_prompt, x_sample, state_delta, state_delta_conv, state_pool, state_lru, state_lru_conv,
           ab_w_in, a_conv_w, a_log_decay, a_dt_bias, a_norm_w, b_w_group, b_scale, ab_w_out,
           c_w_in, c_conv_w, c_conv_b, c_w_r, c_b_r, c_w_i, c_b_i, c_lambda, c_w_out,
           ln_mix_g, ln_mix_b, ln_ffn_g, ln_ffn_b, router_w, router_bias,
           moe_w_gate, moe_w_up, moe_w_down):
    bp, tp, d = x_prompt.shape
    bs, ts, _ = x_sample.shape
    n_p, n_s = bp * tp, bs * ts
    n = n_p + n_s
    tm = TOKEN_TILE
    tmix = MIX_TILE
    assert n_p % tmix == 0 and n_s % tmix == 0 and tmix % tm == 0

    x = jnp.concatenate([x_prompt.reshape(n_p, d), x_sample.reshape(n_s, d)], axis=0)
    groups = (
        dict(row0=0, batch=bp, seq=tp, hist_valid=0, fresh=True,
             ev=dict(nb_blk=1, rows=min(tp, 256)), od=dict(nb_blk=min(bp, 8), rows=min(tp, 64))),
        dict(row0=n_p, batch=bs, seq=ts, hist_valid=B_HIST, fresh=False,
             ev=dict(nb_blk=min(bs, 8), rows=ts), od=dict(nb_blk=min(bs, 32), rows=ts)),
    )
    rwt = router_w.T
    rb = router_bias.reshape(N_EXPERTS, 1)
    row = lambda v: v.reshape(1, -1)

    new = {k: ([], []) for k in ("delta", "dconv", "pool", "lru", "lconv")}
    for layer in range(DEPTH):
        j = layer // 2
        mixed = []
        if layer % 2 == 0:
            w = ab_w_in[j]
            c1 = EV_Z0 + A_V
            w_perm = jnp.concatenate(
                [w[:, :c1], w[:, c1 + 2 * A_HEADS:], w[:, c1:c1 + 2 * A_HEADS],
                 jnp.zeros((d, LANES - 2 * A_HEADS), w.dtype)], axis=1).astype(MM)
            gvec = jnp.zeros((2, LANES), F32)
            gvec = gvec.at[0, A_HEADS:2 * A_HEADS].set(a_log_decay[j])
            gvec = gvec.at[1, A_HEADS:2 * A_HEADS].set(a_dt_bias[j])
            for gi, g in enumerate(groups):
                b_, t_ = g["batch"], g["seq"]
                proj = _proj(x, w_perm, g["row0"], b_ * t_, tm)
                p3 = proj.reshape(b_, t_, EV_W)
                if g["fresh"]:
                    dconv = jnp.zeros((b_, 3, A_CONV_CH), F32)
                    delta = jnp.zeros((b_, A_HEADS, A_DK, A_DV), F32)
                    pool = jnp.zeros((b_, B_HIST, B_WIDTH), F32)
                else:
                    dconv, delta, pool = state_delta_conv[j], state_delta[j], state_pool[j]
                o, s_new = _even_mixer(
                    proj, _pad_hist(dconv, SUBLANES), delta, _pad_hist(pool, 2 * SUBLANES),
                    a_conv_w[j], gvec, row(a_norm_w[j]), b_w_group[j].astype(MM), row(b_scale[j]),
                    batch=b_, seq=t_, hist_valid=g["hist_valid"], **g["ev"])
                mixed.append(o)
                new["delta"][gi].append(s_new)
                new["dconv"][gi].append(_new_hist(dconv, p3, 0, A_CONV_CH, 3))
                new["pool"][gi].append(_new_hist(pool, p3, EV_U0, EV_U0 + B_WIDTH, B_HIST))
            w_out = ab_w_out[j].astype(MM)
        else:
            w_mm = c_w_in[j].astype(MM)
            for gi, g in enumerate(groups):
                b_, t_ = g["batch"], g["seq"]
                p3 = _proj(x, w_mm, g["row0"], b_ * t_, tm).reshape(b_, t_, 2 * C_WIDTH)
                if g["fresh"]:
                    lconv = jnp.zeros((b_, 3, C_WIDTH), F32)
                    lru = jnp.zeros((b_, C_WIDTH), F32)
                else:
                    lconv, lru = state_lru_conv[j], state_lru[j]
                o3, h_last = _odd_mixer(
                    p3, _pad_hist(lconv, SUBLANES), lru, c_conv_w[j], row(c_conv_b[j]),
                    c_w_r[j].astype(MM), row(c_b_r[j]), c_w_i[j].astype(MM), row(c_b_i[j]),
                    row(c_lambda[j]), **g["od"])
                mixed.append(o3.reshape(b_ * t_, C_WIDTH))
                new["lru"][gi].append(h_last)
                new["lconv"][gi].append(_new_hist(lconv, p3, C_WIDTH, 2 * C_WIDTH, 3))
            w_out = c_w_out[j].astype(MM)

        x1e, key, cnt = _mix_out(mixed[0], mixed[1], w_out, x, row(ln_mix_g[layer]),
                                 row(ln_mix_b[layer]), rwt, rb, tmix)
        x = _moe(x1e, key, cnt, row(ln_ffn_g[layer]), row(ln_ffn_b[layer]),
                 moe_w_gate, moe_w_up, moe_w_down, layer, tm)

    stack = lambda key, gi: jnp.stack(new[key][gi])
    return (x[:n_p].reshape(bp, tp, d), x[n_p:n].reshape(bs, ts, d),
            stack("delta", 0), stack("dconv", 0), stack("pool", 0), stack("lru", 0), stack("lconv", 0),
            stack("delta", 1), stack("dconv", 1), stack("pool", 1), stack("lru", 1), stack("lconv", 1))
```

```python
import functools

import jax
import jax.numpy as jnp
from jax import lax
from jax.experimental import pallas as pl
from jax.experimental.pallas import tpu as pltpu

F32 = jnp.float32
I32 = jnp.int32
MM = jnp.bfloat16
HIGHEST = lax.Precision.HIGHEST

D_MODEL = 1024
DEPTH = 4
A_HEADS = 4
A_DK = 128
A_DV = 128
A_QK = A_HEADS * A_DK
A_V = A_HEADS * A_DV
A_CONV_CH = 2 * A_QK + A_V
A_CHUNK = 64
B_GROUPS = 4
B_GW = 128
B_WINDOWS = (2, 4, 8, 16)
B_HIST = 15
B_WIDTH = B_GROUPS * B_GW
C_WIDTH = D_MODEL
C_HEADS = 4
C_HW = C_WIDTH // C_HEADS
C_GATE = 8.0
N_EXPERTS = 16
N_GROUPS = 4
EPG = N_EXPERTS // N_GROUPS
D_EXPERT = 512
ALPHA = (2 * DEPTH) ** 0.25
LN_EPS = 1e-5
RMS_EPS = 1e-6

LANES = 128
SUBLANES = 8
VMEM_LIMIT = 56 * 1024 * 1024

EV_Z0 = A_CONV_CH
EV_U0 = EV_Z0 + A_V
EV_BA0 = EV_U0 + B_WIDTH
EV_W = EV_BA0 + LANES

PAIRS = ((0, 1), (0, 2), (0, 3), (1, 2), (1, 3), (2, 3))
N_CLASSES = N_GROUPS * len(PAIRS)

TOKEN_TILE = 256
MIX_TILE = 512
CLS_ROWS = 32
KEY_CLS = 1 << 16
X1E_W = D_MODEL + LANES


def _cparams(sem):
    return pltpu.CompilerParams(dimension_semantics=sem, vmem_limit_bytes=VMEM_LIMIT)


def _dot(a, b):
    return jnp.dot(a.astype(MM), b.astype(MM), preferred_element_type=F32)


def _dot_nt(a, b):
    return lax.dot_general(a.astype(MM), b.astype(MM), (((1,), (1,)), ((), ())),
                           preferred_element_type=F32)


def _dot_tn(a, b):
    return lax.dot_general(a.astype(MM), b.astype(MM), (((0,), (0,)), ((), ())),
                           preferred_element_type=F32)


def _sigmoid(x):
    return 1.0 / (1.0 + jnp.exp(-x))


def _sigmoid_t(x):
    return 0.5 * jnp.tanh(0.5 * x) + 0.5


def _silu(x):
    return x * _sigmoid_t(x)


def _softplus(x):
    return jnp.maximum(x, 0.0) + jnp.log1p(jnp.exp(-jnp.abs(x)))


def _gelu_tanh(x):
    return x * (0.5 * (1.0 + jnp.tanh(0.7978845608028654 * (x + 0.044715 * (x * x * x)))))


def _layer_norm(v, g, b):
    mu = jnp.mean(v, axis=-1, keepdims=True)
    d = v - mu
    var = jnp.mean(d * d, axis=-1, keepdims=True)
    return d * lax.rsqrt(var + LN_EPS) * g + b


def _proj_kernel(x_ref, w_ref, o_ref):
    o_ref[...] = jnp.dot(x_ref[...].astype(MM), w_ref[...], preferred_element_type=F32)


def _proj(x, w_mm, row0, nrows, tm):
    k, width = w_mm.shape
    off = row0 // tm
    return pl.pallas_call(
        _proj_kernel,
        grid=(nrows // tm,),
        in_specs=[pl.BlockSpec((tm, k), lambda i: (i + off, 0)),
                  pl.BlockSpec((k, width), lambda i: (0, 0))],
        out_specs=pl.BlockSpec((tm, width), lambda i: (i, 0)),
        out_shape=jax.ShapeDtypeStruct((nrows, width), F32),
        compiler_params=_cparams(("parallel",)),
        name="proj",
    )(x, w_mm)


def _even_kernel(p_ref, convh_ref, s0_ref, poolh_ref, convw_ref, gvec_ref, normw_ref,
                 wgrp_ref, scale_ref, o_ref, snew_ref, ext_scr, qkv_scr, pext_scr, s_scr,
                 *, nb_blk, rows, chunk, hist_valid, carry_hist):
    i = pl.program_id(1)
    n_chunks = rows // chunk
    n_neumann = max((chunk - 1).bit_length() - 1, 0)
    hb = 2 * SUBLANES

    @pl.when(i == 0)
    def _():
        s_scr[...] = s0_ref[...]
        ext_scr[:, 0:SUBLANES, :] = convh_ref[...]
        pext_scr[:, 0:hb, :] = poolh_ref[...]

    rid = lax.broadcasted_iota(I32, (chunk, chunk), 0)
    cid = lax.broadcasted_iota(I32, (chunk, chunk), 1)
    incl = rid >= cid
    strict = rid > cid
    eye = (rid == cid).astype(F32)
    ltri = incl.astype(F32)
    lane = lax.broadcasted_iota(I32, (chunk, LANES), 1)
    trow = lax.broadcasted_iota(I32, (rows, LANES), 0)
    pos = (i * rows + trow + (1 + hist_valid)).astype(F32)
    neg_decay_rate = -jnp.exp(gvec_ref[0:1, :])
    dt_bias = gvec_ref[1:2, :]

    for nb in range(nb_blk):
        r0 = nb * rows
        ext_scr[nb, SUBLANES:SUBLANES + rows, :] = p_ref[r0:r0 + rows, 0:A_CONV_CH]
        for ct in range(A_CONV_CH // LANES):
            cs = slice(ct * LANES, (ct + 1) * LANES)
            acc = None
            for j in range(4):
                lo = SUBLANES - 3 + j
                term = ext_scr[nb, lo:lo + rows, cs] * convw_ref[j:j + 1, cs]
                acc = term if acc is None else acc + term
            t = _silu(acc)
            if ct < 2 * A_HEADS:
                t = t * lax.rsqrt(jnp.sum(t * t, axis=-1, keepdims=True) + 1e-6)
                if ct < A_HEADS:
                    t = t * (A_DK ** -0.5)
            qkv_scr[r0:r0 + rows, cs] = t
        if carry_hist:
            ext_scr[nb, 0:SUBLANES, :] = ext_scr[nb, rows:rows + SUBLANES, :]

        pext_scr[nb, hb:hb + rows, :] = p_ref[r0:r0 + rows, EV_U0:EV_U0 + B_WIDTH]
        for gi, w in enumerate(B_WINDOWS):
            cs = slice(gi * B_GW, (gi + 1) * B_GW)
            cur = pext_scr[nb, hb:hb + rows, cs]
            tot = cur
            for j in range(1, w):
                tot = tot + pext_scr[nb, hb - j:hb - j + rows, cs]
            pooled = tot / jnp.minimum(pos, float(w)) - cur
            ob = _dot(pooled, wgrp_ref[gi]) * scale_ref[:, cs]
            o_ref[r0:r0 + rows, A_V + gi * B_GW:A_V + (gi + 1) * B_GW] = ob
        if carry_hist:
            pext_scr[nb, 0:hb, :] = pext_scr[nb, rows:rows + hb, :]

    chunks = [(nb, c) for nb in range(nb_blk) for c in range(n_chunks)]
    probs = [(nb, c, h) for nb, c in chunks for h in range(A_HEADS)]
    rs = {(nb, c): slice(nb * rows + c * chunk, nb * rows + (c + 1) * chunk) for nb, c in chunks}
    gcol = lambda arr, h: arr[:, A_HEADS + h:A_HEADS + h + 1]
    qf = lambda p: qkv_scr[rs[p[:2]], p[2] * LANES:(p[2] + 1) * LANES]
    kf = lambda p: qkv_scr[rs[p[:2]], A_QK + p[2] * LANES:A_QK + (p[2] + 1) * LANES]
    vf = lambda p: qkv_scr[rs[p[:2]], 2 * A_QK + p[2] * LANES:2 * A_QK + (p[2] + 1) * LANES]

    ba = {ck: p_ref[rs[ck], EV_BA0:EV_BA0 + LANES] for ck in chunks}
    beta_full = {ck: _sigmoid(ba[ck]) for ck in chunks}
    g_full = {ck: neg_decay_rate * _softplus(ba[ck] + dt_bias) for ck in chunks}
    gc_full = {ck: jnp.dot(ltri, g_full[ck], precision=HIGHEST, preferred_element_type=F32)
               for ck in chunks}
    eg_full = {ck: jnp.exp(gc_full[ck]) for ck in chunks}
    gl_full = {ck: gc_full[ck][chunk - 1:chunk, :] for ck in chunks}
    ekd_full = {ck: jnp.exp(gl_full[ck] - gc_full[ck]) for ck in chunks}
    egl_full = {ck: jnp.exp(gl_full[ck]) for ck in chunks}
    sel = [(lane == A_HEADS + h).astype(F32) for h in range(A_HEADS)]
    grow = {p: lax.dot_general(sel[p[2]], gc_full[p[:2]], (((1,), (1,)), ((), ())),
                               precision=HIGHEST, preferred_element_type=F32) for p in probs}
    decay = {p: jnp.where(incl, jnp.exp(jnp.where(incl, gcol(gc_full[p[:2]], p[2]) - grow[p], 0.0)), 0.0)
             for p in probs}
    beta = {p: beta_full[p[:2]][:, p[2]:p[2] + 1] for p in probs}
    kk = {p: _dot_nt(kf(p) * beta[p], kf(p)) for p in probs}
    qk = {p: _dot_nt(qf(p), kf(p)) for p in probs}
    a_intra = {p: jnp.where(incl, qk[p] * decay[p], 0.0) for p in probs}
    power = {p: -jnp.where(strict, kk[p] * decay[p], 0.0) for p in probs}
    tinv = {p: eye + power[p] for p in probs}
    for _ in range(n_neumann):
        power = {p: _dot(power[p], power[p]) for p in probs}
        tinv = {p: tinv[p] + _dot(tinv[p], power[p]) for p in probs}
    u_in = {p: _dot(tinv[p], vf(p) * beta[p]) for p in probs}
    w_in = {p: _dot(tinv[p], kf(p) * beta[p] * gcol(eg_full[p[:2]], p[2])) for p in probs}

    for c in range(n_chunks):
        cp = [(nb, c, h) for nb in range(nb_blk) for h in range(A_HEADS)]
        s_old = {p: s_scr[p[0], p[2]] for p in cp}
        wq = {p: _dot(jnp.concatenate([w_in[p], qf(p) * gcol(eg_full[p[:2]], p[2])], axis=0), s_old[p])
              for p in cp}
        u_new = {p: u_in[p] - wq[p][0:chunk] for p in cp}
        au = {p: _dot(a_intra[p], u_new[p]) for p in cp}
        ku = {p: _dot_tn(kf(p) * gcol(ekd_full[p[:2]], p[2]), u_new[p]) for p in cp}
        for p in cp:
            nb, _, h = p
            s_scr[nb, h] = s_old[p] * gcol(egl_full[p[:2]], h) + ku[p]
            o = wq[p][chunk:2 * chunk] + au[p]
            o = o * lax.rsqrt(jnp.mean(o * o, axis=-1, keepdims=True) + RMS_EPS) * normw_ref[...]
            z = p_ref[rs[p[:2]], EV_Z0 + h * LANES:EV_Z0 + (h + 1) * LANES]
            o_ref[rs[p[:2]], h * LANES:(h + 1) * LANES] = o * _silu(z)

    @pl.when(i == pl.num_programs(1) - 1)
    def _():
        snew_ref[...] = s_scr[...]


def _even_mixer(proj, convh8, s0, poolh16, convw, gvec, normw, wgrp_mm, scale,
                *, batch, seq, nb_blk, rows, hist_valid):
    chunk = min(A_CHUNK, seq)
    assert seq % rows == 0 and rows % chunk == 0 and batch % nb_blk == 0
    assert nb_blk == 1 or rows == seq
    t_blocks = seq // rows
    blk = nb_blk * rows
    hb = 2 * SUBLANES
    kern = functools.partial(_even_kernel, nb_blk=nb_blk, rows=rows, chunk=chunk,
                             hist_valid=hist_valid, carry_hist=t_blocks > 1)
    full = lambda shape: pl.BlockSpec(shape, lambda b, i: (0,) * len(shape))
    return pl.pallas_call(
        kern,
        grid=(batch // nb_blk, t_blocks),
        in_specs=[
            pl.BlockSpec((blk, EV_W), lambda b, i: (b * t_blocks + i, 0)),
            pl.BlockSpec((nb_blk, SUBLANES, A_CONV_CH), lambda b, i: (b, 0, 0)),
            pl.BlockSpec((nb_blk, A_HEADS, A_DK, A_DV), lambda b, i: (b, 0, 0, 0)),
            pl.BlockSpec((nb_blk, hb, B_WIDTH), lambda b, i: (b, 0, 0)),
            full((4, A_CONV_CH)), full((2, LANES)), full((1, A_DV)),
            full((B_GROUPS, B_GW, B_GW)), full((1, B_WIDTH)),
        ],
        out_specs=[
            pl.BlockSpec((blk, D_MODEL), lambda b, i: (b * t_blocks + i, 0)),
            pl.BlockSpec((nb_blk, A_HEADS, A_DK, A_DV), lambda b, i: (b, 0, 0, 0)),
        ],
        out_shape=[jax.ShapeDtypeStruct((batch * seq, D_MODEL), F32),
                   jax.ShapeDtypeStruct((batch, A_HEADS, A_DK, A_DV), F32)],
        scratch_shapes=[
            pltpu.VMEM((nb_blk, SUBLANES + rows, A_CONV_CH), F32),
            pltpu.VMEM((blk, A_CONV_CH), F32),
            pltpu.VMEM((nb_blk, hb + rows, B_WIDTH), F32),
            pltpu.VMEM((nb_blk, A_HEADS, A_DK, A_DV), F32),
        ],
        compiler_params=_cparams(("parallel", "arbitrary")),
        name="even_mixer",
    )(proj, convh8, s0, poolh16, convw, gvec, normw, wgrp_mm, scale)


def _odd_kernel(p_ref, convh_ref, h0_ref, convw_ref, convb_ref, wr_ref, br_ref, wi_ref, bi_ref,
                lam_ref, o_ref, hlast_ref, ext_scr, a_scr, b_scr, hs_scr, h_scr,
                *, nb_blk, rows, carry_hist):
    i = pl.program_id(1)
    blk = nb_blk * rows
    pitch = a_scr.shape[1] // nb_blk

    @pl.when(i == 0)
    def _():
        h_scr[...] = h0_ref[...]
        ext_scr[:, 0:SUBLANES, :] = convh_ref[...]

    ext_scr[:, SUBLANES:SUBLANES + rows, :] = p_ref[:, :, C_WIDTH:2 * C_WIDTH]
    log_base = -C_GATE * _softplus(-lam_ref[...])
    for h in range(C_HEADS):
        cs = slice(h * C_HW, (h + 1) * C_HW)
        acc = None
        for j in range(4):
            lo = SUBLANES - 3 + j
            term = ext_scr[:, lo:lo + rows, cs] * convw_ref[j:j + 1, cs]
            acc = term if acc is None else acc + term
        xc = (acc + convb_ref[:, cs]).reshape(blk, C_HW)
        r = _sigmoid_t(_dot(xc, wr_ref[h]) + br_ref[:, cs])
        gi = _sigmoid_t(_dot(xc, wi_ref[h]) + bi_ref[:, cs])
        log_a = r * log_base[:, cs]
        a = jnp.exp(log_a)
        th = jnp.tanh(log_a)
        bt = jnp.sqrt(-2.0 * th) * lax.rsqrt(1.0 - th) * gi * xc
        for lt in range(C_HW // LANES):
            for b in range(nb_blk):
                dst = pl.ds(b * pitch, rows)
                src = slice(b * rows, (b + 1) * rows)
                a_scr[h * (C_HW // LANES) + lt, dst, :] = a[src, lt * LANES:(lt + 1) * LANES]
                b_scr[h * (C_HW // LANES) + lt, dst, :] = bt[src, lt * LANES:(lt + 1) * LANES]
    if carry_hist:
        ext_scr[:, 0:SUBLANES, :] = ext_scr[:, rows:rows + SUBLANES, :]

    n_lt = C_WIDTH // LANES

    def step(t, hcur):
        hnew = []
        for lt in range(n_lt):
            a_t = a_scr[lt, pl.ds(t, nb_blk, stride=pitch), :]
            b_t = b_scr[lt, pl.ds(t, nb_blk, stride=pitch), :]
            hn = a_t * hcur[lt] + b_t
            hs_scr[lt, pl.ds(t, nb_blk, stride=pitch), :] = hn
            hnew.append(hn)
        return tuple(hnew)

    h_init = tuple(h_scr[:, lt * LANES:(lt + 1) * LANES] for lt in range(n_lt))
    h_fin = lax.fori_loop(0, rows, step, h_init, unroll=SUBLANES)
    for lt in range(n_lt):
        cs = slice(lt * LANES, (lt + 1) * LANES)
        h_scr[:, cs] = h_fin[lt]
        for b in range(nb_blk):
            o_ref[b, :, cs] = _gelu_tanh(p_ref[b, :, cs]) * hs_scr[lt, pl.ds(b * pitch, rows), :]

    @pl.when(i == pl.num_programs(1) - 1)
    def _():
        hlast_ref[...] = h_scr[...]


def _odd_mixer(proj3, convh8, h0, convw, convb, wr_mm, br, wi_mm, bi, lam, *, nb_blk, rows):
    batch, seq, _ = proj3.shape
    assert seq % rows == 0 and batch % nb_blk == 0 and rows % SUBLANES == 0
    t_blocks = seq // rows
    pitch = rows + SUBLANES
    kern = functools.partial(_odd_kernel, nb_blk=nb_blk, rows=rows, carry_hist=t_blocks > 1)
    full = lambda shape: pl.BlockSpec(shape, lambda b, i: (0,) * len(shape))
    return pl.pallas_call(
        kern,
        grid=(batch // nb_blk, t_blocks),
        in_specs=[
            pl.BlockSpec((nb_blk, rows, 2 * C_WIDTH), lambda b, i: (b, i, 0)),
            pl.BlockSpec((nb_blk, SUBLANES, C_WIDTH), lambda b, i: (b, 0, 0)),
            pl.BlockSpec((nb_blk, C_WIDTH), lambda b, i: (b, 0)),
            full((4, C_WIDTH)), full((1, C_WIDTH)),
            full((C_HEADS, C_HW, C_HW)), full((1, C_WIDTH)),
            full((C_HEADS, C_HW, C_HW)), full((1, C_WIDTH)),
            full((1, C_WIDTH)),
        ],
        out_specs=[
            pl.BlockSpec((nb_blk, rows, C_WIDTH), lambda b, i: (b, i, 0)),
            pl.BlockSpec((nb_blk, C_WIDTH), lambda b, i: (b, 0)),
        ],
        out_shape=[jax.ShapeDtypeStruct((batch, seq, C_WIDTH), F32),
                   jax.ShapeDtypeStruct((batch, C_WIDTH), F32)],
        scratch_shapes=[
            pltpu.VMEM((nb_blk, SUBLANES + rows, C_WIDTH), F32),
            pltpu.VMEM((C_WIDTH // LANES, nb_blk * pitch, LANES), F32),
            pltpu.VMEM((C_WIDTH // LANES, nb_blk * pitch, LANES), F32),
            pltpu.VMEM((C_WIDTH // LANES, nb_blk * pitch, LANES), F32),
            pltpu.VMEM((nb_blk, C_WIDTH), F32),
        ],
        compiler_params=_cparams(("parallel", "arbitrary")),
        name="odd_mixer",
    )(proj3, convh8, h0, convw, convb, wr_mm, br, wi_mm, bi, lam)


def _route(x1, rwt_ref, rb_ref):
    logits = lax.dot_general(rwt_ref[...], x1, (((1,), (1,)), ((), ())),
                             precision=HIGHEST, preferred_element_type=F32)
    sc = _sigmoid(logits)
    bz = sc + rb_ref[...]
    row = lambda arr, e: arr[e:e + 1, :]
    best = None
    gidx = None
    for g in range(N_GROUPS):
        r = [row(bz, EPG * g + k) for k in range(EPG)]
        top2 = None
        for a, b in PAIRS:
            s = r[a] + r[b]
            top2 = s if top2 is None else jnp.maximum(top2, s)
        if best is None:
            best, gidx = top2, jnp.zeros(top2.shape, I32)
        else:
            upd = top2 > best
            gidx = jnp.where(upd, g, gidx)
            best = jnp.where(upd, top2, best)
    sb, ss = [], []
    for k in range(EPG):
        vb, vs = row(bz, k), row(sc, k)
        for g in range(1, N_GROUPS):
            vb = jnp.where(gidx == g, row(bz, EPG * g + k), vb)
            vs = jnp.where(gidx == g, row(sc, EPG * g + k), vs)
        sb.append(vb)
        ss.append(vs)
    m1, i1 = sb[0], jnp.zeros(sb[0].shape, I32)
    for k in range(1, EPG):
        upd = sb[k] > m1
        i1 = jnp.where(upd, k, i1)
        m1 = jnp.where(upd, sb[k], m1)
    m2, i2 = None, None
    for k in range(EPG):
        cand = jnp.where(i1 == k, -jnp.inf, sb[k])
        if m2 is None:
            m2, i2 = cand, jnp.zeros(cand.shape, I32)
        else:
            upd = cand > m2
            i2 = jnp.where(upd, k, i2)
            m2 = jnp.where(upd, cand, m2)
    lo = jnp.minimum(i1, i2)
    hi = jnp.maximum(i1, i2)
    pair = jnp.where(lo == 0, 0, jnp.where(lo == 1, 3, 5)) + hi - lo - 1
    s_lo, s_hi = ss[0], ss[0]
    for k in range(1, EPG):
        s_lo = jnp.where(lo == k, ss[k], s_lo)
        s_hi = jnp.where(hi == k, ss[k], s_hi)
    den = s_lo + s_hi
    return gidx * len(PAIRS) + pair, s_lo / den, s_hi / den


def _mix_out_kernel(ap_ref, as_ref, w_ref, xp_ref, xs_ref, g_ref, b_ref, rwt_ref, rb_ref, upper_ref,
                    x1e_ref, key_ref, cnt_ref, run_scr, *, np_tiles):
    i = pl.program_id(0)
    tm = xp_ref.shape[0]

    @pl.when(i == 0)
    def _():
        run_scr[...] = jnp.zeros(run_scr.shape, F32)

    def finish(a_ref, x_ref):
        hmix = jnp.dot(a_ref[...].astype(MM), w_ref[...], preferred_element_type=F32)
        x1 = _layer_norm(ALPHA * x_ref[...] + hmix, g_ref[...], b_ref[...])
        cls, wa, wb = _route(x1, rwt_ref, rb_ref)
        crow = lax.broadcasted_iota(I32, (CLS_ROWS, tm), 0)
        onehot = (crow == cls).astype(F32)
        prefix = jnp.dot(onehot.astype(MM), upper_ref[...], preferred_element_type=F32)
        run = run_scr[:, 0:1]
        rank = jnp.sum(onehot * (prefix + (run - 1.0)), axis=0, keepdims=True)
        run_new = run + jnp.sum(onehot, axis=1, keepdims=True)
        run_scr[...] = jnp.broadcast_to(run_new, run_scr.shape)
        cnt_ref[...] = jnp.broadcast_to(run_new, cnt_ref.shape).astype(I32)
        key_ref[...] = cls * KEY_CLS + rank.astype(I32)
        wrow = lax.broadcasted_iota(I32, (LANES, tm), 0)
        wpad = jnp.where(wrow == 0, wa, jnp.where(wrow == 1, wb, 0.0))
        x1e_ref[:, 0:D_MODEL] = x1
        x1e_ref[:, D_MODEL:D_MODEL + LANES] = wpad.T

    @pl.when(i < np_tiles)
    def _():
        finish(ap_ref, xp_ref)

    @pl.when(i >= np_tiles)
    def _():
        finish(as_ref, xs_ref)


def _mix_out(a_p, a_s, w_mm, x_p, row0_p, x_s, row0_s, g, b, rwt, rb, tm):
    n = a_p.shape[0] + a_s.shape[0]
    np_tiles = a_p.shape[0] // tm
    ns_tiles = a_s.shape[0] // tm
    assert (np_tiles + ns_tiles) * tm == n and row0_p % tm == 0 and row0_s % tm == 0
    off_p, off_s = row0_p // tm, row0_s // tm
    kern = functools.partial(_mix_out_kernel, np_tiles=np_tiles)
    full = lambda shape: pl.BlockSpec(shape, lambda i: (0,) * len(shape))
    upper = (jnp.arange(tm)[:, None] <= jnp.arange(tm)[None, :]).astype(MM)
    return pl.pallas_call(
        kern,
        grid=(np_tiles + ns_tiles,),
        in_specs=[
            pl.BlockSpec((tm, D_MODEL), lambda i: (jnp.minimum(i, np_tiles - 1), 0)),
            pl.BlockSpec((tm, D_MODEL), lambda i: (jnp.maximum(i - np_tiles, 0), 0)),
            full((D_MODEL, D_MODEL)),
            pl.BlockSpec((tm, D_MODEL), lambda i: (jnp.minimum(i, np_tiles - 1) + off_p, 0)),
            pl.BlockSpec((tm, D_MODEL), lambda i: (jnp.maximum(i - np_tiles, 0) + off_s, 0)),
            full((1, D_MODEL)), full((1, D_MODEL)),
            full((N_EXPERTS, D_MODEL)), full((N_EXPERTS, 1)),
            full((tm, tm)),
        ],
        out_specs=[
            pl.BlockSpec((tm, X1E_W), lambda i: (i, 0)),
            pl.BlockSpec((1, tm), lambda i: (0, i)),
            full((CLS_ROWS, LANES)),
        ],
        out_shape=[jax.ShapeDtypeStruct((n, X1E_W), F32),
                   jax.ShapeDtypeStruct((1, n), I32),
                   jax.ShapeDtypeStruct((CLS_ROWS, LANES), I32)],
        scratch_shapes=[pltpu.VMEM((CLS_ROWS, LANES), F32)],
        compiler_params=_cparams(("arbitrary",)),
        name="mix_out",
    )(a_p, a_s, w_mm, x_p, x_s, g, b, rwt, rb, upper)


def _moe_kernel(nvalid_ref, chga_ref, chgb_ref, ea_ref, eb_ref, pos_ref,
                x_hbm, g_ref, b_ref, wga_ref, wua_ref, wda_ref, wgb_ref, wub_ref, wdb_ref,
                y_hbm, xbuf0, xbuf1, xbuf2, obuf0, obuf1, obuf2,
                ga_scr, ua_scr, da_scr, gb_scr, ub_scr, db_scr, rowsrc, gsem, ssem, *, tm, n_tok):
    t = pl.program_id(0)
    nt = pl.num_programs(0)
    xbufs, obufs = (xbuf0, xbuf1, xbuf2), (obuf0, obuf1, obuf2)

    def gather_copy(tok, r, sl):
        return pltpu.make_async_copy(x_hbm.at[pl.ds(tok, 1)], xbufs[sl].at[pl.ds(r, 1)], gsem.at[sl])

    def scatter_copy(tok, r, sl):
        return pltpu.make_async_copy(obufs[sl].at[pl.ds(r, 1)], y_hbm.at[pl.ds(tok, 1)], ssem.at[sl])

    def src_token(tile, n_tile, r):
        return rowsrc[jnp.where(r < n_tile, tile * tm + r, 0)]

    def tile_rows(tile):
        return jnp.where(tile < nt, nvalid_ref[jnp.minimum(tile, nt - 1)], 0)

    def issue_gather(tile, n_tile, sl, r, priority=0):
        gather_copy(src_token(tile, n_tile, r), r, sl).start(priority=priority)

    def issue_scatter(tile, n_tile, sl, r, priority=0):
        dst = jnp.where(r < n_tile, src_token(tile, n_tile, r), n_tok + sl * tm + r)
        scatter_copy(dst, r, sl).start(priority=priority)

    def wait_gather(sl):
        pltpu.make_async_copy(x_hbm.at[pl.ds(0, tm)], xbufs[sl], gsem.at[sl]).wait()

    def wait_scatter(sl):
        pltpu.make_async_copy(obufs[sl], y_hbm.at[pl.ds(0, tm)], ssem.at[sl]).wait()

    @pl.when(t == 0)
    def _():
        def build(i, carry):
            rowsrc[pos_ref[i]] = i
            return carry
        lax.fori_loop(0, n_tok, build, 0, unroll=16)

        def body(r, carry):
            issue_gather(0, tile_rows(0), 0, r)
            issue_gather(1, tile_rows(1), 1, r)
            return carry
        lax.fori_loop(0, tm, body, 0, unroll=8)
        for sl in range(3):
            obufs[sl][...] = jnp.zeros(obufs[sl].shape, obufs[sl].dtype)
        for sl in range(2):
            pltpu.make_async_copy(obufs[sl], y_hbm.at[pl.ds(n_tok + sl * tm, tm)], ssem.at[sl]).start()

    @pl.when(chga_ref[t] == 1)
    def _():
        ga_scr[...] = wga_ref[0, 0].astype(MM)
        ua_scr[...] = wua_ref[0, 0].astype(MM)
        da_scr[...] = wda_ref[0, 0].astype(MM)

    @pl.when(chgb_ref[t] == 1)
    def _():
        gb_scr[...] = wgb_ref[0, 0].astype(MM)
        ub_scr[...] = wub_ref[0, 0].astype(MM)
        db_scr[...] = wdb_ref[0, 0].astype(MM)

    prev = jnp.maximum(t - 1, 0)
    n_prev = jnp.where(t > 0, nvalid_ref[prev], 0)
    n_cur = nvalid_ref[t]

    n_ahead = tile_rows(t + 2)

    def tile_block(sl):
        other = (sl + 2) % 3
        wait_gather(sl)
        wait_scatter(sl)
        x1 = xbufs[sl][:, 0:D_MODEL]
        x = x1.astype(MM)

        def expert(wg, wu, wd, wrow):
            gate = jnp.dot(x, wg[...], preferred_element_type=F32)
            up = jnp.dot(x, wu[...], preferred_element_type=F32)
            hid = (_silu(gate) * up).astype(MM)
            return jnp.dot(hid, wd[...], preferred_element_type=F32) * wrow

        f = (expert(ga_scr, ua_scr, da_scr, xbufs[sl][:, D_MODEL:D_MODEL + 1])
             + expert(gb_scr, ub_scr, db_scr, xbufs[sl][:, D_MODEL + 1:D_MODEL + 2]))
        obufs[sl][...] = _layer_norm(ALPHA * x1 + f, g_ref[...], b_ref[...])
        for r in range(tm):
            issue_gather(t + 2, n_ahead, other, r, priority=r % 2)
            issue_scatter(prev, n_prev, other, r, priority=r % 2)

    def drain(sl):
        other = (sl + 2) % 3
        wait_gather(sl)
        wait_gather((sl + 1) % 3)

        def body(r, carry):
            issue_scatter(prev, n_prev, other, r)
            return carry
        lax.fori_loop(0, tm, body, 0, unroll=8)
        for k in range(3):
            wait_scatter(k)

    for sl in range(3):
        pl.when(jnp.logical_and(n_cur > 0, t % 3 == sl))(functools.partial(tile_block, sl))
        pl.when(jnp.logical_and(jnp.logical_and(n_cur == 0, n_prev > 0), t % 3 == sl))(
            functools.partial(drain, sl))


def _moe(x1e, key, cnt, ln_g, ln_b, w_gate, w_up, w_down, layer, tm):
    n = x1e.shape[0]
    n_tiles = n // tm + N_CLASSES
    n_rows = n_tiles * tm
    counts = cnt[:N_CLASSES, 0]
    tiles_c = (counts + tm - 1) // tm
    tile_end_c = jnp.cumsum(tiles_c)
    tile_start_c = tile_end_c - tiles_c
    total = tile_end_c[-1]
    tid = jnp.arange(n_tiles, dtype=I32)
    t_eff = jnp.minimum(tid, total - 1)
    tile_cls = jnp.minimum(jnp.sum((tile_end_c[None, :] <= t_eff[:, None]).astype(I32), axis=1),
                           N_CLASSES - 1)
    sel = (tile_cls[:, None] == jnp.arange(N_CLASSES, dtype=I32)[None, :]).astype(I32)
    left = jnp.sum(sel * counts[None, :], axis=1) - (tid - jnp.sum(sel * tile_start_c[None, :], axis=1)) * tm
    nvalid = jnp.where(tid < total, jnp.clip(left, 0, tm), 0).astype(I32)
    pair_id = tile_cls % len(PAIRS)
    pair_lo = sum(jnp.where(pair_id == k, p[0], 0) for k, p in enumerate(PAIRS))
    pair_hi = sum(jnp.where(pair_id == k, p[1], 0) for k, p in enumerate(PAIRS))
    ea = (EPG * (tile_cls // len(PAIRS)) + pair_lo).astype(I32)
    eb = (EPG * (tile_cls // len(PAIRS)) + pair_hi).astype(I32)
    first = tid == 0
    chga = jnp.logical_or(first, ea != jnp.roll(ea, 1)).astype(I32)
    chgb = jnp.logical_or(first, eb != jnp.roll(eb, 1)).astype(I32)
    key = key.reshape(n)
    tok_cls = key // KEY_CLS
    tok_sel = (tok_cls[:, None] == jnp.arange(N_CLASSES, dtype=I32)[None, :]).astype(I32)
    pos = (jnp.sum(tok_sel * tile_start_c[None, :], axis=1) * tm + key % KEY_CLS).astype(I32)

    kern = functools.partial(_moe_kernel, tm=tm, n_tok=n)
    wspec_a = lambda shape: pl.BlockSpec(
        shape, lambda t, nv, ca, cb, ea_, eb_, ps: (layer, ea_[t], 0, 0))
    wspec_b = lambda shape: pl.BlockSpec(
        shape, lambda t, nv, ca, cb, ea_, eb_, ps: (layer, eb_[t], 0, 0))
    vec = pl.BlockSpec((1, D_MODEL), lambda t, nv, ca, cb, ea_, eb_, ps: (0, 0))
    gu = (1, 1, D_MODEL, D_EXPERT)
    dn = (1, 1, D_EXPERT, D_MODEL)
    grid_spec = pltpu.PrefetchScalarGridSpec(
        num_scalar_prefetch=6,
        grid=(n_tiles,),
        in_specs=[pl.BlockSpec(memory_space=pl.ANY), vec, vec,
                  wspec_a(gu), wspec_a(gu), wspec_a(dn),
                  wspec_b(gu), wspec_b(gu), wspec_b(dn)],
        out_specs=pl.BlockSpec(memory_space=pl.ANY),
        scratch_shapes=[
            pltpu.VMEM((tm, X1E_W), F32), pltpu.VMEM((tm, X1E_W), F32), pltpu.VMEM((tm, X1E_W), F32),
            pltpu.VMEM((tm, D_MODEL), F32), pltpu.VMEM((tm, D_MODEL), F32),
            pltpu.VMEM((tm, D_MODEL), F32),
            pltpu.VMEM((D_MODEL, D_EXPERT), MM), pltpu.VMEM((D_MODEL, D_EXPERT), MM),
            pltpu.VMEM((D_EXPERT, D_MODEL), MM),
            pltpu.VMEM((D_MODEL, D_EXPERT), MM), pltpu.VMEM((D_MODEL, D_EXPERT), MM),
            pltpu.VMEM((D_EXPERT, D_MODEL), MM),
            pltpu.SMEM((n_rows,), I32),
            pltpu.SemaphoreType.DMA((3,)),
            pltpu.SemaphoreType.DMA((3,)),
        ],
    )
    return pl.pallas_call(
        kern,
        grid_spec=grid_spec,
        out_shape=jax.ShapeDtypeStruct((n + 3 * tm, D_MODEL), F32),
        compiler_params=_cparams(("arbitrary",)),
        name="moe",
    )(nvalid, chga, chgb, ea, eb, pos,
      x1e, ln_g, ln_b, w_gate, w_up, w_down, w_gate, w_up, w_down)


def _pad_hist(hist, rows):
    b, r, c = hist.shape
    return jnp.concatenate([jnp.zeros((b, rows - r, c), hist.dtype), hist], axis=1)


def _new_hist(hist, cur, c0, c1, keep):
    t = cur.shape[1]
    if t >= keep:
        return cur[:, t - keep:, c0:c1]
    return jnp.concatenate([hist[:, t:], cur[:, :, c0:c1]], axis=1)


def kernel(x_prompt, x_sample, state_delta, state_delta_conv, state_pool, state_lru, state_lru_conv,
           ab_w_in, a_conv_w, a_log_decay, a_dt_bias, a_norm_w, b_w_group, b_scale, ab_w_out,
           c_w_in, c_conv_w, c_conv_b, c_w_r, c_b_r, c_w_i, c_b_i, c_lambda, c_w_out,
           ln_mix_g, ln_mix_b, ln_ffn_g, ln_ffn_b, router_w, router_bias,
           moe_w_gate, moe_w_up, moe_w_down):
    bp, tp, d = x_prompt.shape
    bs, ts, _ = x_sample.shape
    n_p, n_s = bp * tp, bs * ts
    n = n_p + n_s
    tm = TOKEN_TILE
    tmix = MIX_TILE
    assert n_p % tmix == 0 and n_s % tmix == 0 and tmix % tm == 0

    xin = [(x_prompt.reshape(n_p, d), 0), (x_sample.reshape(n_s, d), 0)]
    groups = (
        dict(batch=bp, seq=tp, hist_valid=0, fresh=True,
             ev=dict(nb_blk=1, rows=min(tp, 256)), od=dict(nb_blk=min(bp, 8), rows=min(tp, 64))),
        dict(batch=bs, seq=ts, hist_valid=B_HIST, fresh=False,
             ev=dict(nb_blk=min(bs, 8), rows=ts), od=dict(nb_blk=min(bs, 32), rows=ts)),
    )
    rwt = router_w.T
    rb = router_bias.reshape(N_EXPERTS, 1)
    row = lambda v: v.reshape(1, -1)

    new = {k: ([], []) for k in ("delta", "dconv", "pool", "lru", "lconv")}
    for layer in range(DEPTH):
        j = layer // 2
        mixed = []
        if layer % 2 == 0:
            w = ab_w_in[j]
            c1 = EV_Z0 + A_V
            w_perm = jnp.concatenate(
                [w[:, :c1], w[:, c1 + 2 * A_HEADS:], w[:, c1:c1 + 2 * A_HEADS],
                 jnp.zeros((d, LANES - 2 * A_HEADS), w.dtype)], axis=1).astype(MM)
            gvec = jnp.zeros((2, LANES), F32)
            gvec = gvec.at[0, A_HEADS:2 * A_HEADS].set(a_log_decay[j])
            gvec = gvec.at[1, A_HEADS:2 * A_HEADS].set(a_dt_bias[j])
            for gi, g in enumerate(groups):
                b_, t_ = g["batch"], g["seq"]
                proj = _proj(xin[gi][0], w_perm, xin[gi][1], b_ * t_, tm)
                p3 = proj.reshape(b_, t_, EV_W)
                if g["fresh"]:
                    dconv = jnp.zeros((b_, 3, A_CONV_CH), F32)
                    delta = jnp.zeros((b_, A_HEADS, A_DK, A_DV), F32)
                    pool = jnp.zeros((b_, B_HIST, B_WIDTH), F32)
                else:
                    dconv, delta, pool = state_delta_conv[j], state_delta[j], state_pool[j]
                o, s_new = _even_mixer(
                    proj, _pad_hist(dconv, SUBLANES), delta, _pad_hist(pool, 2 * SUBLANES),
                    a_conv_w[j], gvec, row(a_norm_w[j]), b_w_group[j].astype(MM), row(b_scale[j]),
                    batch=b_, seq=t_, hist_valid=g["hist_valid"], **g["ev"])
                mixed.append(o)
                new["delta"][gi].append(s_new)
                new["dconv"][gi].append(_new_hist(dconv, p3, 0, A_CONV_CH, 3))
                new["pool"][gi].append(_new_hist(pool, p3, EV_U0, EV_U0 + B_WIDTH, B_HIST))
            w_out = ab_w_out[j].astype(MM)
        else:
            w_mm = c_w_in[j].astype(MM)
            for gi, g in enumerate(groups):
                b_, t_ = g["batch"], g["seq"]
                p3 = _proj(xin[gi][0], w_mm, xin[gi][1], b_ * t_, tm).reshape(b_, t_, 2 * C_WIDTH)
                if g["fresh"]:
                    lconv = jnp.zeros((b_, 3, C_WIDTH), F32)
                    lru = jnp.zeros((b_, C_WIDTH), F32)
                else:
                    lconv, lru = state_lru_conv[j], state_lru[j]
                o3, h_last = _odd_mixer(
                    p3, _pad_hist(lconv, SUBLANES), lru, c_conv_w[j], row(c_conv_b[j]),
                    c_w_r[j].astype(MM), row(c_b_r[j]), c_w_i[j].astype(MM), row(c_b_i[j]),
                    row(c_lambda[j]), **g["od"])
                mixed.append(o3.reshape(b_ * t_, C_WIDTH))
                new["lru"][gi].append(h_last)
                new["lconv"][gi].append(_new_hist(lconv, p3, C_WIDTH, 2 * C_WIDTH, 3))
            w_out = c_w_out[j].astype(MM)

        x1e, key, cnt = _mix_out(mixed[0], mixed[1], w_out, xin[0][0], xin[0][1], xin[1][0], xin[1][1],
                                 row(ln_mix_g[layer]), row(ln_mix_b[layer]), rwt, rb, tmix)
        x = _moe(x1e, key, cnt, row(ln_ffn_g[layer]), row(ln_ffn_b[layer]),
                 moe_w_gate, moe_w_up, moe_w_down, layer, tm)
        xin = [(x, 0), (x, n_p)]

    stack = lambda key, gi: jnp.stack(new[key][gi])
    return (x[:n_p].reshape(bp, tp, d), x[n_p:n].reshape(bs, ts, d),
            stack("delta", 0), stack("dconv", 0), stack("pool", 0), stack("lru", 0), stack("lconv", 0),
            stack("delta", 1), stack("dconv", 1), stack("pool", 1), stack("lru", 1), stack("lconv", 1))
```

```python
import functools

import jax
import jax.numpy as jnp
from jax import lax
from jax.experimental import pallas as pl
from jax.experimental.pallas import tpu as pltpu

F32 = jnp.float32
I32 = jnp.int32
MM = jnp.bfloat16
HIGHEST = lax.Precision.HIGHEST

D_MODEL = 1024
DEPTH = 4
N_A_LAYERS = (DEPTH + 1) // 2
N_C_LAYERS = DEPTH // 2
A_HEADS = 4
A_DK = 128
A_DV = 128
A_QK = A_HEADS * A_DK
A_V = A_HEADS * A_DV
A_CONV_CH = 2 * A_QK + A_V
A_CHUNK = 64
B_GROUPS = 4
B_GW = 128
B_WINDOWS = (2, 4, 8, 16)
B_HIST = 15
B_WIDTH = B_GROUPS * B_GW
C_WIDTH = D_MODEL
C_HEADS = 4
C_HW = C_WIDTH // C_HEADS
C_GATE = 8.0
N_EXPERTS = 16
N_GROUPS = 4
EPG = N_EXPERTS // N_GROUPS
D_EXPERT = 512
ALPHA = (2 * DEPTH) ** 0.25
LN_EPS = 1e-5
RMS_EPS = 1e-6

LANES = 128
SUBLANES = 8
VMEM_LIMIT = 56 * 1024 * 1024

EV_Z0 = A_CONV_CH
EV_U0 = EV_Z0 + A_V
EV_BA0 = EV_U0 + B_WIDTH
EV_W = EV_BA0 + LANES

PAIRS = ((0, 1), (0, 2), (0, 3), (1, 2), (1, 3), (2, 3))
N_CLASSES = N_GROUPS * len(PAIRS)

TOKEN_TILE = 256
MIX_TILE = 512
CLS_ROWS = 32
KEY_CLS = 1 << 16
X1E_W = D_MODEL + LANES


def _cparams(sem):
    return pltpu.CompilerParams(dimension_semantics=sem, vmem_limit_bytes=VMEM_LIMIT)


def _dot(a, b):
    return jnp.dot(a.astype(MM), b.astype(MM), preferred_element_type=F32)


def _dot_nt(a, b):
    return lax.dot_general(a.astype(MM), b.astype(MM), (((1,), (1,)), ((), ())),
                           preferred_element_type=F32)


def _dot_tn(a, b):
    return lax.dot_general(a.astype(MM), b.astype(MM), (((0,), (0,)), ((), ())),
                           preferred_element_type=F32)


def _sigmoid(x):
    return 1.0 / (1.0 + jnp.exp(-x))


def _sigmoid_t(x):
    return 0.5 * jnp.tanh(0.5 * x) + 0.5


def _silu(x):
    return x * _sigmoid_t(x)


def _softplus(x):
    return jnp.maximum(x, 0.0) + jnp.log1p(jnp.exp(-jnp.abs(x)))


def _gelu_tanh(x):
    return x * (0.5 * (1.0 + jnp.tanh(0.7978845608028654 * (x + 0.044715 * (x * x * x)))))


def _layer_norm(v, g, b):
    mu = jnp.mean(v, axis=-1, keepdims=True)
    d = v - mu
    var = jnp.mean(d * d, axis=-1, keepdims=True)
    return d * lax.rsqrt(var + LN_EPS) * g + b


def _even_kernel(*refs, nb_blk, rows, chunk, hist_valid, carry_hist, has_prev):
    (x_ref, w_ref, convh_ref, s0_ref, poolh_ref, convw_ref, gvec_ref, normw_ref,
     wgrp_ref, scale_ref) = refs[:10]
    refs = refs[10 + (1 if has_prev else 0):]
    o_ref, snew_ref, convnew_ref, poolnew_ref, p_ref, ext_scr, qkv_scr, pext_scr, s_scr = refs
    p_ref[...] = jnp.dot(x_ref[...].astype(MM), w_ref[...], preferred_element_type=F32)
    i = pl.program_id(1)
    n_chunks = rows // chunk
    n_neumann = max((chunk - 1).bit_length() - 1, 0)
    hb = 2 * SUBLANES

    @pl.when(i == 0)
    def _():
        s_scr[...] = s0_ref[0]
        ext_scr[:, 0:SUBLANES, :] = convh_ref[...]
        pext_scr[:, 0:hb, :] = poolh_ref[...]

    rid = lax.broadcasted_iota(I32, (chunk, chunk), 0)
    cid = lax.broadcasted_iota(I32, (chunk, chunk), 1)
    incl = rid >= cid
    strict = rid > cid
    eye = (rid == cid).astype(F32)
    ltri = incl.astype(F32)
    lane = lax.broadcasted_iota(I32, (chunk, LANES), 1)
    trow = lax.broadcasted_iota(I32, (rows, LANES), 0)
    pos = (i * rows + trow + (1 + hist_valid)).astype(F32)
    neg_decay_rate = -jnp.exp(gvec_ref[0:1, :])
    dt_bias = gvec_ref[1:2, :]

    for nb in range(nb_blk):
        r0 = nb * rows
        ext_scr[nb, SUBLANES:SUBLANES + rows, :] = p_ref[r0:r0 + rows, 0:A_CONV_CH]
        for ct in range(A_CONV_CH // LANES):
            cs = slice(ct * LANES, (ct + 1) * LANES)
            acc = None
            for j in range(4):
                lo = SUBLANES - 3 + j
                term = ext_scr[nb, lo:lo + rows, cs] * convw_ref[j:j + 1, cs]
                acc = term if acc is None else acc + term
            t = _silu(acc)
            if ct < 2 * A_HEADS:
                t = t * lax.rsqrt(jnp.sum(t * t, axis=-1, keepdims=True) + 1e-6)
                if ct < A_HEADS:
                    t = t * (A_DK ** -0.5)
            qkv_scr[r0:r0 + rows, cs] = t
        if carry_hist:
            ext_scr[nb, 0:SUBLANES, :] = ext_scr[nb, rows:rows + SUBLANES, :]

        pext_scr[nb, hb:hb + rows, :] = p_ref[r0:r0 + rows, EV_U0:EV_U0 + B_WIDTH]
        for gi, w in enumerate(B_WINDOWS):
            cs = slice(gi * B_GW, (gi + 1) * B_GW)
            cur = pext_scr[nb, hb:hb + rows, cs]
            tot = cur
            for j in range(1, w):
                tot = tot + pext_scr[nb, hb - j:hb - j + rows, cs]
            pooled = tot / jnp.minimum(pos, float(w)) - cur
            ob = _dot(pooled, wgrp_ref[gi]) * scale_ref[:, cs]
            o_ref[r0:r0 + rows, A_V + gi * B_GW:A_V + (gi + 1) * B_GW] = ob
        if carry_hist:
            pext_scr[nb, 0:hb, :] = pext_scr[nb, rows:rows + hb, :]

    chunks = [(nb, c) for nb in range(nb_blk) for c in range(n_chunks)]
    probs = [(nb, c, h) for nb, c in chunks for h in range(A_HEADS)]
    rs = {(nb, c): slice(nb * rows + c * chunk, nb * rows + (c + 1) * chunk) for nb, c in chunks}
    gcol = lambda arr, h: arr[:, A_HEADS + h:A_HEADS + h + 1]
    qf = lambda p: qkv_scr[rs[p[:2]], p[2] * LANES:(p[2] + 1) * LANES]
    kf = lambda p: qkv_scr[rs[p[:2]], A_QK + p[2] * LANES:A_QK + (p[2] + 1) * LANES]
    vf = lambda p: qkv_scr[rs[p[:2]], 2 * A_QK + p[2] * LANES:2 * A_QK + (p[2] + 1) * LANES]

    ba = {ck: p_ref[rs[ck], EV_BA0:EV_BA0 + LANES] for ck in chunks}
    beta_full = {ck: _sigmoid(ba[ck]) for ck in chunks}
    g_full = {ck: neg_decay_rate * _softplus(ba[ck] + dt_bias) for ck in chunks}
    gc_full = {ck: jnp.dot(ltri, g_full[ck], precision=HIGHEST, preferred_element_type=F32)
               for ck in chunks}
    eg_full = {ck: jnp.exp(gc_full[ck]) for ck in chunks}
    gl_full = {ck: gc_full[ck][chunk - 1:chunk, :] for ck in chunks}
    ekd_full = {ck: jnp.exp(gl_full[ck] - gc_full[ck]) for ck in chunks}
    egl_full = {ck: jnp.exp(gl_full[ck]) for ck in chunks}
    sel = [(lane == A_HEADS + h).astype(F32) for h in range(A_HEADS)]
    grow = {p: lax.dot_general(sel[p[2]], gc_full[p[:2]], (((1,), (1,)), ((), ())),
                               precision=HIGHEST, preferred_element_type=F32) for p in probs}
    decay = {p: jnp.where(incl, jnp.exp(jnp.where(incl, gcol(gc_full[p[:2]], p[2]) - grow[p], 0.0)), 0.0)
             for p in probs}
    beta = {p: beta_full[p[:2]][:, p[2]:p[2] + 1] for p in probs}
    kk = {p: _dot_nt(kf(p) * beta[p], kf(p)) for p in probs}
    qk = {p: _dot_nt(qf(p), kf(p)) for p in probs}
    a_intra = {p: jnp.where(incl, qk[p] * decay[p], 0.0) for p in probs}
    power = {p: -jnp.where(strict, kk[p] * decay[p], 0.0) for p in probs}
    tinv = {p: eye + power[p] for p in probs}
    for _ in range(n_neumann):
        power = {p: _dot(power[p], power[p]) for p in probs}
        tinv = {p: tinv[p] + _dot(tinv[p], power[p]) for p in probs}
    u_in = {p: _dot(tinv[p], vf(p) * beta[p]) for p in probs}
    w_in = {p: _dot(tinv[p], kf(p) * beta[p] * gcol(eg_full[p[:2]], p[2])) for p in probs}

    for c in range(n_chunks):
        cp = [(nb, c, h) for nb in range(nb_blk) for h in range(A_HEADS)]
        s_old = {p: s_scr[p[0], p[2]] for p in cp}
        wq = {p: _dot(jnp.concatenate([w_in[p], qf(p) * gcol(eg_full[p[:2]], p[2])], axis=0), s_old[p])
              for p in cp}
        u_new = {p: u_in[p] - wq[p][0:chunk] for p in cp}
        au = {p: _dot(a_intra[p], u_new[p]) for p in cp}
        ku = {p: _dot_tn(kf(p) * gcol(ekd_full[p[:2]], p[2]), u_new[p]) for p in cp}
        for p in cp:
            nb, _, h = p
            s_scr[nb, h] = s_old[p] * gcol(egl_full[p[:2]], h) + ku[p]
            o = wq[p][chunk:2 * chunk] + au[p]
            o = o * lax.rsqrt(jnp.mean(o * o, axis=-1, keepdims=True) + RMS_EPS) * normw_ref[...]
            z = p_ref[rs[p[:2]], EV_Z0 + h * LANES:EV_Z0 + (h + 1) * LANES]
            o_ref[rs[p[:2]], h * LANES:(h + 1) * LANES] = o * _silu(z)

    @pl.when(i == pl.num_programs(1) - 1)
    def _():
        snew_ref[0] = s_scr[...]
        if not has_prev:
            for other in range(1, snew_ref.shape[0]):
                snew_ref[other] = jnp.zeros(s_scr.shape, F32)
        convnew_ref[...] = ext_scr[:, rows:rows + SUBLANES, :]
        poolnew_ref[...] = pext_scr[:, rows:rows + hb, :]


def _even_mixer(x, row0, w_mm, convh8, s0_all, layer_j, poolh16, convw, gvec, normw, wgrp_mm, scale,
                snew_prev, *, batch, seq, nb_blk, rows, hist_valid):
    chunk = min(A_CHUNK, seq)
    assert seq % rows == 0 and rows % chunk == 0 and batch % nb_blk == 0
    assert nb_blk == 1 or rows == seq
    t_blocks = seq // rows
    blk = nb_blk * rows
    assert row0 % blk == 0
    off = row0 // blk
    hb = 2 * SUBLANES
    has_prev = snew_prev is not None
    slot = layer_j if s0_all.shape[0] > 1 else 0
    kern = functools.partial(_even_kernel, nb_blk=nb_blk, rows=rows, chunk=chunk, hist_valid=hist_valid,
                             carry_hist=t_blocks > 1, has_prev=has_prev)
    full = lambda shape: pl.BlockSpec(shape, lambda b, i: (0,) * len(shape))
    state_blk = (1, nb_blk, A_HEADS, A_DK, A_DV)
    in_specs = [
        pl.BlockSpec((blk, D_MODEL), lambda b, i: (b * t_blocks + i + off, 0)),
        full((D_MODEL, EV_W)),
        pl.BlockSpec((nb_blk, SUBLANES, A_CONV_CH), lambda b, i: (b, 0, 0)),
        pl.BlockSpec(state_blk, lambda b, i: (slot, b, 0, 0, 0)),
        pl.BlockSpec((nb_blk, hb, B_WIDTH), lambda b, i: (b, 0, 0)),
        full((4, A_CONV_CH)), full((2, LANES)), full((1, A_DV)),
        full((B_GROUPS, B_GW, B_GW)), full((1, B_WIDTH)),
    ]
    args = [x, w_mm, convh8, s0_all, poolh16, convw, gvec, normw, wgrp_mm, scale]
    if has_prev:
        in_specs.append(pl.BlockSpec(memory_space=pl.ANY))
        args.append(snew_prev)
        snew_spec = pl.BlockSpec(state_blk, lambda b, i: (layer_j, b, 0, 0, 0))
    else:
        snew_spec = pl.BlockSpec((N_A_LAYERS,) + state_blk[1:], lambda b, i: (0, b, 0, 0, 0))
    return pl.pallas_call(
        kern,
        grid=(batch // nb_blk, t_blocks),
        in_specs=in_specs,
        out_specs=[
            pl.BlockSpec((blk, D_MODEL), lambda b, i: (b * t_blocks + i, 0)),
            snew_spec,
            pl.BlockSpec((nb_blk, SUBLANES, A_CONV_CH), lambda b, i: (b, 0, 0)),
            pl.BlockSpec((nb_blk, hb, B_WIDTH), lambda b, i: (b, 0, 0)),
        ],
        out_shape=[jax.ShapeDtypeStruct((batch * seq, D_MODEL), F32),
                   jax.ShapeDtypeStruct((N_A_LAYERS, batch, A_HEADS, A_DK, A_DV), F32),
                   jax.ShapeDtypeStruct((batch, SUBLANES, A_CONV_CH), F32),
                   jax.ShapeDtypeStruct((batch, hb, B_WIDTH), F32)],
        scratch_shapes=[
            pltpu.VMEM((blk, EV_W), F32),
            pltpu.VMEM((nb_blk, SUBLANES + rows, A_CONV_CH), F32),
            pltpu.VMEM((blk, A_CONV_CH), F32),
            pltpu.VMEM((nb_blk, hb + rows, B_WIDTH), F32),
            pltpu.VMEM((nb_blk, A_HEADS, A_DK, A_DV), F32),
        ],
        input_output_aliases={len(args) - 1: 1} if has_prev else {},
        compiler_params=_cparams(("parallel", "arbitrary")),
        name="even_mixer",
    )(*args)


def _odd_kernel(*refs, n_x, nb_blk, rows, carry_hist):
    x_refs = refs[:n_x]
    (w_ref, convh_ref, h0_ref, convw_ref, convb_ref, wr_ref, br_ref, wi_ref, bi_ref, lam_ref,
     o_ref, hlast_ref, convnew_ref, p_ref, ext_scr, a_scr, b_scr, hs_scr, h_scr) = refs[n_x:]
    i = pl.program_id(1)
    blk = nb_blk * rows
    pitch = a_scr.shape[1] // nb_blk

    @pl.when(i == 0)
    def _():
        h_scr[...] = h0_ref[...]
        ext_scr[:, 0:SUBLANES, :] = convh_ref[...]

    per = nb_blk // n_x
    for k in range(n_x):
        pk = jnp.dot(x_refs[k][...].astype(MM), w_ref[...], preferred_element_type=F32)
        p_ref[k * per:(k + 1) * per] = pk.reshape(per, rows, 2 * C_WIDTH)

    ext_scr[:, SUBLANES:SUBLANES + rows, :] = p_ref[:, :, C_WIDTH:2 * C_WIDTH]
    log_base = -C_GATE * _softplus(-lam_ref[...])
    for h in range(C_HEADS):
        cs = slice(h * C_HW, (h + 1) * C_HW)
        acc = None
        for j in range(4):
            lo = SUBLANES - 3 + j
            term = ext_scr[:, lo:lo + rows, cs] * convw_ref[j:j + 1, cs]
            acc = term if acc is None else acc + term
        xc = (acc + convb_ref[:, cs]).reshape(blk, C_HW)
        r = _sigmoid_t(_dot(xc, wr_ref[h]) + br_ref[:, cs])
        gi = _sigmoid_t(_dot(xc, wi_ref[h]) + bi_ref[:, cs])
        log_a = r * log_base[:, cs]
        a = jnp.exp(log_a)
        th = jnp.tanh(log_a)
        bt = jnp.sqrt(-2.0 * th) * lax.rsqrt(1.0 - th) * gi * xc
        for lt in range(C_HW // LANES):
            for b in range(nb_blk):
                dst = pl.ds(b * pitch, rows)
                src = slice(b * rows, (b + 1) * rows)
                a_scr[h * (C_HW // LANES) + lt, dst, :] = a[src, lt * LANES:(lt + 1) * LANES]
                b_scr[h * (C_HW // LANES) + lt, dst, :] = bt[src, lt * LANES:(lt + 1) * LANES]
    if carry_hist:
        ext_scr[:, 0:SUBLANES, :] = ext_scr[:, rows:rows + SUBLANES, :]

    n_lt = C_WIDTH // LANES

    def step(t, hcur):
        hnew = []
        for lt in range(n_lt):
            a_t = a_scr[lt, pl.ds(t, nb_blk, stride=pitch), :]
            b_t = b_scr[lt, pl.ds(t, nb_blk, stride=pitch), :]
            hn = a_t * hcur[lt] + b_t
            hs_scr[lt, pl.ds(t, nb_blk, stride=pitch), :] = hn
            hnew.append(hn)
        return tuple(hnew)

    h_init = tuple(h_scr[:, lt * LANES:(lt + 1) * LANES] for lt in range(n_lt))
    h_fin = lax.fori_loop(0, rows, step, h_init, unroll=SUBLANES)
    for lt in range(n_lt):
        cs = slice(lt * LANES, (lt + 1) * LANES)
        h_scr[:, cs] = h_fin[lt]
        for b in range(nb_blk):
            o_ref[b, :, cs] = _gelu_tanh(p_ref[b, :, cs]) * hs_scr[lt, pl.ds(b * pitch, rows), :]

    @pl.when(i == pl.num_programs(1) - 1)
    def _():
        hlast_ref[...] = h_scr[...]
        convnew_ref[...] = ext_scr[:, rows:rows + SUBLANES, :]


def _odd_mixer(x, row0, w_mm, convh8, h0, convw, convb, wr_mm, br, wi_mm, bi, lam,
               *, batch, seq, nb_blk, rows):
    assert seq % rows == 0 and batch % nb_blk == 0 and rows % SUBLANES == 0
    t_blocks = seq // rows
    pitch = rows + SUBLANES
    if t_blocks == 1:
        assert row0 % (nb_blk * rows) == 0
        off = row0 // (nb_blk * rows)
        x_specs = [pl.BlockSpec((nb_blk * rows, D_MODEL), lambda b, i: (b + off, 0))]
    else:
        assert row0 % rows == 0
        off = row0 // rows
        x_specs = [pl.BlockSpec((rows, D_MODEL),
                                lambda b, i, k=k: ((b * nb_blk + k) * t_blocks + i + off, 0))
                   for k in range(nb_blk)]
    n_x = len(x_specs)
    kern = functools.partial(_odd_kernel, n_x=n_x, nb_blk=nb_blk, rows=rows, carry_hist=t_blocks > 1)
    full = lambda shape: pl.BlockSpec(shape, lambda b, i: (0,) * len(shape))
    return pl.pallas_call(
        kern,
        grid=(batch // nb_blk, t_blocks),
        in_specs=x_specs + [
            full((D_MODEL, 2 * C_WIDTH)),
            pl.BlockSpec((nb_blk, SUBLANES, C_WIDTH), lambda b, i: (b, 0, 0)),
            pl.BlockSpec((nb_blk, C_WIDTH), lambda b, i: (b, 0)),
            full((4, C_WIDTH)), full((1, C_WIDTH)),
            full((C_HEADS, C_HW, C_HW)), full((1, C_WIDTH)),
            full((C_HEADS, C_HW, C_HW)), full((1, C_WIDTH)),
            full((1, C_WIDTH)),
        ],
        out_specs=[
            pl.BlockSpec((nb_blk, rows, C_WIDTH), lambda b, i: (b, i, 0)),
            pl.BlockSpec((nb_blk, C_WIDTH), lambda b, i: (b, 0)),
            pl.BlockSpec((nb_blk, SUBLANES, C_WIDTH), lambda b, i: (b, 0, 0)),
        ],
        out_shape=[jax.ShapeDtypeStruct((batch, seq, C_WIDTH), F32),
                   jax.ShapeDtypeStruct((batch, C_WIDTH), F32),
                   jax.ShapeDtypeStruct((batch, SUBLANES, C_WIDTH), F32)],
        scratch_shapes=[
            pltpu.VMEM((nb_blk, rows, 2 * C_WIDTH), F32),
            pltpu.VMEM((nb_blk, SUBLANES + rows, C_WIDTH), F32),
            pltpu.VMEM((C_WIDTH // LANES, nb_blk * pitch, LANES), F32),
            pltpu.VMEM((C_WIDTH // LANES, nb_blk * pitch, LANES), F32),
            pltpu.VMEM((C_WIDTH // LANES, nb_blk * pitch, LANES), F32),
            pltpu.VMEM((nb_blk, C_WIDTH), F32),
        ],
        compiler_params=_cparams(("parallel", "arbitrary")),
        name="odd_mixer",
    )(*([x] * n_x), w_mm, convh8, h0, convw, convb, wr_mm, br, wi_mm, bi, lam)


def _route(x1, rwt_ref, rb_ref):
    logits = lax.dot_general(rwt_ref[...], x1, (((1,), (1,)), ((), ())),
                             precision=HIGHEST, preferred_element_type=F32)
    sc = _sigmoid(logits)
    bz = sc + rb_ref[...]
    row = lambda arr, e: arr[e:e + 1, :]
    best = None
    gidx = None
    for g in range(N_GROUPS):
        r = [row(bz, EPG * g + k) for k in range(EPG)]
        top2 = None
        for a, b in PAIRS:
            s = r[a] + r[b]
            top2 = s if top2 is None else jnp.maximum(top2, s)
        if best is None:
            best, gidx = top2, jnp.zeros(top2.shape, I32)
        else:
            upd = top2 > best
            gidx = jnp.where(upd, g, gidx)
            best = jnp.where(upd, top2, best)
    sb, ss = [], []
    for k in range(EPG):
        vb, vs = row(bz, k), row(sc, k)
        for g in range(1, N_GROUPS):
            vb = jnp.where(gidx == g, row(bz, EPG * g + k), vb)
            vs = jnp.where(gidx == g, row(sc, EPG * g + k), vs)
        sb.append(vb)
        ss.append(vs)
    m1, i1 = sb[0], jnp.zeros(sb[0].shape, I32)
    for k in range(1, EPG):
        upd = sb[k] > m1
        i1 = jnp.where(upd, k, i1)
        m1 = jnp.where(upd, sb[k], m1)
    m2, i2 = None, None
    for k in range(EPG):
        cand = jnp.where(i1 == k, -jnp.inf, sb[k])
        if m2 is None:
            m2, i2 = cand, jnp.zeros(cand.shape, I32)
        else:
            upd = cand > m2
            i2 = jnp.where(upd, k, i2)
            m2 = jnp.where(upd, cand, m2)
    lo = jnp.minimum(i1, i2)
    hi = jnp.maximum(i1, i2)
    pair = jnp.where(lo == 0, 0, jnp.where(lo == 1, 3, 5)) + hi - lo - 1
    s_lo, s_hi = ss[0], ss[0]
    for k in range(1, EPG):
        s_lo = jnp.where(lo == k, ss[k], s_lo)
        s_hi = jnp.where(hi == k, ss[k], s_hi)
    den = s_lo + s_hi
    return gidx * len(PAIRS) + pair, s_lo / den, s_hi / den


def _mix_out_kernel(ap_ref, as_ref, w_ref, xp_ref, xs_ref, g_ref, b_ref, rwt_ref, rb_ref, upper_ref,
                    x1e_ref, key_ref, cnt_ref, run_scr, *, np_tiles):
    i = pl.program_id(0)
    tm = xp_ref.shape[0]

    @pl.when(i == 0)
    def _():
        run_scr[...] = jnp.zeros(run_scr.shape, F32)

    def finish(a_ref, x_ref):
        hmix = jnp.dot(a_ref[...].astype(MM), w_ref[...], preferred_element_type=F32)
        x1 = _layer_norm(ALPHA * x_ref[...] + hmix, g_ref[...], b_ref[...])
        cls, wa, wb = _route(x1, rwt_ref, rb_ref)
        crow = lax.broadcasted_iota(I32, (CLS_ROWS, tm), 0)
        onehot = (crow == cls).astype(F32)
        prefix = jnp.dot(onehot.astype(MM), upper_ref[...], preferred_element_type=F32)
        run = run_scr[:, 0:1]
        rank = jnp.sum(onehot * (prefix + (run - 1.0)), axis=0, keepdims=True)
        run_new = run + jnp.sum(onehot, axis=1, keepdims=True)
        run_scr[...] = jnp.broadcast_to(run_new, run_scr.shape)
        cnt_ref[...] = jnp.broadcast_to(run_new, cnt_ref.shape).astype(I32)
        key_ref[...] = cls * KEY_CLS + rank.astype(I32)
        wrow = lax.broadcasted_iota(I32, (LANES, tm), 0)
        wpad = jnp.where(wrow == 0, wa, jnp.where(wrow == 1, wb, 0.0))
        x1e_ref[:, 0:D_MODEL] = x1
        x1e_ref[:, D_MODEL:D_MODEL + LANES] = wpad.T

    @pl.when(i < np_tiles)
    def _():
        finish(ap_ref, xp_ref)

    @pl.when(i >= np_tiles)
    def _():
        finish(as_ref, xs_ref)


def _mix_out(a_p, a_s, w_mm, x_p, row0_p, x_s, row0_s, g, b, rwt, rb, tm):
    n = a_p.shape[0] + a_s.shape[0]
    np_tiles = a_p.shape[0] // tm
    ns_tiles = a_s.shape[0] // tm
    assert (np_tiles + ns_tiles) * tm == n and row0_p % tm == 0 and row0_s % tm == 0
    off_p, off_s = row0_p // tm, row0_s // tm
    kern = functools.partial(_mix_out_kernel, np_tiles=np_tiles)
    full = lambda shape: pl.BlockSpec(shape, lambda i: (0,) * len(shape))
    upper = (jnp.arange(tm)[:, None] <= jnp.arange(tm)[None, :]).astype(MM)
    return pl.pallas_call(
        kern,
        grid=(np_tiles + ns_tiles,),
        in_specs=[
            pl.BlockSpec((tm, D_MODEL), lambda i: (jnp.minimum(i, np_tiles - 1), 0)),
            pl.BlockSpec((tm, D_MODEL), lambda i: (jnp.maximum(i - np_tiles, 0), 0)),
            full((D_MODEL, D_MODEL)),
            pl.BlockSpec((tm, D_MODEL), lambda i: (jnp.minimum(i, np_tiles - 1) + off_p, 0)),
            pl.BlockSpec((tm, D_MODEL), lambda i: (jnp.maximum(i - np_tiles, 0) + off_s, 0)),
            full((1, D_MODEL)), full((1, D_MODEL)),
            full((N_EXPERTS, D_MODEL)), full((N_EXPERTS, 1)),
            full((tm, tm)),
        ],
        out_specs=[
            pl.BlockSpec((tm, X1E_W), lambda i: (i, 0)),
            pl.BlockSpec((1, tm), lambda i: (0, i)),
            full((CLS_ROWS, LANES)),
        ],
        out_shape=[jax.ShapeDtypeStruct((n, X1E_W), F32),
                   jax.ShapeDtypeStruct((1, n), I32),
                   jax.ShapeDtypeStruct((CLS_ROWS, LANES), I32)],
        scratch_shapes=[pltpu.VMEM((CLS_ROWS, LANES), F32)],
        compiler_params=_cparams(("arbitrary",)),
        name="mix_out",
    )(a_p, a_s, w_mm, x_p, x_s, g, b, rwt, rb, upper)


def _moe_kernel(nvalid_ref, chga_ref, chgb_ref, ea_ref, eb_ref, pos_ref,
                x_hbm, g_ref, b_ref, wga_ref, wua_ref, wda_ref, wgb_ref, wub_ref, wdb_ref,
                y_hbm, xbuf0, xbuf1, xbuf2, obuf0, obuf1, obuf2,
                ga_scr, ua_scr, da_scr, gb_scr, ub_scr, db_scr, rowsrc, gsem, ssem, *, tm, n_tok):
    t = pl.program_id(0)
    nt = pl.num_programs(0)
    xbufs, obufs = (xbuf0, xbuf1, xbuf2), (obuf0, obuf1, obuf2)

    def gather_copy(tok, r, sl):
        return pltpu.make_async_copy(x_hbm.at[pl.ds(tok, 1)], xbufs[sl].at[pl.ds(r, 1)], gsem.at[sl])

    def scatter_copy(tok, r, sl):
        return pltpu.make_async_copy(obufs[sl].at[pl.ds(r, 1)], y_hbm.at[pl.ds(tok, 1)], ssem.at[sl])

    def src_token(tile, n_tile, r):
        return rowsrc[jnp.where(r < n_tile, tile * tm + r, 0)]

    def tile_rows(tile):
        return jnp.where(tile < nt, nvalid_ref[jnp.minimum(tile, nt - 1)], 0)

    def issue_gather(tile, n_tile, sl, r, priority=0):
        gather_copy(src_token(tile, n_tile, r), r, sl).start(priority=priority)

    def issue_scatter(tile, n_tile, sl, r, priority=0):
        dst = jnp.where(r < n_tile, src_token(tile, n_tile, r), n_tok + sl * tm + r)
        scatter_copy(dst, r, sl).start(priority=priority)

    def wait_gather(sl):
        pltpu.make_async_copy(x_hbm.at[pl.ds(0, tm)], xbufs[sl], gsem.at[sl]).wait()

    def wait_scatter(sl):
        pltpu.make_async_copy(obufs[sl], y_hbm.at[pl.ds(0, tm)], ssem.at[sl]).wait()

    @pl.when(t == 0)
    def _():
        def build(i, carry):
            rowsrc[pos_ref[i]] = i
            return carry
        lax.fori_loop(0, n_tok, build, 0, unroll=16)

        def body(r, carry):
            issue_gather(0, tile_rows(0), 0, r)
            issue_gather(1, tile_rows(1), 1, r)
            return carry
        lax.fori_loop(0, tm, body, 0, unroll=8)
        for sl in range(3):
            obufs[sl][...] = jnp.zeros(obufs[sl].shape, obufs[sl].dtype)
        for sl in range(2):
            pltpu.make_async_copy(obufs[sl], y_hbm.at[pl.ds(n_tok + sl * tm, tm)], ssem.at[sl]).start()

    @pl.when(chga_ref[t] == 1)
    def _():
        ga_scr[...] = wga_ref[0, 0].astype(MM)
        ua_scr[...] = wua_ref[0, 0].astype(MM)
        da_scr[...] = wda_ref[0, 0].astype(MM)

    @pl.when(chgb_ref[t] == 1)
    def _():
        gb_scr[...] = wgb_ref[0, 0].astype(MM)
        ub_scr[...] = wub_ref[0, 0].astype(MM)
        db_scr[...] = wdb_ref[0, 0].astype(MM)

    prev = jnp.maximum(t - 1, 0)
    n_prev = jnp.where(t > 0, nvalid_ref[prev], 0)
    n_cur = nvalid_ref[t]

    n_ahead = tile_rows(t + 2)

    def tile_block(sl):
        other = (sl + 2) % 3
        wait_gather(sl)
        wait_scatter(sl)
        x1 = xbufs[sl][:, 0:D_MODEL]
        x = x1.astype(MM)

        def expert(wg, wu, wd, wrow):
            gate = jnp.dot(x, wg[...], preferred_element_type=F32)
            up = jnp.dot(x, wu[...], preferred_element_type=F32)
            hid = (_silu(gate) * up).astype(MM)
            return jnp.dot(hid, wd[...], preferred_element_type=F32) * wrow

        f = (expert(ga_scr, ua_scr, da_scr, xbufs[sl][:, D_MODEL:D_MODEL + 1])
             + expert(gb_scr, ub_scr, db_scr, xbufs[sl][:, D_MODEL + 1:D_MODEL + 2]))
        obufs[sl][...] = _layer_norm(ALPHA * x1 + f, g_ref[...], b_ref[...])
        for r in range(tm):
            issue_gather(t + 2, n_ahead, other, r, priority=r % 2)
            issue_scatter(prev, n_prev, other, r, priority=r % 2)

    def drain(sl):
        other = (sl + 2) % 3
        wait_gather(sl)
        wait_gather((sl + 1) % 3)

        def body(r, carry):
            issue_scatter(prev, n_prev, other, r)
            return carry
        lax.fori_loop(0, tm, body, 0, unroll=8)
        for k in range(3):
            wait_scatter(k)

    for sl in range(3):
        pl.when(jnp.logical_and(n_cur > 0, t % 3 == sl))(functools.partial(tile_block, sl))
        pl.when(jnp.logical_and(jnp.logical_and(n_cur == 0, n_prev > 0), t % 3 == sl))(
            functools.partial(drain, sl))


def _moe(x1e, key, cnt, ln_g, ln_b, w_gate, w_up, w_down, layer, tm):
    n = x1e.shape[0]
    n_tiles = n // tm + N_CLASSES
    n_rows = n_tiles * tm
    counts = cnt[:N_CLASSES, 0]
    tiles_c = (counts + tm - 1) // tm
    tile_end_c = jnp.cumsum(tiles_c)
    tile_start_c = tile_end_c - tiles_c
    total = tile_end_c[-1]
    tid = jnp.arange(n_tiles, dtype=I32)
    t_eff = jnp.minimum(tid, total - 1)
    tile_cls = jnp.minimum(jnp.sum((tile_end_c[None, :] <= t_eff[:, None]).astype(I32), axis=1),
                           N_CLASSES - 1)
    sel = (tile_cls[:, None] == jnp.arange(N_CLASSES, dtype=I32)[None, :]).astype(I32)
    left = jnp.sum(sel * counts[None, :], axis=1) - (tid - jnp.sum(sel * tile_start_c[None, :], axis=1)) * tm
    nvalid = jnp.where(tid < total, jnp.clip(left, 0, tm), 0).astype(I32)
    pair_id = tile_cls % len(PAIRS)
    pair_lo = sum(jnp.where(pair_id == k, p[0], 0) for k, p in enumerate(PAIRS))
    pair_hi = sum(jnp.where(pair_id == k, p[1], 0) for k, p in enumerate(PAIRS))
    ea = (EPG * (tile_cls // len(PAIRS)) + pair_lo).astype(I32)
    eb = (EPG * (tile_cls // len(PAIRS)) + pair_hi).astype(I32)
    first = tid == 0
    chga = jnp.logical_or(first, ea != jnp.roll(ea, 1)).astype(I32)
    chgb = jnp.logical_or(first, eb != jnp.roll(eb, 1)).astype(I32)
    key = key.reshape(n)
    tok_cls = key // KEY_CLS
    tok_sel = (tok_cls[:, None] == jnp.arange(N_CLASSES, dtype=I32)[None, :]).astype(I32)
    pos = (jnp.sum(tok_sel * tile_start_c[None, :], axis=1) * tm + key % KEY_CLS).astype(I32)

    kern = functools.partial(_moe_kernel, tm=tm, n_tok=n)
    wspec_a = lambda shape: pl.BlockSpec(
        shape, lambda t, nv, ca, cb, ea_, eb_, ps: (layer, ea_[t], 0, 0))
    wspec_b = lambda shape: pl.BlockSpec(
        shape, lambda t, nv, ca, cb, ea_, eb_, ps: (layer, eb_[t], 0, 0))
    vec = pl.BlockSpec((1, D_MODEL), lambda t, nv, ca, cb, ea_, eb_, ps: (0, 0))
    gu = (1, 1, D_MODEL, D_EXPERT)
    dn = (1, 1, D_EXPERT, D_MODEL)
    grid_spec = pltpu.PrefetchScalarGridSpec(
        num_scalar_prefetch=6,
        grid=(n_tiles,),
        in_specs=[pl.BlockSpec(memory_space=pl.ANY), vec, vec,
                  wspec_a(gu), wspec_a(gu), wspec_a(dn),
                  wspec_b(gu), wspec_b(gu), wspec_b(dn)],
        out_specs=pl.BlockSpec(memory_space=pl.ANY),
        scratch_shapes=[
            pltpu.VMEM((tm, X1E_W), F32), pltpu.VMEM((tm, X1E_W), F32), pltpu.VMEM((tm, X1E_W), F32),
            pltpu.VMEM((tm, D_MODEL), F32), pltpu.VMEM((tm, D_MODEL), F32),
            pltpu.VMEM((tm, D_MODEL), F32),
            pltpu.VMEM((D_MODEL, D_EXPERT), MM), pltpu.VMEM((D_MODEL, D_EXPERT), MM),
            pltpu.VMEM((D_EXPERT, D_MODEL), MM),
            pltpu.VMEM((D_MODEL, D_EXPERT), MM), pltpu.VMEM((D_MODEL, D_EXPERT), MM),
            pltpu.VMEM((D_EXPERT, D_MODEL), MM),
            pltpu.SMEM((n_rows,), I32),
            pltpu.SemaphoreType.DMA((3,)),
            pltpu.SemaphoreType.DMA((3,)),
        ],
    )
    return pl.pallas_call(
        kern,
        grid_spec=grid_spec,
        out_shape=jax.ShapeDtypeStruct((n + 3 * tm, D_MODEL), F32),
        compiler_params=_cparams(("arbitrary",)),
        name="moe",
    )(nvalid, chga, chgb, ea, eb, pos,
      x1e, ln_g, ln_b, w_gate, w_up, w_down, w_gate, w_up, w_down)


def _pad_hist(hist, rows):
    b, r, c = hist.shape
    return jnp.concatenate([jnp.zeros((b, rows - r, c), hist.dtype), hist], axis=1)


def kernel(x_prompt, x_sample, state_delta, state_delta_conv, state_pool, state_lru, state_lru_conv,
           ab_w_in, a_conv_w, a_log_decay, a_dt_bias, a_norm_w, b_w_group, b_scale, ab_w_out,
           c_w_in, c_conv_w, c_conv_b, c_w_r, c_b_r, c_w_i, c_b_i, c_lambda, c_w_out,
           ln_mix_g, ln_mix_b, ln_ffn_g, ln_ffn_b, router_w, router_bias,
           moe_w_gate, moe_w_up, moe_w_down):
    bp, tp, d = x_prompt.shape
    bs, ts, _ = x_sample.shape
    n_p, n_s = bp * tp, bs * ts
    n = n_p + n_s
    tm = TOKEN_TILE
    tmix = MIX_TILE
    assert n_p % tmix == 0 and n_s % tmix == 0 and tmix % tm == 0

    xin = [(x_prompt.reshape(n_p, d), 0), (x_sample.reshape(n_s, d), 0)]
    groups = (
        dict(batch=bp, seq=tp, hist_valid=0, fresh=True,
             ev=dict(nb_blk=1, rows=min(tp, 256)), od=dict(nb_blk=min(bp, 8), rows=min(tp, 64))),
        dict(batch=bs, seq=ts, hist_valid=B_HIST, fresh=False,
             ev=dict(nb_blk=min(bs, 8), rows=ts), od=dict(nb_blk=min(bs, 32), rows=ts)),
    )
    rwt = router_w.T
    rb = router_bias.reshape(N_EXPERTS, 1)
    row = lambda v: v.reshape(1, -1)

    new = {k: ([], []) for k in ("dconv", "pool", "lru", "lconv")}
    delta_new = [None, None]
    for layer in range(DEPTH):
        j = layer // 2
        mixed = []
        if layer % 2 == 0:
            w = ab_w_in[j]
            c1 = EV_Z0 + A_V
            w_perm = jnp.concatenate(
                [w[:, :c1], w[:, c1 + 2 * A_HEADS:], w[:, c1:c1 + 2 * A_HEADS],
                 jnp.zeros((d, LANES - 2 * A_HEADS), w.dtype)], axis=1).astype(MM)
            gvec = jnp.zeros((2, LANES), F32)
            gvec = gvec.at[0, A_HEADS:2 * A_HEADS].set(a_log_decay[j])
            gvec = gvec.at[1, A_HEADS:2 * A_HEADS].set(a_dt_bias[j])
            for gi, g in enumerate(groups):
                b_, t_ = g["batch"], g["seq"]
                if g["fresh"]:
                    dconv = jnp.zeros((b_, 3, A_CONV_CH), F32)
                    delta_all = jnp.zeros((1, b_, A_HEADS, A_DK, A_DV), F32)
                    pool = jnp.zeros((b_, B_HIST, B_WIDTH), F32)
                else:
                    dconv, delta_all, pool = state_delta_conv[j], state_delta, state_pool[j]
                o, delta_new[gi], conv8, pool16 = _even_mixer(
                    xin[gi][0], xin[gi][1], w_perm, _pad_hist(dconv, SUBLANES), delta_all, j,
                    _pad_hist(pool, 2 * SUBLANES), a_conv_w[j], gvec, row(a_norm_w[j]),
                    b_w_group[j].astype(MM), row(b_scale[j]), delta_new[gi],
                    batch=b_, seq=t_, hist_valid=g["hist_valid"], **g["ev"])
                mixed.append(o)
                new["dconv"][gi].append(conv8[:, SUBLANES - 3:])
                new["pool"][gi].append(pool16[:, 2 * SUBLANES - B_HIST:])
            w_out = ab_w_out[j].astype(MM)
        else:
            w_mm = c_w_in[j].astype(MM)
            for gi, g in enumerate(groups):
                b_, t_ = g["batch"], g["seq"]
                if g["fresh"]:
                    lconv = jnp.zeros((b_, 3, C_WIDTH), F32)
                    lru = jnp.zeros((b_, C_WIDTH), F32)
                else:
                    lconv, lru = state_lru_conv[j], state_lru[j]
                o3, h_last, conv8 = _odd_mixer(
                    xin[gi][0], xin[gi][1], w_mm, _pad_hist(lconv, SUBLANES), lru, c_conv_w[j],
                    row(c_conv_b[j]), c_w_r[j].astype(MM), row(c_b_r[j]), c_w_i[j].astype(MM),
                    row(c_b_i[j]), row(c_lambda[j]), batch=b_, seq=t_, **g["od"])
                mixed.append(o3.reshape(b_ * t_, C_WIDTH))
                new["lru"][gi].append(h_last)
                new["lconv"][gi].append(conv8[:, SUBLANES - 3:])
            w_out = c_w_out[j].astype(MM)

        x1e, key, cnt = _mix_out(mixed[0], mixed[1], w_out, xin[0][0], xin[0][1], xin[1][0], xin[1][1],
                                 row(ln_mix_g[layer]), row(ln_mix_b[layer]), rwt, rb, tmix)
        x = _moe(x1e, key, cnt, row(ln_ffn_g[layer]), row(ln_ffn_b[layer]),
                 moe_w_gate, moe_w_up, moe_w_down, layer, tm)
        xin = [(x, 0), (x, n_p)]

    stack = lambda key, gi: jnp.stack(new[key][gi])
    return (x[:n_p].reshape(bp, tp, d), x[n_p:n].reshape(bs, ts, d),
            delta_new[0], stack("dconv", 0), stack("pool", 0), stack("lru", 0), stack("lconv", 0),
            delta_new[1], stack("dconv", 1), stack("pool", 1), stack("lru", 1), stack("lconv", 1))
```

```python
import functools

import jax
import jax.numpy as jnp
from jax import lax
from jax.experimental import pallas as pl
from jax.experimental.pallas import tpu as pltpu

F32 = jnp.float32
I32 = jnp.int32
MM = jnp.bfloat16
HIGHEST = lax.Precision.HIGHEST

D_MODEL = 1024
DEPTH = 4
N_A_LAYERS = (DEPTH + 1) // 2
A_HEADS = 4
A_DK = 128
A_DV = 128
A_QK = A_HEADS * A_DK
A_V = A_HEADS * A_DV
A_CONV_CH = 2 * A_QK + A_V
A_CHUNK = 64
B_GROUPS = 4
B_GW = 128
B_WINDOWS = (2, 4, 8, 16)
B_HIST = 15
B_WIDTH = B_GROUPS * B_GW
C_WIDTH = D_MODEL
C_HEADS = 4
C_HW = C_WIDTH // C_HEADS
C_GATE = 8.0
N_EXPERTS = 16
N_GROUPS = 4
EPG = N_EXPERTS // N_GROUPS
D_EXPERT = 512
ALPHA = (2 * DEPTH) ** 0.25
LN_EPS = 1e-5
RMS_EPS = 1e-6

LANES = 128
SUBLANES = 8
VMEM_LIMIT = 56 * 1024 * 1024

EV_Z0 = A_CONV_CH
EV_U0 = EV_Z0 + A_V
EV_BA0 = EV_U0 + B_WIDTH
EV_W = EV_BA0 + LANES

PAIRS = ((0, 1), (0, 2), (0, 3), (1, 2), (1, 3), (2, 3))
N_CLASSES = N_GROUPS * len(PAIRS)

TOKEN_TILE = 256
MIX_TILE = 512
CLS_ROWS = 32
KEY_CLS = 1 << 16
X1E_W = D_MODEL + LANES


def _cparams(sem):
    return pltpu.CompilerParams(dimension_semantics=sem, vmem_limit_bytes=VMEM_LIMIT)


def _dot(a, b):
    return jnp.dot(a.astype(MM), b.astype(MM), preferred_element_type=F32)


def _dot_nt(a, b):
    return lax.dot_general(a.astype(MM), b.astype(MM), (((1,), (1,)), ((), ())),
                           preferred_element_type=F32)


def _dot_tn(a, b):
    return lax.dot_general(a.astype(MM), b.astype(MM), (((0,), (0,)), ((), ())),
                           preferred_element_type=F32)


def _sigmoid(x):
    return 1.0 / (1.0 + jnp.exp(-x))


def _sigmoid_t(x):
    return 0.5 * jnp.tanh(0.5 * x) + 0.5


def _silu(x):
    return x * _sigmoid_t(x)


def _softplus(x):
    return jnp.maximum(x, 0.0) + jnp.log1p(jnp.exp(-jnp.abs(x)))


def _gelu_tanh(x):
    return x * (0.5 * (1.0 + jnp.tanh(0.7978845608028654 * (x + 0.044715 * (x * x * x)))))


def _layer_norm(v, g, b):
    mu = jnp.mean(v, axis=-1, keepdims=True)
    d = v - mu
    var = jnp.mean(d * d, axis=-1, keepdims=True)
    return d * lax.rsqrt(var + LN_EPS) * g + b


def _proj_kernel(x_ref, w_ref, o_ref):
    o_ref[...] = jnp.dot(x_ref[...].astype(MM), w_ref[...], preferred_element_type=F32)


def _proj(x, w_mm, row0, nrows, tm):
    k, width = w_mm.shape
    off = row0 // tm
    return pl.pallas_call(
        _proj_kernel,
        grid=(nrows // tm,),
        in_specs=[pl.BlockSpec((tm, k), lambda i: (i + off, 0)),
                  pl.BlockSpec((k, width), lambda i: (0, 0))],
        out_specs=pl.BlockSpec((tm, width), lambda i: (i, 0)),
        out_shape=jax.ShapeDtypeStruct((nrows, width), F32),
        compiler_params=_cparams(("parallel",)),
        name="proj",
    )(x, w_mm)


def _even_kernel(*refs, nb_blk, rows, chunk, hist_valid, carry_hist, has_prev):
    (p_ref, convh_ref, s0_ref, poolh_ref, convw_ref, gvec_ref, normw_ref,
     wgrp_ref, scale_ref) = refs[:9]
    o_ref, snew_ref, ext_scr, qkv_scr, pext_scr, s_scr = refs[9 + (1 if has_prev else 0):]
    i = pl.program_id(1)
    n_chunks = rows // chunk
    n_neumann = max((chunk - 1).bit_length() - 1, 0)
    hb = 2 * SUBLANES

    @pl.when(i == 0)
    def _():
        s_scr[...] = s0_ref[0]
        ext_scr[:, 0:SUBLANES, :] = convh_ref[...]
        pext_scr[:, 0:hb, :] = poolh_ref[...]

    rid = lax.broadcasted_iota(I32, (chunk, chunk), 0)
    cid = lax.broadcasted_iota(I32, (chunk, chunk), 1)
    incl = rid >= cid
    strict = rid > cid
    eye = (rid == cid).astype(F32)
    ltri = incl.astype(F32)
    lane = lax.broadcasted_iota(I32, (chunk, LANES), 1)
    trow = lax.broadcasted_iota(I32, (rows, LANES), 0)
    pos = (i * rows + trow + (1 + hist_valid)).astype(F32)
    neg_decay_rate = -jnp.exp(gvec_ref[0:1, :])
    dt_bias = gvec_ref[1:2, :]

    for nb in range(nb_blk):
        r0 = nb * rows
        ext_scr[nb, SUBLANES:SUBLANES + rows, :] = p_ref[r0:r0 + rows, 0:A_CONV_CH]
        for ct in range(A_CONV_CH // LANES):
            cs = slice(ct * LANES, (ct + 1) * LANES)
            acc = None
            for j in range(4):
                lo = SUBLANES - 3 + j
                term = ext_scr[nb, lo:lo + rows, cs] * convw_ref[j:j + 1, cs]
                acc = term if acc is None else acc + term
            t = _silu(acc)
            if ct < 2 * A_HEADS:
                t = t * lax.rsqrt(jnp.sum(t * t, axis=-1, keepdims=True) + 1e-6)
                if ct < A_HEADS:
                    t = t * (A_DK ** -0.5)
            qkv_scr[r0:r0 + rows, cs] = t
        if carry_hist:
            ext_scr[nb, 0:SUBLANES, :] = ext_scr[nb, rows:rows + SUBLANES, :]

        pext_scr[nb, hb:hb + rows, :] = p_ref[r0:r0 + rows, EV_U0:EV_U0 + B_WIDTH]
        for gi, w in enumerate(B_WINDOWS):
            cs = slice(gi * B_GW, (gi + 1) * B_GW)
            cur = pext_scr[nb, hb:hb + rows, cs]
            tot = cur
            for j in range(1, w):
                tot = tot + pext_scr[nb, hb - j:hb - j + rows, cs]
            pooled = tot / jnp.minimum(pos, float(w)) - cur
            ob = _dot(pooled, wgrp_ref[gi]) * scale_ref[:, cs]
            o_ref[r0:r0 + rows, A_V + gi * B_GW:A_V + (gi + 1) * B_GW] = ob
        if carry_hist:
            pext_scr[nb, 0:hb, :] = pext_scr[nb, rows:rows + hb, :]

    chunks = [(nb, c) for nb in range(nb_blk) for c in range(n_chunks)]
    probs = [(nb, c, h) for nb, c in chunks for h in range(A_HEADS)]
    rs = {(nb, c): slice(nb * rows + c * chunk, nb * rows + (c + 1) * chunk) for nb, c in chunks}
    gcol = lambda arr, h: arr[:, A_HEADS + h:A_HEADS + h + 1]
    qf = lambda p: qkv_scr[rs[p[:2]], p[2] * LANES:(p[2] + 1) * LANES]
    kf = lambda p: qkv_scr[rs[p[:2]], A_QK + p[2] * LANES:A_QK + (p[2] + 1) * LANES]
    vf = lambda p: qkv_scr[rs[p[:2]], 2 * A_QK + p[2] * LANES:2 * A_QK + (p[2] + 1) * LANES]

    ba = {ck: p_ref[rs[ck], EV_BA0:EV_BA0 + LANES] for ck in chunks}
    beta_full = {ck: _sigmoid(ba[ck]) for ck in chunks}
    g_full = {ck: neg_decay_rate * _softplus(ba[ck] + dt_bias) for ck in chunks}
    gc_full = {ck: jnp.dot(ltri, g_full[ck], precision=HIGHEST, preferred_element_type=F32)
               for ck in chunks}
    eg_full = {ck: jnp.exp(gc_full[ck]) for ck in chunks}
    gl_full = {ck: gc_full[ck][chunk - 1:chunk, :] for ck in chunks}
    ekd_full = {ck: jnp.exp(gl_full[ck] - gc_full[ck]) for ck in chunks}
    egl_full = {ck: jnp.exp(gl_full[ck]) for ck in chunks}
    sel = [(lane == A_HEADS + h).astype(F32) for h in range(A_HEADS)]
    grow = {p: lax.dot_general(sel[p[2]], gc_full[p[:2]], (((1,), (1,)), ((), ())),
                               precision=HIGHEST, preferred_element_type=F32) for p in probs}
    decay = {p: jnp.where(incl, jnp.exp(jnp.where(incl, gcol(gc_full[p[:2]], p[2]) - grow[p], 0.0)), 0.0)
             for p in probs}
    beta = {p: beta_full[p[:2]][:, p[2]:p[2] + 1] for p in probs}
    kk = {p: _dot_nt(kf(p) * beta[p], kf(p)) for p in probs}
    qk = {p: _dot_nt(qf(p), kf(p)) for p in probs}
    a_intra = {p: jnp.where(incl, qk[p] * decay[p], 0.0) for p in probs}
    power = {p: -jnp.where(strict, kk[p] * decay[p], 0.0) for p in probs}
    tinv = {p: eye + power[p] for p in probs}
    for _ in range(n_neumann):
        power = {p: _dot(power[p], power[p]) for p in probs}
        tinv = {p: tinv[p] + _dot(tinv[p], power[p]) for p in probs}
    u_in = {p: _dot(tinv[p], vf(p) * beta[p]) for p in probs}
    w_in = {p: _dot(tinv[p], kf(p) * beta[p] * gcol(eg_full[p[:2]], p[2])) for p in probs}

    for c in range(n_chunks):
        cp = [(nb, c, h) for nb in range(nb_blk) for h in range(A_HEADS)]
        s_old = {p: s_scr[p[0], p[2]] for p in cp}
        wq = {p: _dot(jnp.concatenate([w_in[p], qf(p) * gcol(eg_full[p[:2]], p[2])], axis=0), s_old[p])
              for p in cp}
        u_new = {p: u_in[p] - wq[p][0:chunk] for p in cp}
        au = {p: _dot(a_intra[p], u_new[p]) for p in cp}
        ku = {p: _dot_tn(kf(p) * gcol(ekd_full[p[:2]], p[2]), u_new[p]) for p in cp}
        for p in cp:
            nb, _, h = p
            s_scr[nb, h] = s_old[p] * gcol(egl_full[p[:2]], h) + ku[p]
            o = wq[p][chunk:2 * chunk] + au[p]
            o = o * lax.rsqrt(jnp.mean(o * o, axis=-1, keepdims=True) + RMS_EPS) * normw_ref[...]
            z = p_ref[rs[p[:2]], EV_Z0 + h * LANES:EV_Z0 + (h + 1) * LANES]
            o_ref[rs[p[:2]], h * LANES:(h + 1) * LANES] = o * _silu(z)

    @pl.when(i == pl.num_programs(1) - 1)
    def _():
        snew_ref[0] = s_scr[...]
        if not has_prev:
            for other in range(1, snew_ref.shape[0]):
                snew_ref[other] = jnp.zeros(s_scr.shape, F32)


def _even_mixer(proj, convh8, s0_all, layer_j, poolh16, convw, gvec, normw, wgrp_mm, scale, snew_prev,
                *, batch, seq, nb_blk, rows, hist_valid):
    chunk = min(A_CHUNK, seq)
    assert seq % rows == 0 and rows % chunk == 0 and batch % nb_blk == 0
    assert nb_blk == 1 or rows == seq
    t_blocks = seq // rows
    blk = nb_blk * rows
    hb = 2 * SUBLANES
    has_prev = snew_prev is not None
    slot = layer_j if s0_all.shape[0] > 1 else 0
    kern = functools.partial(_even_kernel, nb_blk=nb_blk, rows=rows, chunk=chunk,
                             hist_valid=hist_valid, carry_hist=t_blocks > 1, has_prev=has_prev)
    full = lambda shape: pl.BlockSpec(shape, lambda b, i: (0,) * len(shape))
    state_blk = (1, nb_blk, A_HEADS, A_DK, A_DV)
    in_specs = [
        pl.BlockSpec((blk, EV_W), lambda b, i: (b * t_blocks + i, 0)),
        pl.BlockSpec((nb_blk, SUBLANES, A_CONV_CH), lambda b, i: (b, 0, 0)),
        pl.BlockSpec(state_blk, lambda b, i: (slot, b, 0, 0, 0)),
        pl.BlockSpec((nb_blk, hb, B_WIDTH), lambda b, i: (b, 0, 0)),
        full((4, A_CONV_CH)), full((2, LANES)), full((1, A_DV)),
        full((B_GROUPS, B_GW, B_GW)), full((1, B_WIDTH)),
    ]
    args = [proj, convh8, s0_all, poolh16, convw, gvec, normw, wgrp_mm, scale]
    if has_prev:
        in_specs.append(pl.BlockSpec(memory_space=pl.ANY))
        args.append(snew_prev)
        snew_spec = pl.BlockSpec(state_blk, lambda b, i: (layer_j, b, 0, 0, 0))
    else:
        snew_spec = pl.BlockSpec((N_A_LAYERS,) + state_blk[1:], lambda b, i: (0, b, 0, 0, 0))
    return pl.pallas_call(
        kern,
        grid=(batch // nb_blk, t_blocks),
        in_specs=in_specs,
        out_specs=[
            pl.BlockSpec((blk, D_MODEL), lambda b, i: (b * t_blocks + i, 0)),
            snew_spec,
        ],
        out_shape=[jax.ShapeDtypeStruct((batch * seq, D_MODEL), F32),
                   jax.ShapeDtypeStruct((N_A_LAYERS, batch, A_HEADS, A_DK, A_DV), F32)],
        scratch_shapes=[
            pltpu.VMEM((nb_blk, SUBLANES + rows, A_CONV_CH), F32),
            pltpu.VMEM((blk, A_CONV_CH), F32),
            pltpu.VMEM((nb_blk, hb + rows, B_WIDTH), F32),
            pltpu.VMEM((nb_blk, A_HEADS, A_DK, A_DV), F32),
        ],
        input_output_aliases={len(args) - 1: 1} if has_prev else {},
        compiler_params=_cparams(("parallel", "arbitrary")),
        name="even_mixer",
    )(*args)


def _odd_kernel(p_ref, convh_ref, h0_ref, convw_ref, convb_ref, wr_ref, br_ref, wi_ref, bi_ref,
                lam_ref, o_ref, hlast_ref, ext_scr, a_scr, b_scr, hs_scr, h_scr,
                *, nb_blk, rows, carry_hist):
    i = pl.program_id(1)
    blk = nb_blk * rows
    pitch = a_scr.shape[1] // nb_blk

    @pl.when(i == 0)
    def _():
        h_scr[...] = h0_ref[...]
        ext_scr[:, 0:SUBLANES, :] = convh_ref[...]

    ext_scr[:, SUBLANES:SUBLANES + rows, :] = p_ref[:, :, C_WIDTH:2 * C_WIDTH]
    log_base = -C_GATE * _softplus(-lam_ref[...])
    for h in range(C_HEADS):
        cs = slice(h * C_HW, (h + 1) * C_HW)
        acc = None
        for j in range(4):
            lo = SUBLANES - 3 + j
            term = ext_scr[:, lo:lo + rows, cs] * convw_ref[j:j + 1, cs]
            acc = term if acc is None else acc + term
        xc = (acc + convb_ref[:, cs]).reshape(blk, C_HW)
        r = _sigmoid_t(_dot(xc, wr_ref[h]) + br_ref[:, cs])
        gi = _sigmoid_t(_dot(xc, wi_ref[h]) + bi_ref[:, cs])
        log_a = r * log_base[:, cs]
        a = jnp.exp(log_a)
        th = jnp.tanh(log_a)
        bt = jnp.sqrt(-2.0 * th) * lax.rsqrt(1.0 - th) * gi * xc
        for lt in range(C_HW // LANES):
            for b in range(nb_blk):
                dst = pl.ds(b * pitch, rows)
                src = slice(b * rows, (b + 1) * rows)
                a_scr[h * (C_HW // LANES) + lt, dst, :] = a[src, lt * LANES:(lt + 1) * LANES]
                b_scr[h * (C_HW // LANES) + lt, dst, :] = bt[src, lt * LANES:(lt + 1) * LANES]
    if carry_hist:
        ext_scr[:, 0:SUBLANES, :] = ext_scr[:, rows:rows + SUBLANES, :]

    n_lt = C_WIDTH // LANES

    def step(t, hcur):
        hnew = []
        for lt in range(n_lt):
            a_t = a_scr[lt, pl.ds(t, nb_blk, stride=pitch), :]
            b_t = b_scr[lt, pl.ds(t, nb_blk, stride=pitch), :]
            hn = a_t * hcur[lt] + b_t
            hs_scr[lt, pl.ds(t, nb_blk, stride=pitch), :] = hn
            hnew.append(hn)
        return tuple(hnew)

    h_init = tuple(h_scr[:, lt * LANES:(lt + 1) * LANES] for lt in range(n_lt))
    h_fin = lax.fori_loop(0, rows, step, h_init, unroll=SUBLANES)
    for lt in range(n_lt):
        cs = slice(lt * LANES, (lt + 1) * LANES)
        h_scr[:, cs] = h_fin[lt]
        for b in range(nb_blk):
            o_ref[b, :, cs] = _gelu_tanh(p_ref[b, :, cs]) * hs_scr[lt, pl.ds(b * pitch, rows), :]

    @pl.when(i == pl.num_programs(1) - 1)
    def _():
        hlast_ref[...] = h_scr[...]


def _odd_mixer(proj3, convh8, h0, convw, convb, wr_mm, br, wi_mm, bi, lam, *, nb_blk, rows):
    batch, seq, _ = proj3.shape
    assert seq % rows == 0 and batch % nb_blk == 0 and rows % SUBLANES == 0
    t_blocks = seq // rows
    pitch = rows + SUBLANES
    kern = functools.partial(_odd_kernel, nb_blk=nb_blk, rows=rows, carry_hist=t_blocks > 1)
    full = lambda shape: pl.BlockSpec(shape, lambda b, i: (0,) * len(shape))
    return pl.pallas_call(
        kern,
        grid=(batch // nb_blk, t_blocks),
        in_specs=[
            pl.BlockSpec((nb_blk, rows, 2 * C_WIDTH), lambda b, i: (b, i, 0)),
            pl.BlockSpec((nb_blk, SUBLANES, C_WIDTH), lambda b, i: (b, 0, 0)),
            pl.BlockSpec((nb_blk, C_WIDTH), lambda b, i: (b, 0)),
            full((4, C_WIDTH)), full((1, C_WIDTH)),
            full((C_HEADS, C_HW, C_HW)), full((1, C_WIDTH)),
            full((C_HEADS, C_HW, C_HW)), full((1, C_WIDTH)),
            full((1, C_WIDTH)),
        ],
        out_specs=[
            pl.BlockSpec((nb_blk, rows, C_WIDTH), lambda b, i: (b, i, 0)),
            pl.BlockSpec((nb_blk, C_WIDTH), lambda b, i: (b, 0)),
        ],
        out_shape=[jax.ShapeDtypeStruct((batch, seq, C_WIDTH), F32),
                   jax.ShapeDtypeStruct((batch, C_WIDTH), F32)],
        scratch_shapes=[
            pltpu.VMEM((nb_blk, SUBLANES + rows, C_WIDTH), F32),
            pltpu.VMEM((C_WIDTH // LANES, nb_blk * pitch, LANES), F32),
            pltpu.VMEM((C_WIDTH // LANES, nb_blk * pitch, LANES), F32),
            pltpu.VMEM((C_WIDTH // LANES, nb_blk * pitch, LANES), F32),
            pltpu.VMEM((nb_blk, C_WIDTH), F32),
        ],
        compiler_params=_cparams(("parallel", "arbitrary")),
        name="odd_mixer",
    )(proj3, convh8, h0, convw, convb, wr_mm, br, wi_mm, bi, lam)


def _route(x1, rwt_ref, rb_ref):
    logits = lax.dot_general(rwt_ref[...], x1, (((1,), (1,)), ((), ())),
                             precision=HIGHEST, preferred_element_type=F32)
    sc = _sigmoid(logits)
    bz = sc + rb_ref[...]
    row = lambda arr, e: arr[e:e + 1, :]
    best = None
    gidx = None
    for g in range(N_GROUPS):
        r = [row(bz, EPG * g + k) for k in range(EPG)]
        top2 = None
        for a, b in PAIRS:
            s = r[a] + r[b]
            top2 = s if top2 is None else jnp.maximum(top2, s)
        if best is None:
            best, gidx = top2, jnp.zeros(top2.shape, I32)
        else:
            upd = top2 > best
            gidx = jnp.where(upd, g, gidx)
            best = jnp.where(upd, top2, best)
    sb, ss = [], []
    for k in range(EPG):
        vb, vs = row(bz, k), row(sc, k)
        for g in range(1, N_GROUPS):
            vb = jnp.where(gidx == g, row(bz, EPG * g + k), vb)
            vs = jnp.where(gidx == g, row(sc, EPG * g + k), vs)
        sb.append(vb)
        ss.append(vs)
    m1, i1 = sb[0], jnp.zeros(sb[0].shape, I32)
    for k in range(1, EPG):
        upd = sb[k] > m1
        i1 = jnp.where(upd, k, i1)
        m1 = jnp.where(upd, sb[k], m1)
    m2, i2 = None, None
    for k in range(EPG):
        cand = jnp.where(i1 == k, -jnp.inf, sb[k])
        if m2 is None:
            m2, i2 = cand, jnp.zeros(cand.shape, I32)
        else:
            upd = cand > m2
            i2 = jnp.where(upd, k, i2)
            m2 = jnp.where(upd, cand, m2)
    lo = jnp.minimum(i1, i2)
    hi = jnp.maximum(i1, i2)
    pair = jnp.where(lo == 0, 0, jnp.where(lo == 1, 3, 5)) + hi - lo - 1
    s_lo, s_hi = ss[0], ss[0]
    for k in range(1, EPG):
        s_lo = jnp.where(lo == k, ss[k], s_lo)
        s_hi = jnp.where(hi == k, ss[k], s_hi)
    den = s_lo + s_hi
    return gidx * len(PAIRS) + pair, s_lo / den, s_hi / den


def _mix_out_kernel(ap_ref, as_ref, w_ref, xp_ref, xs_ref, g_ref, b_ref, rwt_ref, rb_ref, upper_ref,
                    x1e_ref, key_ref, cnt_ref, run_scr, *, np_tiles):
    i = pl.program_id(0)
    tm = xp_ref.shape[0]

    @pl.when(i == 0)
    def _():
        run_scr[...] = jnp.zeros(run_scr.shape, F32)

    def finish(a_ref, x_ref):
        hmix = jnp.dot(a_ref[...].astype(MM), w_ref[...], preferred_element_type=F32)
        x1 = _layer_norm(ALPHA * x_ref[...] + hmix, g_ref[...], b_ref[...])
        cls, wa, wb = _route(x1, rwt_ref, rb_ref)
        crow = lax.broadcasted_iota(I32, (CLS_ROWS, tm), 0)
        onehot = (crow == cls).astype(F32)
        prefix = jnp.dot(onehot.astype(MM), upper_ref[...], preferred_element_type=F32)
        run = run_scr[:, 0:1]
        rank = jnp.sum(onehot * (prefix + (run - 1.0)), axis=0, keepdims=True)
        run_new = run + jnp.sum(onehot, axis=1, keepdims=True)
        run_scr[...] = jnp.broadcast_to(run_new, run_scr.shape)
        cnt_ref[...] = jnp.broadcast_to(run_new, cnt_ref.shape).astype(I32)
        key_ref[...] = cls * KEY_CLS + rank.astype(I32)
        wrow = lax.broadcasted_iota(I32, (LANES, tm), 0)
        wpad = jnp.where(wrow == 0, wa, jnp.where(wrow == 1, wb, 0.0))
        x1e_ref[:, 0:D_MODEL] = x1
        x1e_ref[:, D_MODEL:D_MODEL + LANES] = wpad.T

    @pl.when(i < np_tiles)
    def _():
        finish(ap_ref, xp_ref)

    @pl.when(i >= np_tiles)
    def _():
        finish(as_ref, xs_ref)


def _mix_out(a_p, a_s, w_mm, x_p, row0_p, x_s, row0_s, g, b, rwt, rb, tm):
    n = a_p.shape[0] + a_s.shape[0]
    np_tiles = a_p.shape[0] // tm
    ns_tiles = a_s.shape[0] // tm
    assert (np_tiles + ns_tiles) * tm == n and row0_p % tm == 0 and row0_s % tm == 0
    off_p, off_s = row0_p // tm, row0_s // tm
    kern = functools.partial(_mix_out_kernel, np_tiles=np_tiles)
    full = lambda shape: pl.BlockSpec(shape, lambda i: (0,) * len(shape))
    upper = (jnp.arange(tm)[:, None] <= jnp.arange(tm)[None, :]).astype(MM)
    return pl.pallas_call(
        kern,
        grid=(np_tiles + ns_tiles,),
        in_specs=[
            pl.BlockSpec((tm, D_MODEL), lambda i: (jnp.minimum(i, np_tiles - 1), 0)),
            pl.BlockSpec((tm, D_MODEL), lambda i: (jnp.maximum(i - np_tiles, 0), 0)),
            full((D_MODEL, D_MODEL)),
            pl.BlockSpec((tm, D_MODEL), lambda i: (jnp.minimum(i, np_tiles - 1) + off_p, 0)),
            pl.BlockSpec((tm, D_MODEL), lambda i: (jnp.maximum(i - np_tiles, 0) + off_s, 0)),
            full((1, D_MODEL)), full((1, D_MODEL)),
            full((N_EXPERTS, D_MODEL)), full((N_EXPERTS, 1)),
            full((tm, tm)),
        ],
        out_specs=[
            pl.BlockSpec((tm, X1E_W), lambda i: (i, 0)),
            pl.BlockSpec((1, tm), lambda i: (0, i)),
            full((CLS_ROWS, LANES)),
        ],
        out_shape=[jax.ShapeDtypeStruct((n, X1E_W), F32),
                   jax.ShapeDtypeStruct((1, n), I32),
                   jax.ShapeDtypeStruct((CLS_ROWS, LANES), I32)],
        scratch_shapes=[pltpu.VMEM((CLS_ROWS, LANES), F32)],
        compiler_params=_cparams(("arbitrary",)),
        name="mix_out",
    )(a_p, a_s, w_mm, x_p, x_s, g, b, rwt, rb, upper)


def _moe_kernel(nvalid_ref, chga_ref, chgb_ref, ea_ref, eb_ref, pos_ref,
                x_hbm, g_ref, b_ref, wga_ref, wua_ref, wda_ref, wgb_ref, wub_ref, wdb_ref,
                y_hbm, xbuf0, xbuf1, xbuf2, obuf0, obuf1, obuf2,
                ga_scr, ua_scr, da_scr, gb_scr, ub_scr, db_scr, rowsrc, gsem, ssem, *, tm, n_tok):
    t = pl.program_id(0)
    nt = pl.num_programs(0)
    xbufs, obufs = (xbuf0, xbuf1, xbuf2), (obuf0, obuf1, obuf2)

    def gather_copy(tok, r, sl):
        return pltpu.make_async_copy(x_hbm.at[pl.ds(tok, 1)], xbufs[sl].at[pl.ds(r, 1)], gsem.at[sl])

    def scatter_copy(tok, r, sl):
        return pltpu.make_async_copy(obufs[sl].at[pl.ds(r, 1)], y_hbm.at[pl.ds(tok, 1)], ssem.at[sl])

    def src_token(tile, n_tile, r):
        return rowsrc[jnp.where(r < n_tile, tile * tm + r, 0)]

    def tile_rows(tile):
        return jnp.where(tile < nt, nvalid_ref[jnp.minimum(tile, nt - 1)], 0)

    def issue_gather(tile, n_tile, sl, r, priority=0):
        gather_copy(src_token(tile, n_tile, r), r, sl).start(priority=priority)

    def issue_scatter(tile, n_tile, sl, r, priority=0):
        dst = jnp.where(r < n_tile, src_token(tile, n_tile, r), n_tok + sl * tm + r)
        scatter_copy(dst, r, sl).start(priority=priority)

    def wait_gather(sl):
        pltpu.make_async_copy(x_hbm.at[pl.ds(0, tm)], xbufs[sl], gsem.at[sl]).wait()

    def wait_scatter(sl):
        pltpu.make_async_copy(obufs[sl], y_hbm.at[pl.ds(0, tm)], ssem.at[sl]).wait()

    @pl.when(t == 0)
    def _():
        def build(i, carry):
            rowsrc[pos_ref[i]] = i
            return carry
        lax.fori_loop(0, n_tok, build, 0, unroll=16)

        def body(r, carry):
            issue_gather(0, tile_rows(0), 0, r)
            issue_gather(1, tile_rows(1), 1, r)
            return carry
        lax.fori_loop(0, tm, body, 0, unroll=8)
        for sl in range(3):
            obufs[sl][...] = jnp.zeros(obufs[sl].shape, obufs[sl].dtype)
        for sl in range(2):
            pltpu.make_async_copy(obufs[sl], y_hbm.at[pl.ds(n_tok + sl * tm, tm)], ssem.at[sl]).start()

    @pl.when(chga_ref[t] == 1)
    def _():
        ga_scr[...] = wga_ref[0, 0].astype(MM)
        ua_scr[...] = wua_ref[0, 0].astype(MM)
        da_scr[...] = wda_ref[0, 0].astype(MM)

    @pl.when(chgb_ref[t] == 1)
    def _():
        gb_scr[...] = wgb_ref[0, 0].astype(MM)
        ub_scr[...] = wub_ref[0, 0].astype(MM)
        db_scr[...] = wdb_ref[0, 0].astype(MM)

    prev = jnp.maximum(t - 1, 0)
    n_prev = jnp.where(t > 0, nvalid_ref[prev], 0)
    n_cur = nvalid_ref[t]

    n_ahead = tile_rows(t + 2)

    def tile_block(sl):
        other = (sl + 2) % 3
        wait_gather(sl)
        wait_scatter(sl)
        x1 = xbufs[sl][:, 0:D_MODEL]
        x = x1.astype(MM)

        def expert(wg, wu, wd, wrow):
            gate = jnp.dot(x, wg[...], preferred_element_type=F32)
            up = jnp.dot(x, wu[...], preferred_element_type=F32)
            hid = (_silu(gate) * up).astype(MM)
            return jnp.dot(hid, wd[...], preferred_element_type=F32) * wrow

        f = (expert(ga_scr, ua_scr, da_scr, xbufs[sl][:, D_MODEL:D_MODEL + 1])
             + expert(gb_scr, ub_scr, db_scr, xbufs[sl][:, D_MODEL + 1:D_MODEL + 2]))
        obufs[sl][...] = _layer_norm(ALPHA * x1 + f, g_ref[...], b_ref[...])
        for r in range(tm):
            issue_gather(t + 2, n_ahead, other, r, priority=r % 2)
            issue_scatter(prev, n_prev, other, r, priority=r % 2)

    def drain(sl):
        other = (sl + 2) % 3
        wait_gather(sl)
        wait_gather((sl + 1) % 3)

        def body(r, carry):
            issue_scatter(prev, n_prev, other, r)
            return carry
        lax.fori_loop(0, tm, body, 0, unroll=8)
        for k in range(3):
            wait_scatter(k)

    for sl in range(3):
        pl.when(jnp.logical_and(n_cur > 0, t % 3 == sl))(functools.partial(tile_block, sl))
        pl.when(jnp.logical_and(jnp.logical_and(n_cur == 0, n_prev > 0), t % 3 == sl))(
            functools.partial(drain, sl))


def _moe(x1e, key, cnt, ln_g, ln_b, w_gate, w_up, w_down, layer, tm):
    n = x1e.shape[0]
    n_tiles = n // tm + N_CLASSES
    n_rows = n_tiles * tm
    counts = cnt[:N_CLASSES, 0]
    tiles_c = (counts + tm - 1) // tm
    tile_end_c = jnp.cumsum(tiles_c)
    tile_start_c = tile_end_c - tiles_c
    total = tile_end_c[-1]
    tid = jnp.arange(n_tiles, dtype=I32)
    t_eff = jnp.minimum(tid, total - 1)
    tile_cls = jnp.minimum(jnp.sum((tile_end_c[None, :] <= t_eff[:, None]).astype(I32), axis=1),
                           N_CLASSES - 1)
    sel = (tile_cls[:, None] == jnp.arange(N_CLASSES, dtype=I32)[None, :]).astype(I32)
    left = jnp.sum(sel * counts[None, :], axis=1) - (tid - jnp.sum(sel * tile_start_c[None, :], axis=1)) * tm
    nvalid = jnp.where(tid < total, jnp.clip(left, 0, tm), 0).astype(I32)
    pair_id = tile_cls % len(PAIRS)
    pair_lo = sum(jnp.where(pair_id == k, p[0], 0) for k, p in enumerate(PAIRS))
    pair_hi = sum(jnp.where(pair_id == k, p[1], 0) for k, p in enumerate(PAIRS))
    ea = (EPG * (tile_cls // len(PAIRS)) + pair_lo).astype(I32)
    eb = (EPG * (tile_cls // len(PAIRS)) + pair_hi).astype(I32)
    first = tid == 0
    chga = jnp.logical_or(first, ea != jnp.roll(ea, 1)).astype(I32)
    chgb = jnp.logical_or(first, eb != jnp.roll(eb, 1)).astype(I32)
    key = key.reshape(n)
    tok_cls = key // KEY_CLS
    tok_sel = (tok_cls[:, None] == jnp.arange(N_CLASSES, dtype=I32)[None, :]).astype(I32)
    pos = (jnp.sum(tok_sel * tile_start_c[None, :], axis=1) * tm + key % KEY_CLS).astype(I32)

    kern = functools.partial(_moe_kernel, tm=tm, n_tok=n)
    wspec_a = lambda shape: pl.BlockSpec(
        shape, lambda t, nv, ca, cb, ea_, eb_, ps: (layer, ea_[t], 0, 0))
    wspec_b = lambda shape: pl.BlockSpec(
        shape, lambda t, nv, ca, cb, ea_, eb_, ps: (layer, eb_[t], 0, 0))
    vec = pl.BlockSpec((1, D_MODEL), lambda t, nv, ca, cb, ea_, eb_, ps: (0, 0))
    gu = (1, 1, D_MODEL, D_EXPERT)
    dn = (1, 1, D_EXPERT, D_MODEL)
    grid_spec = pltpu.PrefetchScalarGridSpec(
        num_scalar_prefetch=6,
        grid=(n_tiles,),
        in_specs=[pl.BlockSpec(memory_space=pl.ANY), vec, vec,
                  wspec_a(gu), wspec_a(gu), wspec_a(dn),
                  wspec_b(gu), wspec_b(gu), wspec_b(dn)],
        out_specs=pl.BlockSpec(memory_space=pl.ANY),
        scratch_shapes=[
            pltpu.VMEM((tm, X1E_W), F32), pltpu.VMEM((tm, X1E_W), F32), pltpu.VMEM((tm, X1E_W), F32),
            pltpu.VMEM((tm, D_MODEL), F32), pltpu.VMEM((tm, D_MODEL), F32),
            pltpu.VMEM((tm, D_MODEL), F32),
            pltpu.VMEM((D_MODEL, D_EXPERT), MM), pltpu.VMEM((D_MODEL, D_EXPERT), MM),
            pltpu.VMEM((D_EXPERT, D_MODEL), MM),
            pltpu.VMEM((D_MODEL, D_EXPERT), MM), pltpu.VMEM((D_MODEL, D_EXPERT), MM),
            pltpu.VMEM((D_EXPERT, D_MODEL), MM),
            pltpu.SMEM((n_rows,), I32),
            pltpu.SemaphoreType.DMA((3,)),
            pltpu.SemaphoreType.DMA((3,)),
        ],
    )
    return pl.pallas_call(
        kern,
        grid_spec=grid_spec,
        out_shape=jax.ShapeDtypeStruct((n + 3 * tm, D_MODEL), F32),
        compiler_params=_cparams(("arbitrary",)),
        name="moe",
    )(nvalid, chga, chgb, ea, eb, pos,
      x1e, ln_g, ln_b, w_gate, w_up, w_down, w_gate, w_up, w_down)


def _pad_hist(hist, rows):
    b, r, c = hist.shape
    return jnp.concatenate([jnp.zeros((b, rows - r, c), hist.dtype), hist], axis=1)


def _new_hist(hist, cur, c0, c1, keep):
    t = cur.shape[1]
    if t >= keep:
        return cur[:, t - keep:, c0:c1]
    return jnp.concatenate([hist[:, t:], cur[:, :, c0:c1]], axis=1)


def kernel(x_prompt, x_sample, state_delta, state_delta_conv, state_pool, state_lru, state_lru_conv,
           ab_w_in, a_conv_w, a_log_decay, a_dt_bias, a_norm_w, b_w_group, b_scale, ab_w_out,
           c_w_in, c_conv_w, c_conv_b, c_w_r, c_b_r, c_w_i, c_b_i, c_lambda, c_w_out,
           ln_mix_g, ln_mix_b, ln_ffn_g, ln_ffn_b, router_w, router_bias,
           moe_w_gate, moe_w_up, moe_w_down):
    bp, tp, d = x_prompt.shape
    bs, ts, _ = x_sample.shape
    n_p, n_s = bp * tp, bs * ts
    n = n_p + n_s
    tm = TOKEN_TILE
    tmix = MIX_TILE
    assert n_p % tmix == 0 and n_s % tmix == 0 and tmix % tm == 0

    xin = [(x_prompt.reshape(n_p, d), 0), (x_sample.reshape(n_s, d), 0)]
    groups = (
        dict(batch=bp, seq=tp, hist_valid=0, fresh=True,
             ev=dict(nb_blk=1, rows=min(tp, 256)), od=dict(nb_blk=min(bp, 8), rows=min(tp, 64))),
        dict(batch=bs, seq=ts, hist_valid=B_HIST, fresh=False,
             ev=dict(nb_blk=min(bs, 8), rows=ts), od=dict(nb_blk=min(bs, 32), rows=ts)),
    )
    rwt = router_w.T
    rb = router_bias.reshape(N_EXPERTS, 1)
    row = lambda v: v.reshape(1, -1)

    new = {k: ([], []) for k in ("dconv", "pool", "lru", "lconv")}
    delta_new = [None, None]
    for layer in range(DEPTH):
        j = layer // 2
        mixed = []
        if layer % 2 == 0:
            w = ab_w_in[j]
            c1 = EV_Z0 + A_V
            w_perm = jnp.concatenate(
                [w[:, :c1], w[:, c1 + 2 * A_HEADS:], w[:, c1:c1 + 2 * A_HEADS],
                 jnp.zeros((d, LANES - 2 * A_HEADS), w.dtype)], axis=1).astype(MM)
            gvec = jnp.zeros((2, LANES), F32)
            gvec = gvec.at[0, A_HEADS:2 * A_HEADS].set(a_log_decay[j])
            gvec = gvec.at[1, A_HEADS:2 * A_HEADS].set(a_dt_bias[j])
            for gi, g in enumerate(groups):
                b_, t_ = g["batch"], g["seq"]
                proj = _proj(xin[gi][0], w_perm, xin[gi][1], b_ * t_, tm)
                p3 = proj.reshape(b_, t_, EV_W)
                if g["fresh"]:
                    dconv = jnp.zeros((b_, 3, A_CONV_CH), F32)
                    delta_all = jnp.zeros((1, b_, A_HEADS, A_DK, A_DV), F32)
                    pool = jnp.zeros((b_, B_HIST, B_WIDTH), F32)
                else:
                    dconv, delta_all, pool = state_delta_conv[j], state_delta, state_pool[j]
                o, delta_new[gi] = _even_mixer(
                    proj, _pad_hist(dconv, SUBLANES), delta_all, j, _pad_hist(pool, 2 * SUBLANES),
                    a_conv_w[j], gvec, row(a_norm_w[j]), b_w_group[j].astype(MM), row(b_scale[j]),
                    delta_new[gi], batch=b_, seq=t_, hist_valid=g["hist_valid"], **g["ev"])
                mixed.append(o)
                new["dconv"][gi].append(_new_hist(dconv, p3, 0, A_CONV_CH, 3))
                new["pool"][gi].append(_new_hist(pool, p3, EV_U0, EV_U0 + B_WIDTH, B_HIST))
            w_out = ab_w_out[j].astype(MM)
        else:
            w_mm = c_w_in[j].astype(MM)
            for gi, g in enumerate(groups):
                b_, t_ = g["batch"], g["seq"]
                p3 = _proj(xin[gi][0], w_mm, xin[gi][1], b_ * t_, tm).reshape(b_, t_, 2 * C_WIDTH)
                if g["fresh"]:
                    lconv = jnp.zeros((b_, 3, C_WIDTH), F32)
                    lru = jnp.zeros((b_, C_WIDTH), F32)
                else:
                    lconv, lru = state_lru_conv[j], state_lru[j]
                o3, h_last = _odd_mixer(
                    p3, _pad_hist(lconv, SUBLANES), lru, c_conv_w[j], row(c_conv_b[j]),
                    c_w_r[j].astype(MM), row(c_b_r[j]), c_w_i[j].astype(MM), row(c_b_i[j]),
                    row(c_lambda[j]), **g["od"])
                mixed.append(o3.reshape(b_ * t_, C_WIDTH))
                new["lru"][gi].append(h_last)
                new["lconv"][gi].append(_new_hist(lconv, p3, C_WIDTH, 2 * C_WIDTH, 3))
            w_out = c_w_out[j].astype(MM)

        x1e, key, cnt = _mix_out(mixed[0], mixed[1], w_out, xin[0][0], xin[0][1], xin[1][0], xin[1][1],
                                 row(ln_mix_g[layer]), row(ln_mix_b[layer]), rwt, rb, tmix)
        x = _moe(x1e, key, cnt, row(ln_ffn_g[layer]), row(ln_ffn_b[layer]),
                 moe_w_gate, moe_w_up, moe_w_down, layer, tm)
        xin = [(x, 0), (x, n_p)]

    stack = lambda key, gi: jnp.stack(new[key][gi])
    return (x[:n_p].reshape(bp, tp, d), x[n_p:n].reshape(bs, ts, d),
            delta_new[0], stack("dconv", 0), stack("pool", 0), stack("lru", 0), stack("lconv", 0),
            delta_new[1], stack("dconv", 1), stack("pool", 1), stack("lru", 1), stack("lconv", 1))
```

```python
import functools

import jax
import jax.numpy as jnp
from jax import lax
from jax.experimental import pallas as pl
from jax.experimental.pallas import tpu as pltpu

F32 = jnp.float32
I32 = jnp.int32
MM = jnp.bfloat16

D_MODEL = 1024
DEPTH = 4
N_A_LAYERS = (DEPTH + 1) // 2
A_HEADS = 4
A_DK = 128
A_DV = 128
A_QK = A_HEADS * A_DK
A_V = A_HEADS * A_DV
A_CONV_CH = 2 * A_QK + A_V
A_CHUNK = 64
B_GROUPS = 4
B_GW = 128
B_WINDOWS = (2, 4, 8, 16)
B_HIST = 15
B_WIDTH = B_GROUPS * B_GW
C_WIDTH = D_MODEL
C_HEADS = 4
C_HW = C_WIDTH // C_HEADS
C_GATE = 8.0
N_EXPERTS = 16
N_GROUPS = 4
EPG = N_EXPERTS // N_GROUPS
D_EXPERT = 512
ALPHA = (2 * DEPTH) ** 0.25
LN_EPS = 1e-5
RMS_EPS = 1e-6

LANES = 128
SUBLANES = 8
VMEM_LIMIT = 56 * 1024 * 1024

EV_Z0 = A_CONV_CH
EV_U0 = EV_Z0 + A_V
EV_BA0 = EV_U0 + B_WIDTH
EV_W = EV_BA0 + LANES

PAIRS = ((0, 1), (0, 2), (0, 3), (1, 2), (1, 3), (2, 3))
N_CLASSES = N_GROUPS * len(PAIRS)

TOKEN_TILE = 256
MIX_TILE = 512
CLS_ROWS = 32
KEY_CLS = 1 << 16
X1E_W = D_MODEL + LANES


def _cparams(sem):
    return pltpu.CompilerParams(dimension_semantics=sem, vmem_limit_bytes=VMEM_LIMIT)


def _dot(a, b):
    return jnp.dot(a.astype(MM), b.astype(MM), preferred_element_type=F32)


def _dot_nt(a, b):
    return lax.dot_general(a.astype(MM), b.astype(MM), (((1,), (1,)), ((), ())),
                           preferred_element_type=F32)


def _dot_tn(a, b):
    return lax.dot_general(a.astype(MM), b.astype(MM), (((0,), (0,)), ((), ())),
                           preferred_element_type=F32)


def _split3(v):
    hi = v.astype(MM)
    r1 = v - hi.astype(F32)
    mid = r1.astype(MM)
    lo = (r1 - mid.astype(F32)).astype(MM)
    return hi, mid, lo


def _sigmoid(x):
    return 1.0 / (1.0 + jnp.exp(-x))


def _sigmoid_t(x):
    return 0.5 * jnp.tanh(0.5 * x) + 0.5


def _silu(x):
    return x * _sigmoid_t(x)


def _softplus(x):
    return jnp.maximum(x, 0.0) + jnp.log1p(jnp.exp(-jnp.abs(x)))


def _gelu_tanh(x):
    return x * (0.5 * (1.0 + jnp.tanh(0.7978845608028654 * (x + 0.044715 * (x * x * x)))))


def _layer_norm(v, g, b):
    mu = jnp.mean(v, axis=-1, keepdims=True)
    d = v - mu
    var = jnp.mean(d * d, axis=-1, keepdims=True)
    return d * lax.rsqrt(var + LN_EPS) * g + b


def _proj_kernel(x_ref, w_ref, o_ref):
    o_ref[...] = jnp.dot(x_ref[...].astype(MM), w_ref[...], preferred_element_type=F32)


def _proj(x, w_mm, row0, nrows, tm):
    k, width = w_mm.shape
    off = row0 // tm
    return pl.pallas_call(
        _proj_kernel,
        grid=(nrows // tm,),
        in_specs=[pl.BlockSpec((tm, k), lambda i: (i + off, 0)),
                  pl.BlockSpec((k, width), lambda i: (0, 0))],
        out_specs=pl.BlockSpec((tm, width), lambda i: (i, 0)),
        out_shape=jax.ShapeDtypeStruct((nrows, width), F32),
        compiler_params=_cparams(("parallel",)),
        name="proj",
    )(x, w_mm)


def _even_kernel(*refs, nb_blk, rows, chunk, hist_valid, carry_hist, has_prev):
    (p_ref, convh_ref, s0_ref, poolh_ref, convw_ref, gvec_ref, normw_ref,
     wgrp_ref, scale_ref) = refs[:9]
    o_ref, snew_ref, ext_scr, qkv_scr, pext_scr, s_scr = refs[9 + (1 if has_prev else 0):]
    i = pl.program_id(1)
    n_chunks = rows // chunk
    n_neumann = max((chunk - 1).bit_length() - 1, 0)
    hb = 2 * SUBLANES

    @pl.when(i == 0)
    def _():
        s_scr[...] = s0_ref[0]
        ext_scr[:, 0:SUBLANES, :] = convh_ref[...]
        pext_scr[:, 0:hb, :] = poolh_ref[...]

    rid = lax.broadcasted_iota(I32, (chunk, chunk), 0)
    cid = lax.broadcasted_iota(I32, (chunk, chunk), 1)
    incl = rid >= cid
    strict = rid > cid
    eye = (rid == cid).astype(F32)
    ltri = incl.astype(F32)
    lane = lax.broadcasted_iota(I32, (chunk, LANES), 1)
    trow = lax.broadcasted_iota(I32, (rows, LANES), 0)
    pos = (i * rows + trow + (1 + hist_valid)).astype(F32)
    neg_decay_rate = -jnp.exp(gvec_ref[0:1, :])
    dt_bias = gvec_ref[1:2, :]

    for nb in range(nb_blk):
        r0 = nb * rows
        ext_scr[nb, SUBLANES:SUBLANES + rows, :] = p_ref[r0:r0 + rows, 0:A_CONV_CH]
        for ct in range(A_CONV_CH // LANES):
            cs = slice(ct * LANES, (ct + 1) * LANES)
            acc = None
            for j in range(4):
                lo = SUBLANES - 3 + j
                term = ext_scr[nb, lo:lo + rows, cs] * convw_ref[j:j + 1, cs]
                acc = term if acc is None else acc + term
            t = _silu(acc)
            if ct < 2 * A_HEADS:
                t = t * lax.rsqrt(jnp.sum(t * t, axis=-1, keepdims=True) + 1e-6)
                if ct < A_HEADS:
                    t = t * (A_DK ** -0.5)
            qkv_scr[r0:r0 + rows, cs] = t
        if carry_hist:
            ext_scr[nb, 0:SUBLANES, :] = ext_scr[nb, rows:rows + SUBLANES, :]

        pext_scr[nb, hb:hb + rows, :] = p_ref[r0:r0 + rows, EV_U0:EV_U0 + B_WIDTH]
        for gi, w in enumerate(B_WINDOWS):
            cs = slice(gi * B_GW, (gi + 1) * B_GW)
            cur = pext_scr[nb, hb:hb + rows, cs]
            tot = cur
            for j in range(1, w):
                tot = tot + pext_scr[nb, hb - j:hb - j + rows, cs]
            pooled = tot / jnp.minimum(pos, float(w)) - cur
            ob = _dot(pooled, wgrp_ref[gi]) * scale_ref[:, cs]
            o_ref[r0:r0 + rows, A_V + gi * B_GW:A_V + (gi + 1) * B_GW] = ob
        if carry_hist:
            pext_scr[nb, 0:hb, :] = pext_scr[nb, rows:rows + hb, :]

    chunks = [(nb, c) for nb in range(nb_blk) for c in range(n_chunks)]
    probs = [(nb, c, h) for nb, c in chunks for h in range(A_HEADS)]
    rs = {(nb, c): slice(nb * rows + c * chunk, nb * rows + (c + 1) * chunk) for nb, c in chunks}
    gcol = lambda arr, h: arr[:, A_HEADS + h:A_HEADS + h + 1]
    qf = lambda p: qkv_scr[rs[p[:2]], p[2] * LANES:(p[2] + 1) * LANES]
    kf = lambda p: qkv_scr[rs[p[:2]], A_QK + p[2] * LANES:A_QK + (p[2] + 1) * LANES]
    vf = lambda p: qkv_scr[rs[p[:2]], 2 * A_QK + p[2] * LANES:2 * A_QK + (p[2] + 1) * LANES]

    ba = {ck: p_ref[rs[ck], EV_BA0:EV_BA0 + LANES] for ck in chunks}
    beta_full = {ck: _sigmoid(ba[ck]) for ck in chunks}
    g_full = {ck: neg_decay_rate * _softplus(ba[ck] + dt_bias) for ck in chunks}
    g_parts = {ck: _split3(g_full[ck]) for ck in chunks}
    gc_full = {ck: sum(jnp.dot(ltri.astype(MM), part, preferred_element_type=F32)
                       for part in g_parts[ck]) for ck in chunks}
    eg_full = {ck: jnp.exp(gc_full[ck]) for ck in chunks}
    gl_full = {ck: gc_full[ck][chunk - 1:chunk, :] for ck in chunks}
    ekd_full = {ck: jnp.exp(gl_full[ck] - gc_full[ck]) for ck in chunks}
    egl_full = {ck: jnp.exp(gl_full[ck]) for ck in chunks}
    sel = [(lane == A_HEADS + h).astype(MM) for h in range(A_HEADS)]
    gc_parts = {ck: _split3(gc_full[ck]) for ck in chunks}
    grow = {p: sum(lax.dot_general(sel[p[2]], part, (((1,), (1,)), ((), ())),
                                   preferred_element_type=F32) for part in gc_parts[p[:2]])
            for p in probs}
    decay = {p: jnp.where(incl, jnp.exp(jnp.where(incl, gcol(gc_full[p[:2]], p[2]) - grow[p], 0.0)), 0.0)
             for p in probs}
    beta = {p: beta_full[p[:2]][:, p[2]:p[2] + 1] for p in probs}
    kk = {p: _dot_nt(kf(p) * beta[p], kf(p)) for p in probs}
    qk = {p: _dot_nt(qf(p), kf(p)) for p in probs}
    a_intra = {p: jnp.where(incl, qk[p] * decay[p], 0.0) for p in probs}
    power = {p: -jnp.where(strict, kk[p] * decay[p], 0.0) for p in probs}
    tinv = {p: eye + power[p] for p in probs}
    for _ in range(n_neumann):
        power = {p: _dot(power[p], power[p]) for p in probs}
        tinv = {p: tinv[p] + _dot(tinv[p], power[p]) for p in probs}
    u_in = {p: _dot(tinv[p], vf(p) * beta[p]) for p in probs}
    w_in = {p: _dot(tinv[p], kf(p) * beta[p] * gcol(eg_full[p[:2]], p[2])) for p in probs}

    for c in range(n_chunks):
        cp = [(nb, c, h) for nb in range(nb_blk) for h in range(A_HEADS)]
        s_old = {p: s_scr[p[0], p[2]] for p in cp}
        wq = {p: _dot(jnp.concatenate([w_in[p], qf(p) * gcol(eg_full[p[:2]], p[2])], axis=0), s_old[p])
              for p in cp}
        u_new = {p: u_in[p] - wq[p][0:chunk] for p in cp}
        au = {p: _dot(a_intra[p], u_new[p]) for p in cp}
        ku = {p: _dot_tn(kf(p) * gcol(ekd_full[p[:2]], p[2]), u_new[p]) for p in cp}
        for p in cp:
            nb, _, h = p
            s_scr[nb, h] = s_old[p] * gcol(egl_full[p[:2]], h) + ku[p]
            o = wq[p][chunk:2 * chunk] + au[p]
            o = o * lax.rsqrt(jnp.mean(o * o, axis=-1, keepdims=True) + RMS_EPS) * normw_ref[...]
            z = p_ref[rs[p[:2]], EV_Z0 + h * LANES:EV_Z0 + (h + 1) * LANES]
            o_ref[rs[p[:2]], h * LANES:(h + 1) * LANES] = o * _silu(z)

    @pl.when(i == pl.num_programs(1) - 1)
    def _():
        snew_ref[0] = s_scr[...]
        if not has_prev:
            for other in range(1, snew_ref.shape[0]):
                snew_ref[other] = jnp.zeros(s_scr.shape, F32)


def _even_mixer(proj, convh8, s0_all, layer_j, poolh16, convw, gvec, normw, wgrp_mm, scale, snew_prev,
                *, batch, seq, nb_blk, rows, hist_valid):
    chunk = min(A_CHUNK, seq)
    assert seq % rows == 0 and rows % chunk == 0 and batch % nb_blk == 0
    assert nb_blk == 1 or rows == seq
    t_blocks = seq // rows
    blk = nb_blk * rows
    hb = 2 * SUBLANES
    has_prev = snew_prev is not None
    slot = layer_j if s0_all.shape[0] > 1 else 0
    kern = functools.partial(_even_kernel, nb_blk=nb_blk, rows=rows, chunk=chunk,
                             hist_valid=hist_valid, carry_hist=t_blocks > 1, has_prev=has_prev)
    full = lambda shape: pl.BlockSpec(shape, lambda b, i: (0,) * len(shape))
    state_blk = (1, nb_blk, A_HEADS, A_DK, A_DV)
    in_specs = [
        pl.BlockSpec((blk, EV_W), lambda b, i: (b * t_blocks + i, 0)),
        pl.BlockSpec((nb_blk, SUBLANES, A_CONV_CH), lambda b, i: (b, 0, 0)),
        pl.BlockSpec(state_blk, lambda b, i: (slot, b, 0, 0, 0)),
        pl.BlockSpec((nb_blk, hb, B_WIDTH), lambda b, i: (b, 0, 0)),
        full((4, A_CONV_CH)), full((2, LANES)), full((1, A_DV)),
        full((B_GROUPS, B_GW, B_GW)), full((1, B_WIDTH)),
    ]
    args = [proj, convh8, s0_all, poolh16, convw, gvec, normw, wgrp_mm, scale]
    if has_prev:
        in_specs.append(pl.BlockSpec(memory_space=pl.ANY))
        args.append(snew_prev)
        snew_spec = pl.BlockSpec(state_blk, lambda b, i: (layer_j, b, 0, 0, 0))
    else:
        snew_spec = pl.BlockSpec((N_A_LAYERS,) + state_blk[1:], lambda b, i: (0, b, 0, 0, 0))
    return pl.pallas_call(
        kern,
        grid=(batch // nb_blk, t_blocks),
        in_specs=in_specs,
        out_specs=[
            pl.BlockSpec((blk, D_MODEL), lambda b, i: (b * t_blocks + i, 0)),
            snew_spec,
        ],
        out_shape=[jax.ShapeDtypeStruct((batch * seq, D_MODEL), F32),
                   jax.ShapeDtypeStruct((N_A_LAYERS, batch, A_HEADS, A_DK, A_DV), F32)],
        scratch_shapes=[
            pltpu.VMEM((nb_blk, SUBLANES + rows, A_CONV_CH), F32),
            pltpu.VMEM((blk, A_CONV_CH), F32),
            pltpu.VMEM((nb_blk, hb + rows, B_WIDTH), F32),
            pltpu.VMEM((nb_blk, A_HEADS, A_DK, A_DV), F32),
        ],
        input_output_aliases={len(args) - 1: 1} if has_prev else {},
        compiler_params=_cparams(("parallel", "arbitrary")),
        name="even_mixer",
    )(*args)


def _odd_kernel(p_ref, convh_ref, h0_ref, convw_ref, convb_ref, wr_ref, br_ref, wi_ref, bi_ref,
                lam_ref, o_ref, hlast_ref, ext_scr, a_scr, b_scr, hs_scr, h_scr,
                *, nb_blk, rows, carry_hist):
    i = pl.program_id(1)
    blk = nb_blk * rows
    pitch = a_scr.shape[1] // nb_blk

    @pl.when(i == 0)
    def _():
        h_scr[...] = h0_ref[...]
        ext_scr[:, 0:SUBLANES, :] = convh_ref[...]

    ext_scr[:, SUBLANES:SUBLANES + rows, :] = p_ref[:, :, C_WIDTH:2 * C_WIDTH]
    log_base = -C_GATE * _softplus(-lam_ref[...])
    for h in range(C_HEADS):
        cs = slice(h * C_HW, (h + 1) * C_HW)
        acc = None
        for j in range(4):
            lo = SUBLANES - 3 + j
            term = ext_scr[:, lo:lo + rows, cs] * convw_ref[j:j + 1, cs]
            acc = term if acc is None else acc + term
        xc = (acc + convb_ref[:, cs]).reshape(blk, C_HW)
        r = _sigmoid_t(_dot(xc, wr_ref[h]) + br_ref[:, cs])
        gi = _sigmoid_t(_dot(xc, wi_ref[h]) + bi_ref[:, cs])
        log_a = r * log_base[:, cs]
        a = jnp.exp(log_a)
        th = jnp.tanh(log_a)
        bt = jnp.sqrt(-2.0 * th) * lax.rsqrt(1.0 - th) * gi * xc
        for lt in range(C_HW // LANES):
            for b in range(nb_blk):
                dst = pl.ds(b * pitch, rows)
                src = slice(b * rows, (b + 1) * rows)
                a_scr[h * (C_HW // LANES) + lt, dst, :] = a[src, lt * LANES:(lt + 1) * LANES]
                b_scr[h * (C_HW // LANES) + lt, dst, :] = bt[src, lt * LANES:(lt + 1) * LANES]
    if carry_hist:
        ext_scr[:, 0:SUBLANES, :] = ext_scr[:, rows:rows + SUBLANES, :]

    n_lt = C_WIDTH // LANES

    def step(t, hcur):
        hnew = []
        for lt in range(n_lt):
            a_t = a_scr[lt, pl.ds(t, nb_blk, stride=pitch), :]
            b_t = b_scr[lt, pl.ds(t, nb_blk, stride=pitch), :]
            hn = a_t * hcur[lt] + b_t
            hs_scr[lt, pl.ds(t, nb_blk, stride=pitch), :] = hn
            hnew.append(hn)
        return tuple(hnew)

    h_init = tuple(h_scr[:, lt * LANES:(lt + 1) * LANES] for lt in range(n_lt))
    h_fin = lax.fori_loop(0, rows, step, h_init, unroll=SUBLANES)
    for lt in range(n_lt):
        cs = slice(lt * LANES, (lt + 1) * LANES)
        h_scr[:, cs] = h_fin[lt]
        for b in range(nb_blk):
            o_ref[b, :, cs] = _gelu_tanh(p_ref[b, :, cs]) * hs_scr[lt, pl.ds(b * pitch, rows), :]

    @pl.when(i == pl.num_programs(1) - 1)
    def _():
        hlast_ref[...] = h_scr[...]


def _odd_mixer(proj3, convh8, h0, convw, convb, wr_mm, br, wi_mm, bi, lam, *, nb_blk, rows):
    batch, seq, _ = proj3.shape
    assert seq % rows == 0 and batch % nb_blk == 0 and rows % SUBLANES == 0
    t_blocks = seq // rows
    pitch = rows + SUBLANES
    kern = functools.partial(_odd_kernel, nb_blk=nb_blk, rows=rows, carry_hist=t_blocks > 1)
    full = lambda shape: pl.BlockSpec(shape, lambda b, i: (0,) * len(shape))
    return pl.pallas_call(
        kern,
        grid=(batch // nb_blk, t_blocks),
        in_specs=[
            pl.BlockSpec((nb_blk, rows, 2 * C_WIDTH), lambda b, i: (b, i, 0)),
            pl.BlockSpec((nb_blk, SUBLANES, C_WIDTH), lambda b, i: (b, 0, 0)),
            pl.BlockSpec((nb_blk, C_WIDTH), lambda b, i: (b, 0)),
            full((4, C_WIDTH)), full((1, C_WIDTH)),
            full((C_HEADS, C_HW, C_HW)), full((1, C_WIDTH)),
            full((C_HEADS, C_HW, C_HW)), full((1, C_WIDTH)),
            full((1, C_WIDTH)),
        ],
        out_specs=[
            pl.BlockSpec((nb_blk, rows, C_WIDTH), lambda b, i: (b, i, 0)),
            pl.BlockSpec((nb_blk, C_WIDTH), lambda b, i: (b, 0)),
        ],
        out_shape=[jax.ShapeDtypeStruct((batch, seq, C_WIDTH), F32),
                   jax.ShapeDtypeStruct((batch, C_WIDTH), F32)],
        scratch_shapes=[
            pltpu.VMEM((nb_blk, SUBLANES + rows, C_WIDTH), F32),
            pltpu.VMEM((C_WIDTH // LANES, nb_blk * pitch, LANES), F32),
            pltpu.VMEM((C_WIDTH // LANES, nb_blk * pitch, LANES), F32),
            pltpu.VMEM((C_WIDTH // LANES, nb_blk * pitch, LANES), F32),
            pltpu.VMEM((nb_blk, C_WIDTH), F32),
        ],
        compiler_params=_cparams(("parallel", "arbitrary")),
        name="odd_mixer",
    )(proj3, convh8, h0, convw, convb, wr_mm, br, wi_mm, bi, lam)


def _route(x1, rwt_ref, rb_ref):
    def split(v):
        hi = v.astype(MM)
        return hi, (v - hi.astype(F32)).astype(MM)

    nt = lambda a, b: lax.dot_general(a, b, (((1,), (1,)), ((), ())), preferred_element_type=F32)
    r_hi, r_lo = split(rwt_ref[...])
    x_hi, x_lo = split(x1)
    logits = nt(r_hi, x_hi) + nt(r_hi, x_lo) + nt(r_lo, x_hi)
    sc = _sigmoid(logits)
    bz = sc + rb_ref[...]
    row = lambda arr, e: arr[e:e + 1, :]
    best = None
    gidx = None
    for g in range(N_GROUPS):
        r = [row(bz, EPG * g + k) for k in range(EPG)]
        top2 = None
        for a, b in PAIRS:
            s = r[a] + r[b]
            top2 = s if top2 is None else jnp.maximum(top2, s)
        if best is None:
            best, gidx = top2, jnp.zeros(top2.shape, I32)
        else:
            upd = top2 > best
            gidx = jnp.where(upd, g, gidx)
            best = jnp.where(upd, top2, best)
    sb, ss = [], []
    for k in range(EPG):
        vb, vs = row(bz, k), row(sc, k)
        for g in range(1, N_GROUPS):
            vb = jnp.where(gidx == g, row(bz, EPG * g + k), vb)
            vs = jnp.where(gidx == g, row(sc, EPG * g + k), vs)
        sb.append(vb)
        ss.append(vs)
    m1, i1 = sb[0], jnp.zeros(sb[0].shape, I32)
    for k in range(1, EPG):
        upd = sb[k] > m1
        i1 = jnp.where(upd, k, i1)
        m1 = jnp.where(upd, sb[k], m1)
    m2, i2 = None, None
    for k in range(EPG):
        cand = jnp.where(i1 == k, -jnp.inf, sb[k])
        if m2 is None:
            m2, i2 = cand, jnp.zeros(cand.shape, I32)
        else:
            upd = cand > m2
            i2 = jnp.where(upd, k, i2)
            m2 = jnp.where(upd, cand, m2)
    lo = jnp.minimum(i1, i2)
    hi = jnp.maximum(i1, i2)
    pair = jnp.where(lo == 0, 0, jnp.where(lo == 1, 3, 5)) + hi - lo - 1
    s_lo, s_hi = ss[0], ss[0]
    for k in range(1, EPG):
        s_lo = jnp.where(lo == k, ss[k], s_lo)
        s_hi = jnp.where(hi == k, ss[k], s_hi)
    den = s_lo + s_hi
    return gidx * len(PAIRS) + pair, s_lo / den, s_hi / den


def _mix_out_kernel(ap_ref, as_ref, w_ref, xp_ref, xs_ref, g_ref, b_ref, rwt_ref, rb_ref, upper_ref,
                    x1e_ref, key_ref, cnt_ref, run_scr, *, np_tiles):
    i = pl.program_id(0)
    tm = xp_ref.shape[0]

    @pl.when(i == 0)
    def _():
        run_scr[...] = jnp.zeros(run_scr.shape, F32)

    def finish(a_ref, x_ref):
        hmix = jnp.dot(a_ref[...].astype(MM), w_ref[...], preferred_element_type=F32)
        x1 = _layer_norm(ALPHA * x_ref[...] + hmix, g_ref[...], b_ref[...])
        cls, wa, wb = _route(x1, rwt_ref, rb_ref)
        crow = lax.broadcasted_iota(I32, (CLS_ROWS, tm), 0)
        onehot = (crow == cls).astype(F32)
        prefix = jnp.dot(onehot.astype(MM), upper_ref[...], preferred_element_type=F32)
        run = run_scr[:, 0:1]
        rank = jnp.sum(onehot * (prefix + (run - 1.0)), axis=0, keepdims=True)
        run_new = run + jnp.sum(onehot, axis=1, keepdims=True)
        run_scr[...] = jnp.broadcast_to(run_new, run_scr.shape)
        cnt_ref[...] = jnp.broadcast_to(run_new, cnt_ref.shape).astype(I32)
        key_ref[...] = cls * KEY_CLS + rank.astype(I32)
        wrow = lax.broadcasted_iota(I32, (LANES, tm), 0)
        wpad = jnp.where(wrow == 0, wa, jnp.where(wrow == 1, wb, 0.0))
        x1e_ref[:, 0:D_MODEL] = x1
        x1e_ref[:, D_MODEL:D_MODEL + LANES] = wpad.T

    @pl.when(i < np_tiles)
    def _():
        finish(ap_ref, xp_ref)

    @pl.when(i >= np_tiles)
    def _():
        finish(as_ref, xs_ref)


def _mix_out(a_p, a_s, w_mm, x_p, row0_p, x_s, row0_s, g, b, rwt, rb, tm):
    n = a_p.shape[0] + a_s.shape[0]
    np_tiles = a_p.shape[0] // tm
    ns_tiles = a_s.shape[0] // tm
    assert (np_tiles + ns_tiles) * tm == n and row0_p % tm == 0 and row0_s % tm == 0
    off_p, off_s = row0_p // tm, row0_s // tm
    kern = functools.partial(_mix_out_kernel, np_tiles=np_tiles)
    full = lambda shape: pl.BlockSpec(shape, lambda i: (0,) * len(shape))
    upper = (jnp.arange(tm)[:, None] <= jnp.arange(tm)[None, :]).astype(MM)
    return pl.pallas_call(
        kern,
        grid=(np_tiles + ns_tiles,),
        in_specs=[
            pl.BlockSpec((tm, D_MODEL), lambda i: (jnp.minimum(i, np_tiles - 1), 0)),
            pl.BlockSpec((tm, D_MODEL), lambda i: (jnp.maximum(i - np_tiles, 0), 0)),
            full((D_MODEL, D_MODEL)),
            pl.BlockSpec((tm, D_MODEL), lambda i: (jnp.minimum(i, np_tiles - 1) + off_p, 0)),
            pl.BlockSpec((tm, D_MODEL), lambda i: (jnp.maximum(i - np_tiles, 0) + off_s, 0)),
            full((1, D_MODEL)), full((1, D_MODEL)),
            full((N_EXPERTS, D_MODEL)), full((N_EXPERTS, 1)),
            full((tm, tm)),
        ],
        out_specs=[
            pl.BlockSpec((tm, X1E_W), lambda i: (i, 0)),
            pl.BlockSpec((1, tm), lambda i: (0, i)),
            full((CLS_ROWS, LANES)),
        ],
        out_shape=[jax.ShapeDtypeStruct((n, X1E_W), F32),
                   jax.ShapeDtypeStruct((1, n), I32),
                   jax.ShapeDtypeStruct((CLS_ROWS, LANES), I32)],
        scratch_shapes=[pltpu.VMEM((CLS_ROWS, LANES), F32)],
        compiler_params=_cparams(("arbitrary",)),
        name="mix_out",
    )(a_p, a_s, w_mm, x_p, x_s, g, b, rwt, rb, upper)


def _moe_kernel(nvalid_ref, chga_ref, chgb_ref, ea_ref, eb_ref, pos_ref,
                x_hbm, g_ref, b_ref, wga_ref, wua_ref, wda_ref, wgb_ref, wub_ref, wdb_ref,
                y_hbm, xbuf0, xbuf1, xbuf2, obuf0, obuf1, obuf2,
                ga_scr, ua_scr, da_scr, gb_scr, ub_scr, db_scr, rowsrc, gsem, ssem, *, tm, n_tok):
    t = pl.program_id(0)
    nt = pl.num_programs(0)
    xbufs, obufs = (xbuf0, xbuf1, xbuf2), (obuf0, obuf1, obuf2)

    def gather_copy(tok, r, sl):
        return pltpu.make_async_copy(x_hbm.at[pl.ds(tok, 1)], xbufs[sl].at[pl.ds(r, 1)], gsem.at[sl])

    def scatter_copy(tok, r, sl):
        return pltpu.make_async_copy(obufs[sl].at[pl.ds(r, 1)], y_hbm.at[pl.ds(tok, 1)], ssem.at[sl])

    def src_token(tile, n_tile, r):
        return rowsrc[jnp.where(r < n_tile, tile * tm + r, 0)]

    def tile_rows(tile):
        return jnp.where(tile < nt, nvalid_ref[jnp.minimum(tile, nt - 1)], 0)

    def issue_gather(tile, n_tile, sl, r, priority=0):
        gather_copy(src_token(tile, n_tile, r), r, sl).start(priority=priority)

    def issue_scatter(tile, n_tile, sl, r, priority=0):
        dst = jnp.where(r < n_tile, src_token(tile, n_tile, r), n_tok + sl * tm + r)
        scatter_copy(dst, r, sl).start(priority=priority)

    def wait_gather(sl):
        pltpu.make_async_copy(x_hbm.at[pl.ds(0, tm)], xbufs[sl], gsem.at[sl]).wait()

    def wait_scatter(sl):
        pltpu.make_async_copy(obufs[sl], y_hbm.at[pl.ds(0, tm)], ssem.at[sl]).wait()

    @pl.when(t == 0)
    def _():
        def build(i, carry):
            rowsrc[pos_ref[i]] = i
            return carry
        lax.fori_loop(0, n_tok, build, 0, unroll=16)

        def body(r, carry):
            issue_gather(0, tile_rows(0), 0, r)
            issue_gather(1, tile_rows(1), 1, r)
            return carry
        lax.fori_loop(0, tm, body, 0, unroll=8)
        for sl in range(3):
            obufs[sl][...] = jnp.zeros(obufs[sl].shape, obufs[sl].dtype)
        for sl in range(2):
            pltpu.make_async_copy(obufs[sl], y_hbm.at[pl.ds(n_tok + sl * tm, tm)], ssem.at[sl]).start()

    @pl.when(chga_ref[t] == 1)
    def _():
        ga_scr[...] = wga_ref[0, 0].astype(MM)
        ua_scr[...] = wua_ref[0, 0].astype(MM)
        da_scr[...] = wda_ref[0, 0].astype(MM)

    @pl.when(chgb_ref[t] == 1)
    def _():
        gb_scr[...] = wgb_ref[0, 0].astype(MM)
        ub_scr[...] = wub_ref[0, 0].astype(MM)
        db_scr[...] = wdb_ref[0, 0].astype(MM)

    prev = jnp.maximum(t - 1, 0)
    n_prev = jnp.where(t > 0, nvalid_ref[prev], 0)
    n_cur = nvalid_ref[t]

    n_ahead = tile_rows(t + 2)

    def tile_block(sl):
        other = (sl + 2) % 3
        wait_gather(sl)
        wait_scatter(sl)
        x1 = xbufs[sl][:, 0:D_MODEL]
        x = x1.astype(MM)

        def expert(wg, wu, wd, wrow):
            gate = jnp.dot(x, wg[...], preferred_element_type=F32)
            up = jnp.dot(x, wu[...], preferred_element_type=F32)
            hid = (_silu(gate) * up).astype(MM)
            return jnp.dot(hid, wd[...], preferred_element_type=F32) * wrow

        f = (expert(ga_scr, ua_scr, da_scr, xbufs[sl][:, D_MODEL:D_MODEL + 1])
             + expert(gb_scr, ub_scr, db_scr, xbufs[sl][:, D_MODEL + 1:D_MODEL + 2]))
        obufs[sl][...] = _layer_norm(ALPHA * x1 + f, g_ref[...], b_ref[...])
        for r in range(tm):
            issue_gather(t + 2, n_ahead, other, r, priority=r % 2)
            issue_scatter(prev, n_prev, other, r, priority=r % 2)

    def drain(sl):
        other = (sl + 2) % 3
        wait_gather(sl)
        wait_gather((sl + 1) % 3)

        def body(r, carry):
            issue_scatter(prev, n_prev, other, r)
            return carry
        lax.fori_loop(0, tm, body, 0, unroll=8)
        for k in range(3):
            wait_scatter(k)

    for sl in range(3):
        pl.when(jnp.logical_and(n_cur > 0, t % 3 == sl))(functools.partial(tile_block, sl))
        pl.when(jnp.logical_and(jnp.logical_and(n_cur == 0, n_prev > 0), t % 3 == sl))(
            functools.partial(drain, sl))


def _moe(x1e, key, cnt, ln_g, ln_b, w_gate, w_up, w_down, layer, tm):
    n = x1e.shape[0]
    n_tiles = n // tm + N_CLASSES
    n_rows = n_tiles * tm
    counts = cnt[:N_CLASSES, 0]
    tiles_c = (counts + tm - 1) // tm
    tile_end_c = jnp.cumsum(tiles_c)
    tile_start_c = tile_end_c - tiles_c
    total = tile_end_c[-1]
    tid = jnp.arange(n_tiles, dtype=I32)
    t_eff = jnp.minimum(tid, total - 1)
    tile_cls = jnp.minimum(jnp.sum((tile_end_c[None, :] <= t_eff[:, None]).astype(I32), axis=1),
                           N_CLASSES - 1)
    sel = (tile_cls[:, None] == jnp.arange(N_CLASSES, dtype=I32)[None, :]).astype(I32)
    left = jnp.sum(sel * counts[None, :], axis=1) - (tid - jnp.sum(sel * tile_start_c[None, :], axis=1)) * tm
    nvalid = jnp.where(tid < total, jnp.clip(left, 0, tm), 0).astype(I32)
    pair_id = tile_cls % len(PAIRS)
    pair_lo = sum(jnp.where(pair_id == k, p[0], 0) for k, p in enumerate(PAIRS))
    pair_hi = sum(jnp.where(pair_id == k, p[1], 0) for k, p in enumerate(PAIRS))
    ea = (EPG * (tile_cls // len(PAIRS)) + pair_lo).astype(I32)
    eb = (EPG * (tile_cls // len(PAIRS)) + pair_hi).astype(I32)
    first = tid == 0
    chga = jnp.logical_or(first, ea != jnp.roll(ea, 1)).astype(I32)
    chgb = jnp.logical_or(first, eb != jnp.roll(eb, 1)).astype(I32)
    key = key.reshape(n)
    tok_cls = key // KEY_CLS
    tok_sel = (tok_cls[:, None] == jnp.arange(N_CLASSES, dtype=I32)[None, :]).astype(I32)
    pos = (jnp.sum(tok_sel * tile_start_c[None, :], axis=1) * tm + key % KEY_CLS).astype(I32)

    kern = functools.partial(_moe_kernel, tm=tm, n_tok=n)
    wspec_a = lambda shape: pl.BlockSpec(
        shape, lambda t, nv, ca, cb, ea_, eb_, ps: (layer, ea_[t], 0, 0))
    wspec_b = lambda shape: pl.BlockSpec(
        shape, lambda t, nv, ca, cb, ea_, eb_, ps: (layer, eb_[t], 0, 0))
    vec = pl.BlockSpec((1, D_MODEL), lambda t, nv, ca, cb, ea_, eb_, ps: (0, 0))
    gu = (1, 1, D_MODEL, D_EXPERT)
    dn = (1, 1, D_EXPERT, D_MODEL)
    grid_spec = pltpu.PrefetchScalarGridSpec(
        num_scalar_prefetch=6,
        grid=(n_tiles,),
        in_specs=[pl.BlockSpec(memory_space=pl.ANY), vec, vec,
                  wspec_a(gu), wspec_a(gu), wspec_a(dn),
                  wspec_b(gu), wspec_b(gu), wspec_b(dn)],
        out_specs=pl.BlockSpec(memory_space=pl.ANY),
        scratch_shapes=[
            pltpu.VMEM((tm, X1E_W), F32), pltpu.VMEM((tm, X1E_W), F32), pltpu.VMEM((tm, X1E_W), F32),
            pltpu.VMEM((tm, D_MODEL), F32), pltpu.VMEM((tm, D_MODEL), F32),
            pltpu.VMEM((tm, D_MODEL), F32),
            pltpu.VMEM((D_MODEL, D_EXPERT), MM), pltpu.VMEM((D_MODEL, D_EXPERT), MM),
            pltpu.VMEM((D_EXPERT, D_MODEL), MM),
            pltpu.VMEM((D_MODEL, D_EXPERT), MM), pltpu.VMEM((D_MODEL, D_EXPERT), MM),
            pltpu.VMEM((D_EXPERT, D_MODEL), MM),
            pltpu.SMEM((n_rows,), I32),
            pltpu.SemaphoreType.DMA((3,)),
            pltpu.SemaphoreType.DMA((3,)),
        ],
    )
    return pl.pallas_call(
        kern,
        grid_spec=grid_spec,
        out_shape=jax.ShapeDtypeStruct((n + 3 * tm, D_MODEL), F32),
        compiler_params=_cparams(("arbitrary",)),
        name="moe",
    )(nvalid, chga, chgb, ea, eb, pos,
      x1e, ln_g, ln_b, w_gate, w_up, w_down, w_gate, w_up, w_down)


def _pad_hist(hist, rows):
    b, r, c = hist.shape
    return jnp.concatenate([jnp.zeros((b, rows - r, c), hist.dtype), hist], axis=1)


def _new_hist(hist, cur, c0, c1, keep):
    t = cur.shape[1]
    if t >= keep:
        return cur[:, t - keep:, c0:c1]
    return jnp.concatenate([hist[:, t:], cur[:, :, c0:c1]], axis=1)


def kernel(x_prompt, x_sample, state_delta, state_delta_conv, state_pool, state_lru, state_lru_conv,
           ab_w_in, a_conv_w, a_log_decay, a_dt_bias, a_norm_w, b_w_group, b_scale, ab_w_out,
           c_w_in, c_conv_w, c_conv_b, c_w_r, c_b_r, c_w_i, c_b_i, c_lambda, c_w_out,
           ln_mix_g, ln_mix_b, ln_ffn_g, ln_ffn_b, router_w, router_bias,
           moe_w_gate, moe_w_up, moe_w_down):
    bp, tp, d = x_prompt.shape
    bs, ts, _ = x_sample.shape
    n_p, n_s = bp * tp, bs * ts
    n = n_p + n_s
    tm = TOKEN_TILE
    tmix = MIX_TILE
    assert n_p % tmix == 0 and n_s % tmix == 0 and tmix % tm == 0

    xin = [(x_prompt.reshape(n_p, d), 0), (x_sample.reshape(n_s, d), 0)]
    groups = (
        dict(batch=bp, seq=tp, hist_valid=0, fresh=True,
             ev=dict(nb_blk=1, rows=min(tp, 256)), od=dict(nb_blk=min(bp, 8), rows=min(tp, 64))),
        dict(batch=bs, seq=ts, hist_valid=B_HIST, fresh=False,
             ev=dict(nb_blk=min(bs, 8), rows=ts), od=dict(nb_blk=min(bs, 32), rows=ts)),
    )
    rwt = router_w.T
    rb = router_bias.reshape(N_EXPERTS, 1)
    row = lambda v: v.reshape(1, -1)

    new = {k: ([], []) for k in ("dconv", "pool", "lru", "lconv")}
    delta_new = [None, None]
    for layer in range(DEPTH):
        j = layer // 2
        mixed = []
        if layer % 2 == 0:
            w = ab_w_in[j]
            c1 = EV_Z0 + A_V
            w_perm = jnp.concatenate(
                [w[:, :c1], w[:, c1 + 2 * A_HEADS:], w[:, c1:c1 + 2 * A_HEADS],
                 jnp.zeros((d, LANES - 2 * A_HEADS), w.dtype)], axis=1).astype(MM)
            gvec = jnp.zeros((2, LANES), F32)
            gvec = gvec.at[0, A_HEADS:2 * A_HEADS].set(a_log_decay[j])
            gvec = gvec.at[1, A_HEADS:2 * A_HEADS].set(a_dt_bias[j])
            for gi, g in enumerate(groups):
                b_, t_ = g["batch"], g["seq"]
                proj = _proj(xin[gi][0], w_perm, xin[gi][1], b_ * t_, tm)
                p3 = proj.reshape(b_, t_, EV_W)
                if g["fresh"]:
                    dconv = jnp.zeros((b_, 3, A_CONV_CH), F32)
                    delta_all = jnp.zeros((1, b_, A_HEADS, A_DK, A_DV), F32)
                    pool = jnp.zeros((b_, B_HIST, B_WIDTH), F32)
                else:
                    dconv, delta_all, pool = state_delta_conv[j], state_delta, state_pool[j]
                o, delta_new[gi] = _even_mixer(
                    proj, _pad_hist(dconv, SUBLANES), delta_all, j, _pad_hist(pool, 2 * SUBLANES),
                    a_conv_w[j], gvec, row(a_norm_w[j]), b_w_group[j].astype(MM), row(b_scale[j]),
                    delta_new[gi], batch=b_, seq=t_, hist_valid=g["hist_valid"], **g["ev"])
                mixed.append(o)
                new["dconv"][gi].append(_new_hist(dconv, p3, 0, A_CONV_CH, 3))
                new["pool"][gi].append(_new_hist(pool, p3, EV_U0, EV_U0 + B_WIDTH, B_HIST))
            w_out = ab_w_out[j].astype(MM)
        else:
            w_mm = c_w_in[j].astype(MM)
            for gi, g in enumerate(groups):
                b_, t_ = g["batch"], g["seq"]
                p3 = _proj(xin[gi][0], w_mm, xin[gi][1], b_ * t_, tm).reshape(b_, t_, 2 * C_WIDTH)
                if g["fresh"]:
                    lconv = jnp.zeros((b_, 3, C_WIDTH), F32)
                    lru = jnp.zeros((b_, C_WIDTH), F32)
                else:
                    lconv, lru = state_lru_conv[j], state_lru[j]
                o3, h_last = _odd_mixer(
                    p3, _pad_hist(lconv, SUBLANES), lru, c_conv_w[j], row(c_conv_b[j]),
                    c_w_r[j].astype(MM), row(c_b_r[j]), c_w_i[j].astype(MM), row(c_b_i[j]),
                    row(c_lambda[j]), **g["od"])
                mixed.append(o3.reshape(b_ * t_, C_WIDTH))
                new["lru"][gi].append(h_last)
                new["lconv"][gi].append(_new_hist(lconv, p3, C_WIDTH, 2 * C_WIDTH, 3))
            w_out = c_w_out[j].astype(MM)

        x1e, key, cnt = _mix_out(mixed[0], mixed[1], w_out, xin[0][0], xin[0][1], xin[1][0], xin[1][1],
                                 row(ln_mix_g[layer]), row(ln_mix_b[layer]), rwt, rb, tmix)
        x = _moe(x1e, key, cnt, row(ln_ffn_g[layer]), row(ln_ffn_b[layer]),
                 moe_w_gate, moe_w_up, moe_w_down, layer, tm)
        xin = [(x, 0), (x, n_p)]

    stack = lambda key, gi: jnp.stack(new[key][gi])
    return (x[:n_p].reshape(bp, tp, d), x[n_p:n].reshape(bs, ts, d),
            delta_new[0], stack("dconv", 0), stack("pool", 0), stack("lru", 0), stack("lconv", 0),
            delta_new[1], stack("dconv", 1), stack("pool", 1), stack("lru", 1), stack("lconv", 1))
```

```python
import functools

import jax
import jax.numpy as jnp
from jax import lax
from jax.experimental import pallas as pl
from jax.experimental.pallas import tpu as pltpu

F32 = jnp.float32
I32 = jnp.int32
MM = jnp.bfloat16

D_MODEL = 1024
DEPTH = 4
N_A_LAYERS = (DEPTH + 1) // 2
A_HEADS = 4
A_DK = 128
A_DV = 128
A_QK = A_HEADS * A_DK
A_V = A_HEADS * A_DV
A_CONV_CH = 2 * A_QK + A_V
A_CHUNK = 64
B_GROUPS = 4
B_GW = 128
B_WINDOWS = (2, 4, 8, 16)
B_HIST = 15
B_WIDTH = B_GROUPS * B_GW
C_WIDTH = D_MODEL
C_HEADS = 4
C_HW = C_WIDTH // C_HEADS
C_GATE = 8.0
N_EXPERTS = 16
N_GROUPS = 4
EPG = N_EXPERTS // N_GROUPS
D_EXPERT = 512
ALPHA = (2 * DEPTH) ** 0.25
LN_EPS = 1e-5
RMS_EPS = 1e-6

LANES = 128
SUBLANES = 8
VMEM_LIMIT = 56 * 1024 * 1024

EV_Z0 = A_CONV_CH
EV_U0 = EV_Z0 + A_V
EV_BA0 = EV_U0 + B_WIDTH
EV_W = EV_BA0 + LANES

PAIRS = ((0, 1), (0, 2), (0, 3), (1, 2), (1, 3), (2, 3))
N_CLASSES = N_GROUPS * len(PAIRS)

TOKEN_TILE = 256
MIX_TILE = 512
CLS_ROWS = 32
KEY_CLS = 1 << 16
X1E_W = D_MODEL + LANES


def _cparams(sem):
    return pltpu.CompilerParams(dimension_semantics=sem, vmem_limit_bytes=VMEM_LIMIT)


def _dot(a, b):
    return jnp.dot(a.astype(MM), b.astype(MM), preferred_element_type=F32)


def _dot_nt(a, b):
    return lax.dot_general(a.astype(MM), b.astype(MM), (((1,), (1,)), ((), ())),
                           preferred_element_type=F32)


def _dot_tn(a, b):
    return lax.dot_general(a.astype(MM), b.astype(MM), (((0,), (0,)), ((), ())),
                           preferred_element_type=F32)


def _split3(v):
    hi = v.astype(MM)
    r1 = v - hi.astype(F32)
    mid = r1.astype(MM)
    lo = (r1 - mid.astype(F32)).astype(MM)
    return hi, mid, lo


def _sigmoid(x):
    return 1.0 / (1.0 + jnp.exp(-x))


def _sigmoid_t(x):
    return 0.5 * jnp.tanh(0.5 * x) + 0.5


def _silu(x):
    return x * _sigmoid_t(x)


def _softplus(x):
    return jnp.maximum(x, 0.0) + jnp.log1p(jnp.exp(-jnp.abs(x)))


def _gelu_tanh(x):
    return x * (0.5 * (1.0 + jnp.tanh(0.7978845608028654 * (x + 0.044715 * (x * x * x)))))


def _layer_norm(v, g, b):
    mu = jnp.mean(v, axis=-1, keepdims=True)
    d = v - mu
    var = jnp.mean(d * d, axis=-1, keepdims=True)
    return d * lax.rsqrt(var + LN_EPS) * g + b


def _proj_kernel(x_ref, w_ref, o_ref):
    o_ref[...] = jnp.dot(x_ref[...].astype(MM), w_ref[...], preferred_element_type=F32)


def _proj(x, w_mm, row0, nrows, tm):
    k, width = w_mm.shape
    off = row0 // tm
    return pl.pallas_call(
        _proj_kernel,
        grid=(nrows // tm,),
        in_specs=[pl.BlockSpec((tm, k), lambda i: (i + off, 0)),
                  pl.BlockSpec((k, width), lambda i: (0, 0))],
        out_specs=pl.BlockSpec((tm, width), lambda i: (i, 0)),
        out_shape=jax.ShapeDtypeStruct((nrows, width), F32),
        compiler_params=_cparams(("parallel",)),
        name="proj",
    )(x, w_mm)


def _even_kernel(*refs, nb_blk, rows, chunk, hist_valid, carry_hist, has_prev):
    (p_ref, convh_ref, s0_ref, poolh_ref, convw_ref, gvec_ref, normw_ref,
     wgrp_ref, scale_ref) = refs[:9]
    o_ref, snew_ref, ext_scr, qkv_scr, pext_scr, s_scr = refs[9 + (1 if has_prev else 0):]
    i = pl.program_id(1)
    n_chunks = rows // chunk
    n_neumann = max((chunk - 1).bit_length() - 1, 0)
    hb = 2 * SUBLANES

    @pl.when(i == 0)
    def _():
        s_scr[...] = s0_ref[0]
        ext_scr[:, 0:SUBLANES, :] = convh_ref[...]
        pext_scr[:, 0:hb, :] = poolh_ref[...]

    rid = lax.broadcasted_iota(I32, (chunk, chunk), 0)
    cid = lax.broadcasted_iota(I32, (chunk, chunk), 1)
    incl = rid >= cid
    strict = rid > cid
    eye = (rid == cid).astype(F32)
    ltri = incl.astype(F32)
    lane = lax.broadcasted_iota(I32, (chunk, LANES), 1)
    trow = lax.broadcasted_iota(I32, (rows, LANES), 0)
    pos = (i * rows + trow + (1 + hist_valid)).astype(F32)
    neg_decay_rate = -jnp.exp(gvec_ref[0:1, :])
    dt_bias = gvec_ref[1:2, :]

    for nb in range(nb_blk):
        r0 = nb * rows
        ext_scr[nb, SUBLANES:SUBLANES + rows, :] = p_ref[r0:r0 + rows, 0:A_CONV_CH]
        for ct in range(A_CONV_CH // LANES):
            cs = slice(ct * LANES, (ct + 1) * LANES)
            acc = None
            for j in range(4):
                lo = SUBLANES - 3 + j
                term = ext_scr[nb, lo:lo + rows, cs] * convw_ref[j:j + 1, cs]
                acc = term if acc is None else acc + term
            t = _silu(acc)
            if ct < 2 * A_HEADS:
                t = t * lax.rsqrt(jnp.sum(t * t, axis=-1, keepdims=True) + 1e-6)
                if ct < A_HEADS:
                    t = t * (A_DK ** -0.5)
            qkv_scr[r0:r0 + rows, cs] = t
        if carry_hist:
            ext_scr[nb, 0:SUBLANES, :] = ext_scr[nb, rows:rows + SUBLANES, :]

        pext_scr[nb, hb:hb + rows, :] = p_ref[r0:r0 + rows, EV_U0:EV_U0 + B_WIDTH]
        for gi, w in enumerate(B_WINDOWS):
            cs = slice(gi * B_GW, (gi + 1) * B_GW)
            cur = pext_scr[nb, hb:hb + rows, cs]
            tot = cur
            for j in range(1, w):
                tot = tot + pext_scr[nb, hb - j:hb - j + rows, cs]
            pooled = tot / jnp.minimum(pos, float(w)) - cur
            ob = _dot(pooled, wgrp_ref[gi]) * scale_ref[:, cs]
            o_ref[r0:r0 + rows, A_V + gi * B_GW:A_V + (gi + 1) * B_GW] = ob
        if carry_hist:
            pext_scr[nb, 0:hb, :] = pext_scr[nb, rows:rows + hb, :]

    chunks = [(nb, c) for nb in range(nb_blk) for c in range(n_chunks)]
    probs = [(nb, c, h) for nb, c in chunks for h in range(A_HEADS)]
    rs = {(nb, c): slice(nb * rows + c * chunk, nb * rows + (c + 1) * chunk) for nb, c in chunks}
    gcol = lambda arr, h: arr[:, A_HEADS + h:A_HEADS + h + 1]
    qf = lambda p: qkv_scr[rs[p[:2]], p[2] * LANES:(p[2] + 1) * LANES]
    kf = lambda p: qkv_scr[rs[p[:2]], A_QK + p[2] * LANES:A_QK + (p[2] + 1) * LANES]
    vf = lambda p: qkv_scr[rs[p[:2]], 2 * A_QK + p[2] * LANES:2 * A_QK + (p[2] + 1) * LANES]

    ba = {ck: p_ref[rs[ck], EV_BA0:EV_BA0 + LANES] for ck in chunks}
    beta_full = {ck: _sigmoid(ba[ck]) for ck in chunks}
    g_full = {ck: neg_decay_rate * _softplus(ba[ck] + dt_bias) for ck in chunks}
    g_parts = {ck: _split3(g_full[ck]) for ck in chunks}
    gc_full = {ck: sum(jnp.dot(ltri.astype(MM), part, preferred_element_type=F32)
                       for part in g_parts[ck]) for ck in chunks}
    eg_full = {ck: jnp.exp(gc_full[ck]) for ck in chunks}
    gl_full = {ck: gc_full[ck][chunk - 1:chunk, :] for ck in chunks}
    ekd_full = {ck: jnp.exp(gl_full[ck] - gc_full[ck]) for ck in chunks}
    egl_full = {ck: jnp.exp(gl_full[ck]) for ck in chunks}
    sel = [(lane == A_HEADS + h).astype(MM) for h in range(A_HEADS)]
    gc_parts = {ck: _split3(gc_full[ck]) for ck in chunks}
    grow = {p: sum(lax.dot_general(sel[p[2]], part, (((1,), (1,)), ((), ())),
                                   preferred_element_type=F32) for part in gc_parts[p[:2]])
            for p in probs}
    decay = {p: jnp.where(incl, jnp.exp(jnp.where(incl, gcol(gc_full[p[:2]], p[2]) - grow[p], 0.0)), 0.0)
             for p in probs}
    beta = {p: beta_full[p[:2]][:, p[2]:p[2] + 1] for p in probs}
    kq = {p: _dot_nt(jnp.concatenate([kf(p) * beta[p], qf(p)], axis=0), kf(p)) for p in probs}
    a_intra = {p: jnp.where(incl, kq[p][chunk:2 * chunk] * decay[p], 0.0) for p in probs}
    power = {p: -jnp.where(strict, kq[p][0:chunk] * decay[p], 0.0) for p in probs}
    tinv = {p: eye + power[p] for p in probs}
    for _ in range(n_neumann):
        power = {p: _dot(power[p], power[p]) for p in probs}
        tinv = {p: tinv[p] + _dot(tinv[p], power[p]) for p in probs}
    uw = {p: _dot(tinv[p], jnp.concatenate(
        [vf(p) * beta[p], kf(p) * beta[p] * gcol(eg_full[p[:2]], p[2])], axis=1)) for p in probs}
    u_in = {p: uw[p][:, 0:A_DV] for p in probs}
    w_in = {p: uw[p][:, A_DV:A_DV + A_DK] for p in probs}

    for c in range(n_chunks):
        cp = [(nb, c, h) for nb in range(nb_blk) for h in range(A_HEADS)]
        s_old = {p: s_scr[p[0], p[2]] for p in cp}
        wq = {p: _dot(jnp.concatenate([w_in[p], qf(p) * gcol(eg_full[p[:2]], p[2])], axis=0), s_old[p])
              for p in cp}
        u_new = {p: u_in[p] - wq[p][0:chunk] for p in cp}
        au = {p: _dot(a_intra[p], u_new[p]) for p in cp}
        ku = {p: _dot_tn(kf(p) * gcol(ekd_full[p[:2]], p[2]), u_new[p]) for p in cp}
        for p in cp:
            nb, _, h = p
            s_scr[nb, h] = s_old[p] * gcol(egl_full[p[:2]], h) + ku[p]
            o = wq[p][chunk:2 * chunk] + au[p]
            o = o * lax.rsqrt(jnp.mean(o * o, axis=-1, keepdims=True) + RMS_EPS) * normw_ref[...]
            z = p_ref[rs[p[:2]], EV_Z0 + h * LANES:EV_Z0 + (h + 1) * LANES]
            o_ref[rs[p[:2]], h * LANES:(h + 1) * LANES] = o * _silu(z)

    @pl.when(i == pl.num_programs(1) - 1)
    def _():
        snew_ref[0] = s_scr[...]
        if not has_prev:
            for other in range(1, snew_ref.shape[0]):
                snew_ref[other] = jnp.zeros(s_scr.shape, F32)


def _even_mixer(proj, convh8, s0_all, layer_j, poolh16, convw, gvec, normw, wgrp_mm, scale, snew_prev,
                *, batch, seq, nb_blk, rows, hist_valid):
    chunk = min(A_CHUNK, seq)
    assert seq % rows == 0 and rows % chunk == 0 and batch % nb_blk == 0
    assert nb_blk == 1 or rows == seq
    t_blocks = seq // rows
    blk = nb_blk * rows
    hb = 2 * SUBLANES
    has_prev = snew_prev is not None
    slot = layer_j if s0_all.shape[0] > 1 else 0
    kern = functools.partial(_even_kernel, nb_blk=nb_blk, rows=rows, chunk=chunk,
                             hist_valid=hist_valid, carry_hist=t_blocks > 1, has_prev=has_prev)
    full = lambda shape: pl.BlockSpec(shape, lambda b, i: (0,) * len(shape))
    state_blk = (1, nb_blk, A_HEADS, A_DK, A_DV)
    in_specs = [
        pl.BlockSpec((blk, EV_W), lambda b, i: (b * t_blocks + i, 0)),
        pl.BlockSpec((nb_blk, SUBLANES, A_CONV_CH), lambda b, i: (b, 0, 0)),
        pl.BlockSpec(state_blk, lambda b, i: (slot, b, 0, 0, 0)),
        pl.BlockSpec((nb_blk, hb, B_WIDTH), lambda b, i: (b, 0, 0)),
        full((4, A_CONV_CH)), full((2, LANES)), full((1, A_DV)),
        full((B_GROUPS, B_GW, B_GW)), full((1, B_WIDTH)),
    ]
    args = [proj, convh8, s0_all, poolh16, convw, gvec, normw, wgrp_mm, scale]
    if has_prev:
        in_specs.append(pl.BlockSpec(memory_space=pl.ANY))
        args.append(snew_prev)
        snew_spec = pl.BlockSpec(state_blk, lambda b, i: (layer_j, b, 0, 0, 0))
    else:
        snew_spec = pl.BlockSpec((N_A_LAYERS,) + state_blk[1:], lambda b, i: (0, b, 0, 0, 0))
    return pl.pallas_call(
        kern,
        grid=(batch // nb_blk, t_blocks),
        in_specs=in_specs,
        out_specs=[
            pl.BlockSpec((blk, D_MODEL), lambda b, i: (b * t_blocks + i, 0)),
            snew_spec,
        ],
        out_shape=[jax.ShapeDtypeStruct((batch * seq, D_MODEL), F32),
                   jax.ShapeDtypeStruct((N_A_LAYERS, batch, A_HEADS, A_DK, A_DV), F32)],
        scratch_shapes=[
            pltpu.VMEM((nb_blk, SUBLANES + rows, A_CONV_CH), F32),
            pltpu.VMEM((blk, A_CONV_CH), F32),
            pltpu.VMEM((nb_blk, hb + rows, B_WIDTH), F32),
            pltpu.VMEM((nb_blk, A_HEADS, A_DK, A_DV), F32),
        ],
        input_output_aliases={len(args) - 1: 1} if has_prev else {},
        compiler_params=_cparams(("parallel", "arbitrary")),
        name="even_mixer",
    )(*args)


def _odd_kernel(p_ref, convh_ref, h0_ref, convw_ref, convb_ref, wr_ref, br_ref, wi_ref, bi_ref,
                lam_ref, o_ref, hlast_ref, ext_scr, a_scr, b_scr, hs_scr, h_scr,
                *, nb_blk, rows, carry_hist):
    i = pl.program_id(1)
    blk = nb_blk * rows
    pitch = a_scr.shape[1] // nb_blk

    @pl.when(i == 0)
    def _():
        h_scr[...] = h0_ref[...]
        ext_scr[:, 0:SUBLANES, :] = convh_ref[...]

    ext_scr[:, SUBLANES:SUBLANES + rows, :] = p_ref[:, :, C_WIDTH:2 * C_WIDTH]
    log_base = -C_GATE * _softplus(-lam_ref[...])
    for h in range(C_HEADS):
        cs = slice(h * C_HW, (h + 1) * C_HW)
        acc = None
        for j in range(4):
            lo = SUBLANES - 3 + j
            term = ext_scr[:, lo:lo + rows, cs] * convw_ref[j:j + 1, cs]
            acc = term if acc is None else acc + term
        xc = (acc + convb_ref[:, cs]).reshape(blk, C_HW)
        r = _sigmoid_t(_dot(xc, wr_ref[h]) + br_ref[:, cs])
        gi = _sigmoid_t(_dot(xc, wi_ref[h]) + bi_ref[:, cs])
        log_a = r * log_base[:, cs]
        a = jnp.exp(log_a)
        th = jnp.tanh(log_a)
        bt = jnp.sqrt(-2.0 * th) * lax.rsqrt(1.0 - th) * gi * xc
        for lt in range(C_HW // LANES):
            for b in range(nb_blk):
                dst = pl.ds(b * pitch, rows)
                src = slice(b * rows, (b + 1) * rows)
                a_scr[h * (C_HW // LANES) + lt, dst, :] = a[src, lt * LANES:(lt + 1) * LANES]
                b_scr[h * (C_HW // LANES) + lt, dst, :] = bt[src, lt * LANES:(lt + 1) * LANES]
    if carry_hist:
        ext_scr[:, 0:SUBLANES, :] = ext_scr[:, rows:rows + SUBLANES, :]

    n_lt = C_WIDTH // LANES

    def step(t, hcur):
        hnew = []
        for lt in range(n_lt):
            a_t = a_scr[lt, pl.ds(t, nb_blk, stride=pitch), :]
            b_t = b_scr[lt, pl.ds(t, nb_blk, stride=pitch), :]
            hn = a_t * hcur[lt] + b_t
            hs_scr[lt, pl.ds(t, nb_blk, stride=pitch), :] = hn
            hnew.append(hn)
        return tuple(hnew)

    h_init = tuple(h_scr[:, lt * LANES:(lt + 1) * LANES] for lt in range(n_lt))
    h_fin = lax.fori_loop(0, rows, step, h_init, unroll=SUBLANES)
    for lt in range(n_lt):
        cs = slice(lt * LANES, (lt + 1) * LANES)
        h_scr[:, cs] = h_fin[lt]
        for b in range(nb_blk):
            o_ref[b, :, cs] = _gelu_tanh(p_ref[b, :, cs]) * hs_scr[lt, pl.ds(b * pitch, rows), :]

    @pl.when(i == pl.num_programs(1) - 1)
    def _():
        hlast_ref[...] = h_scr[...]


def _odd_mixer(proj3, convh8, h0, convw, convb, wr_mm, br, wi_mm, bi, lam, *, nb_blk, rows):
    batch, seq, _ = proj3.shape
    assert seq % rows == 0 and batch % nb_blk == 0 and rows % SUBLANES == 0
    t_blocks = seq // rows
    pitch = rows + SUBLANES
    kern = functools.partial(_odd_kernel, nb_blk=nb_blk, rows=rows, carry_hist=t_blocks > 1)
    full = lambda shape: pl.BlockSpec(shape, lambda b, i: (0,) * len(shape))
    return pl.pallas_call(
        kern,
        grid=(batch // nb_blk, t_blocks),
        in_specs=[
            pl.BlockSpec((nb_blk, rows, 2 * C_WIDTH), lambda b, i: (b, i, 0)),
            pl.BlockSpec((nb_blk, SUBLANES, C_WIDTH), lambda b, i: (b, 0, 0)),
            pl.BlockSpec((nb_blk, C_WIDTH), lambda b, i: (b, 0)),
            full((4, C_WIDTH)), full((1, C_WIDTH)),
            full((C_HEADS, C_HW, C_HW)), full((1, C_WIDTH)),
            full((C_HEADS, C_HW, C_HW)), full((1, C_WIDTH)),
            full((1, C_WIDTH)),
        ],
        out_specs=[
            pl.BlockSpec((nb_blk, rows, C_WIDTH), lambda b, i: (b, i, 0)),
            pl.BlockSpec((nb_blk, C_WIDTH), lambda b, i: (b, 0)),
        ],
        out_shape=[jax.ShapeDtypeStruct((batch, seq, C_WIDTH), F32),
                   jax.ShapeDtypeStruct((batch, C_WIDTH), F32)],
        scratch_shapes=[
            pltpu.VMEM((nb_blk, SUBLANES + rows, C_WIDTH), F32),
            pltpu.VMEM((C_WIDTH // LANES, nb_blk * pitch, LANES), F32),
            pltpu.VMEM((C_WIDTH // LANES, nb_blk * pitch, LANES), F32),
            pltpu.VMEM((C_WIDTH // LANES, nb_blk * pitch, LANES), F32),
            pltpu.VMEM((nb_blk, C_WIDTH), F32),
        ],
        compiler_params=_cparams(("parallel", "arbitrary")),
        name="odd_mixer",
    )(proj3, convh8, h0, convw, convb, wr_mm, br, wi_mm, bi, lam)


def _route(x1, rwt_ref, rb_ref):
    def split(v):
        hi = v.astype(MM)
        return hi, (v - hi.astype(F32)).astype(MM)

    nt = lambda a, b: lax.dot_general(a, b, (((1,), (1,)), ((), ())), preferred_element_type=F32)
    r_hi, r_lo = split(rwt_ref[...])
    x_hi, x_lo = split(x1)
    logits = nt(r_hi, x_hi) + nt(r_hi, x_lo) + nt(r_lo, x_hi)
    sc = _sigmoid(logits)
    bz = sc + rb_ref[...]
    row = lambda arr, e: arr[e:e + 1, :]
    best = None
    gidx = None
    for g in range(N_GROUPS):
        r = [row(bz, EPG * g + k) for k in range(EPG)]
        top2 = None
        for a, b in PAIRS:
            s = r[a] + r[b]
            top2 = s if top2 is None else jnp.maximum(top2, s)
        if best is None:
            best, gidx = top2, jnp.zeros(top2.shape, I32)
        else:
            upd = top2 > best
            gidx = jnp.where(upd, g, gidx)
            best = jnp.where(upd, top2, best)
    sb, ss = [], []
    for k in range(EPG):
        vb, vs = row(bz, k), row(sc, k)
        for g in range(1, N_GROUPS):
            vb = jnp.where(gidx == g, row(bz, EPG * g + k), vb)
            vs = jnp.where(gidx == g, row(sc, EPG * g + k), vs)
        sb.append(vb)
        ss.append(vs)
    m1, i1 = sb[0], jnp.zeros(sb[0].shape, I32)
    for k in range(1, EPG):
        upd = sb[k] > m1
        i1 = jnp.where(upd, k, i1)
        m1 = jnp.where(upd, sb[k], m1)
    m2, i2 = None, None
    for k in range(EPG):
        cand = jnp.where(i1 == k, -jnp.inf, sb[k])
        if m2 is None:
            m2, i2 = cand, jnp.zeros(cand.shape, I32)
        else:
            upd = cand > m2
            i2 = jnp.where(upd, k, i2)
            m2 = jnp.where(upd, cand, m2)
    lo = jnp.minimum(i1, i2)
    hi = jnp.maximum(i1, i2)
    pair = jnp.where(lo == 0, 0, jnp.where(lo == 1, 3, 5)) + hi - lo - 1
    s_lo, s_hi = ss[0], ss[0]
    for k in range(1, EPG):
        s_lo = jnp.where(lo == k, ss[k], s_lo)
        s_hi = jnp.where(hi == k, ss[k], s_hi)
    den = s_lo + s_hi
    return gidx * len(PAIRS) + pair, s_lo / den, s_hi / den


def _mix_out_kernel(ap_ref, as_ref, w_ref, xp_ref, xs_ref, g_ref, b_ref, rwt_ref, rb_ref, upper_ref,
                    x1e_ref, key_ref, cnt_ref, run_scr, *, np_tiles):
    i = pl.program_id(0)
    tm = xp_ref.shape[0]

    @pl.when(i == 0)
    def _():
        run_scr[...] = jnp.zeros(run_scr.shape, F32)

    def finish(a_ref, x_ref):
        hmix = jnp.dot(a_ref[...].astype(MM), w_ref[...], preferred_element_type=F32)
        x1 = _layer_norm(ALPHA * x_ref[...] + hmix, g_ref[...], b_ref[...])
        cls, wa, wb = _route(x1, rwt_ref, rb_ref)
        crow = lax.broadcasted_iota(I32, (CLS_ROWS, tm), 0)
        onehot = (crow == cls).astype(F32)
        prefix = jnp.dot(onehot.astype(MM), upper_ref[...], preferred_element_type=F32)
        run = run_scr[:, 0:1]
        rank = jnp.sum(onehot * (prefix + (run - 1.0)), axis=0, keepdims=True)
        run_new = run + jnp.sum(onehot, axis=1, keepdims=True)
        run_scr[...] = jnp.broadcast_to(run_new, run_scr.shape)
        cnt_ref[...] = jnp.broadcast_to(run_new, cnt_ref.shape).astype(I32)
        key_ref[...] = cls * KEY_CLS + rank.astype(I32)
        wrow = lax.broadcasted_iota(I32, (LANES, tm), 0)
        wpad = jnp.where(wrow == 0, wa, jnp.where(wrow == 1, wb, 0.0))
        x1e_ref[:, 0:D_MODEL] = x1
        x1e_ref[:, D_MODEL:D_MODEL + LANES] = wpad.T

    @pl.when(i < np_tiles)
    def _():
        finish(ap_ref, xp_ref)

    @pl.when(i >= np_tiles)
    def _():
        finish(as_ref, xs_ref)


def _mix_out(a_p, a_s, w_mm, x_p, row0_p, x_s, row0_s, g, b, rwt, rb, tm):
    n = a_p.shape[0] + a_s.shape[0]
    np_tiles = a_p.shape[0] // tm
    ns_tiles = a_s.shape[0] // tm
    assert (np_tiles + ns_tiles) * tm == n and row0_p % tm == 0 and row0_s % tm == 0
    off_p, off_s = row0_p // tm, row0_s // tm
    kern = functools.partial(_mix_out_kernel, np_tiles=np_tiles)
    full = lambda shape: pl.BlockSpec(shape, lambda i: (0,) * len(shape))
    upper = (jnp.arange(tm)[:, None] <= jnp.arange(tm)[None, :]).astype(MM)
    return pl.pallas_call(
        kern,
        grid=(np_tiles + ns_tiles,),
        in_specs=[
            pl.BlockSpec((tm, D_MODEL), lambda i: (jnp.minimum(i, np_tiles - 1), 0)),
            pl.BlockSpec((tm, D_MODEL), lambda i: (jnp.maximum(i - np_tiles, 0), 0)),
            full((D_MODEL, D_MODEL)),
            pl.BlockSpec((tm, D_MODEL), lambda i: (jnp.minimum(i, np_tiles - 1) + off_p, 0)),
            pl.BlockSpec((tm, D_MODEL), lambda i: (jnp.maximum(i - np_tiles, 0) + off_s, 0)),
            full((1, D_MODEL)), full((1, D_MODEL)),
            full((N_EXPERTS, D_MODEL)), full((N_EXPERTS, 1)),
            full((tm, tm)),
        ],
        out_specs=[
            pl.BlockSpec((tm, X1E_W), lambda i: (i, 0)),
            pl.BlockSpec((1, tm), lambda i: (0, i)),
            full((CLS_ROWS, LANES)),
        ],
        out_shape=[jax.ShapeDtypeStruct((n, X1E_W), F32),
                   jax.ShapeDtypeStruct((1, n), I32),
                   jax.ShapeDtypeStruct((CLS_ROWS, LANES), I32)],
        scratch_shapes=[pltpu.VMEM((CLS_ROWS, LANES), F32)],
        compiler_params=_cparams(("arbitrary",)),
        name="mix_out",
    )(a_p, a_s, w_mm, x_p, x_s, g, b, rwt, rb, upper)


def _moe_kernel(nvalid_ref, chga_ref, chgb_ref, ea_ref, eb_ref, pos_ref,
                x_hbm, g_ref, b_ref, wga_ref, wua_ref, wda_ref, wgb_ref, wub_ref, wdb_ref,
                y_hbm, xbuf0, xbuf1, xbuf2, obuf0, obuf1, obuf2,
                ga_scr, ua_scr, da_scr, gb_scr, ub_scr, db_scr, rowsrc, gsem, ssem, *, tm, n_tok):
    t = pl.program_id(0)
    nt = pl.num_programs(0)
    xbufs, obufs = (xbuf0, xbuf1, xbuf2), (obuf0, obuf1, obuf2)

    def gather_copy(tok, r, sl):
        return pltpu.make_async_copy(x_hbm.at[pl.ds(tok, 1)], xbufs[sl].at[pl.ds(r, 1)], gsem.at[sl])

    def scatter_copy(tok, r, sl):
        return pltpu.make_async_copy(obufs[sl].at[pl.ds(r, 1)], y_hbm.at[pl.ds(tok, 1)], ssem.at[sl])

    def src_token(tile, n_tile, r):
        return rowsrc[jnp.where(r < n_tile, tile * tm + r, 0)]

    def tile_rows(tile):
        return jnp.where(tile < nt, nvalid_ref[jnp.minimum(tile, nt - 1)], 0)

    def issue_gather(tile, n_tile, sl, r, priority=0):
        gather_copy(src_token(tile, n_tile, r), r, sl).start(priority=priority)

    def issue_scatter(tile, n_tile, sl, r, priority=0):
        dst = jnp.where(r < n_tile, src_token(tile, n_tile, r), n_tok + sl * tm + r)
        scatter_copy(dst, r, sl).start(priority=priority)

    def wait_gather(sl):
        pltpu.make_async_copy(x_hbm.at[pl.ds(0, tm)], xbufs[sl], gsem.at[sl]).wait()

    def wait_scatter(sl):
        pltpu.make_async_copy(obufs[sl], y_hbm.at[pl.ds(0, tm)], ssem.at[sl]).wait()

    @pl.when(t == 0)
    def _():
        def build(i, carry):
            rowsrc[pos_ref[i]] = i
            return carry
        lax.fori_loop(0, n_tok, build, 0, unroll=16)

        def body(r, carry):
            issue_gather(0, tile_rows(0), 0, r)
            issue_gather(1, tile_rows(1), 1, r)
            return carry
        lax.fori_loop(0, tm, body, 0, unroll=8)
        for sl in range(3):
            obufs[sl][...] = jnp.zeros(obufs[sl].shape, obufs[sl].dtype)
        for sl in range(2):
            pltpu.make_async_copy(obufs[sl], y_hbm.at[pl.ds(n_tok + sl * tm, tm)], ssem.at[sl]).start()

    @pl.when(chga_ref[t] == 1)
    def _():
        ga_scr[...] = wga_ref[0, 0].astype(MM)
        ua_scr[...] = wua_ref[0, 0].astype(MM)
        da_scr[...] = wda_ref[0, 0].astype(MM)

    @pl.when(chgb_ref[t] == 1)
    def _():
        gb_scr[...] = wgb_ref[0, 0].astype(MM)
        ub_scr[...] = wub_ref[0, 0].astype(MM)
        db_scr[...] = wdb_ref[0, 0].astype(MM)

    prev = jnp.maximum(t - 1, 0)
    n_prev = jnp.where(t > 0, nvalid_ref[prev], 0)
    n_cur = nvalid_ref[t]

    n_ahead = tile_rows(t + 2)

    def tile_block(sl):
        other = (sl + 2) % 3
        wait_gather(sl)
        wait_scatter(sl)
        x1 = xbufs[sl][:, 0:D_MODEL]
        x = x1.astype(MM)

        def expert(wg, wu, wd, wrow):
            gate = jnp.dot(x, wg[...], preferred_element_type=F32)
            up = jnp.dot(x, wu[...], preferred_element_type=F32)
            hid = (_silu(gate) * up).astype(MM)
            return jnp.dot(hid, wd[...], preferred_element_type=F32) * wrow

        f = (expert(ga_scr, ua_scr, da_scr, xbufs[sl][:, D_MODEL:D_MODEL + 1])
             + expert(gb_scr, ub_scr, db_scr, xbufs[sl][:, D_MODEL + 1:D_MODEL + 2]))
        obufs[sl][...] = _layer_norm(ALPHA * x1 + f, g_ref[...], b_ref[...])
        for r in range(tm):
            issue_gather(t + 2, n_ahead, other, r, priority=r % 2)
            issue_scatter(prev, n_prev, other, r, priority=r % 2)

    def drain(sl):
        other = (sl + 2) % 3
        wait_gather(sl)
        wait_gather((sl + 1) % 3)

        def body(r, carry):
            issue_scatter(prev, n_prev, other, r)
            return carry
        lax.fori_loop(0, tm, body, 0, unroll=8)
        for k in range(3):
            wait_scatter(k)

    for sl in range(3):
        pl.when(jnp.logical_and(n_cur > 0, t % 3 == sl))(functools.partial(tile_block, sl))
        pl.when(jnp.logical_and(jnp.logical_and(n_cur == 0, n_prev > 0), t % 3 == sl))(
            functools.partial(drain, sl))


def _moe(x1e, key, cnt, ln_g, ln_b, w_gate, w_up, w_down, layer, tm):
    n = x1e.shape[0]
    n_tiles = n // tm + N_CLASSES
    n_rows = n_tiles * tm
    counts = cnt[:N_CLASSES, 0]
    tiles_c = (counts + tm - 1) // tm
    tile_end_c = jnp.cumsum(tiles_c)
    tile_start_c = tile_end_c - tiles_c
    total = tile_end_c[-1]
    tid = jnp.arange(n_tiles, dtype=I32)
    t_eff = jnp.minimum(tid, total - 1)
    tile_cls = jnp.minimum(jnp.sum((tile_end_c[None, :] <= t_eff[:, None]).astype(I32), axis=1),
                           N_CLASSES - 1)
    sel = (tile_cls[:, None] == jnp.arange(N_CLASSES, dtype=I32)[None, :]).astype(I32)
    left = jnp.sum(sel * counts[None, :], axis=1) - (tid - jnp.sum(sel * tile_start_c[None, :], axis=1)) * tm
    nvalid = jnp.where(tid < total, jnp.clip(left, 0, tm), 0).astype(I32)
    pair_id = tile_cls % len(PAIRS)
    pair_lo = sum(jnp.where(pair_id == k, p[0], 0) for k, p in enumerate(PAIRS))
    pair_hi = sum(jnp.where(pair_id == k, p[1], 0) for k, p in enumerate(PAIRS))
    ea = (EPG * (tile_cls // len(PAIRS)) + pair_lo).astype(I32)
    eb = (EPG * (tile_cls // len(PAIRS)) + pair_hi).astype(I32)
    first = tid == 0
    chga = jnp.logical_or(first, ea != jnp.roll(ea, 1)).astype(I32)
    chgb = jnp.logical_or(first, eb != jnp.roll(eb, 1)).astype(I32)
    key = key.reshape(n)
    tok_cls = key // KEY_CLS
    tok_sel = (tok_cls[:, None] == jnp.arange(N_CLASSES, dtype=I32)[None, :]).astype(I32)
    pos = (jnp.sum(tok_sel * tile_start_c[None, :], axis=1) * tm + key % KEY_CLS).astype(I32)

    kern = functools.partial(_moe_kernel, tm=tm, n_tok=n)
    wspec_a = lambda shape: pl.BlockSpec(
        shape, lambda t, nv, ca, cb, ea_, eb_, ps: (layer, ea_[t], 0, 0))
    wspec_b = lambda shape: pl.BlockSpec(
        shape, lambda t, nv, ca, cb, ea_, eb_, ps: (layer, eb_[t], 0, 0))
    vec = pl.BlockSpec((1, D_MODEL), lambda t, nv, ca, cb, ea_, eb_, ps: (0, 0))
    gu = (1, 1, D_MODEL, D_EXPERT)
    dn = (1, 1, D_EXPERT, D_MODEL)
    grid_spec = pltpu.PrefetchScalarGridSpec(
        num_scalar_prefetch=6,
        grid=(n_tiles,),
        in_specs=[pl.BlockSpec(memory_space=pl.ANY), vec, vec,
                  wspec_a(gu), wspec_a(gu), wspec_a(dn),
                  wspec_b(gu), wspec_b(gu), wspec_b(dn)],
        out_specs=pl.BlockSpec(memory_space=pl.ANY),
        scratch_shapes=[
            pltpu.VMEM((tm, X1E_W), F32), pltpu.VMEM((tm, X1E_W), F32), pltpu.VMEM((tm, X1E_W), F32),
            pltpu.VMEM((tm, D_MODEL), F32), pltpu.VMEM((tm, D_MODEL), F32),
            pltpu.VMEM((tm, D_MODEL), F32),
            pltpu.VMEM((D_MODEL, D_EXPERT), MM), pltpu.VMEM((D_MODEL, D_EXPERT), MM),
            pltpu.VMEM((D_EXPERT, D_MODEL), MM),
            pltpu.VMEM((D_MODEL, D_EXPERT), MM), pltpu.VMEM((D_MODEL, D_EXPERT), MM),
            pltpu.VMEM((D_EXPERT, D_MODEL), MM),
            pltpu.SMEM((n_rows,), I32),
            pltpu.SemaphoreType.DMA((3,)),
            pltpu.SemaphoreType.DMA((3,)),
        ],
    )
    return pl.pallas_call(
        kern,
        grid_spec=grid_spec,
        out_shape=jax.ShapeDtypeStruct((n + 3 * tm, D_MODEL), F32),
        compiler_params=_cparams(("arbitrary",)),
        name="moe",
    )(nvalid, chga, chgb, ea, eb, pos,
      x1e, ln_g, ln_b, w_gate, w_up, w_down, w_gate, w_up, w_down)


def _pad_hist(hist, rows):
    b, r, c = hist.shape
    return jnp.concatenate([jnp.zeros((b, rows - r, c), hist.dtype), hist], axis=1)


def _new_hist(hist, cur, c0, c1, keep):
    t = cur.shape[1]
    if t >= keep:
        return cur[:, t - keep:, c0:c1]
    return jnp.concatenate([hist[:, t:], cur[:, :, c0:c1]], axis=1)


def kernel(x_prompt, x_sample, state_delta, state_delta_conv, state_pool, state_lru, state_lru_conv,
           ab_w_in, a_conv_w, a_log_decay, a_dt_bias, a_norm_w, b_w_group, b_scale, ab_w_out,
           c_w_in, c_conv_w, c_conv_b, c_w_r, c_b_r, c_w_i, c_b_i, c_lambda, c_w_out,
           ln_mix_g, ln_mix_b, ln_ffn_g, ln_ffn_b, router_w, router_bias,
           moe_w_gate, moe_w_up, moe_w_down):
    bp, tp, d = x_prompt.shape
    bs, ts, _ = x_sample.shape
    n_p, n_s = bp * tp, bs * ts
    n = n_p + n_s
    tm = TOKEN_TILE
    tmix = MIX_TILE
    assert n_p % tmix == 0 and n_s % tmix == 0 and tmix % tm == 0

    xin = [(x_prompt.reshape(n_p, d), 0), (x_sample.reshape(n_s, d), 0)]
    groups = (
        dict(batch=bp, seq=tp, hist_valid=0, fresh=True,
             ev=dict(nb_blk=1, rows=min(tp, 256)), od=dict(nb_blk=min(bp, 8), rows=min(tp, 64))),
        dict(batch=bs, seq=ts, hist_valid=B_HIST, fresh=False,
             ev=dict(nb_blk=min(bs, 8), rows=ts), od=dict(nb_blk=min(bs, 32), rows=ts)),
    )
    rwt = router_w.T
    rb = router_bias.reshape(N_EXPERTS, 1)
    row = lambda v: v.reshape(1, -1)

    new = {k: ([], []) for k in ("dconv", "pool", "lru", "lconv")}
    delta_new = [None, None]
    for layer in range(DEPTH):
        j = layer // 2
        mixed = []
        if layer % 2 == 0:
            w = ab_w_in[j]
            c1 = EV_Z0 + A_V
            w_perm = jnp.concatenate(
                [w[:, :c1], w[:, c1 + 2 * A_HEADS:], w[:, c1:c1 + 2 * A_HEADS],
                 jnp.zeros((d, LANES - 2 * A_HEADS), w.dtype)], axis=1).astype(MM)
            gvec = jnp.zeros((2, LANES), F32)
            gvec = gvec.at[0, A_HEADS:2 * A_HEADS].set(a_log_decay[j])
            gvec = gvec.at[1, A_HEADS:2 * A_HEADS].set(a_dt_bias[j])
            for gi, g in enumerate(groups):
                b_, t_ = g["batch"], g["seq"]
                proj = _proj(xin[gi][0], w_perm, xin[gi][1], b_ * t_, tmix)
                p3 = proj.reshape(b_, t_, EV_W)
                if g["fresh"]:
                    dconv = jnp.zeros((b_, 3, A_CONV_CH), F32)
                    delta_all = jnp.zeros((1, b_, A_HEADS, A_DK, A_DV), F32)
                    pool = jnp.zeros((b_, B_HIST, B_WIDTH), F32)
                else:
                    dconv, delta_all, pool = state_delta_conv[j], state_delta, state_pool[j]
                o, delta_new[gi] = _even_mixer(
                    proj, _pad_hist(dconv, SUBLANES), delta_all, j, _pad_hist(pool, 2 * SUBLANES),
                    a_conv_w[j], gvec, row(a_norm_w[j]), b_w_group[j].astype(MM), row(b_scale[j]),
                    delta_new[gi], batch=b_, seq=t_, hist_valid=g["hist_valid"], **g["ev"])
                mixed.append(o)
                new["dconv"][gi].append(_new_hist(dconv, p3, 0, A_CONV_CH, 3))
                new["pool"][gi].append(_new_hist(pool, p3, EV_U0, EV_U0 + B_WIDTH, B_HIST))
            w_out = ab_w_out[j].astype(MM)
        else:
            w_mm = c_w_in[j].astype(MM)
            for gi, g in enumerate(groups):
                b_, t_ = g["batch"], g["seq"]
                p3 = _proj(xin[gi][0], w_mm, xin[gi][1], b_ * t_, tmix).reshape(b_, t_, 2 * C_WIDTH)
                if g["fresh"]:
                    lconv = jnp.zeros((b_, 3, C_WIDTH), F32)
                    lru = jnp.zeros((b_, C_WIDTH), F32)
                else:
                    lconv, lru = state_lru_conv[j], state_lru[j]
                o3, h_last = _odd_mixer(
                    p3, _pad_hist(lconv, SUBLANES), lru, c_conv_w[j], row(c_conv_b[j]),
                    c_w_r[j].astype(MM), row(c_b_r[j]), c_w_i[j].astype(MM), row(c_b_i[j]),
                    row(c_lambda[j]), **g["od"])
                mixed.append(o3.reshape(b_ * t_, C_WIDTH))
                new["lru"][gi].append(h_last)
                new["lconv"][gi].append(_new_hist(lconv, p3, C_WIDTH, 2 * C_WIDTH, 3))
            w_out = c_w_out[j].astype(MM)

        x1e, key, cnt = _mix_out(mixed[0], mixed[1], w_out, xin[0][0], xin[0][1], xin[1][0], xin[1][1],
                                 row(ln_mix_g[layer]), row(ln_mix_b[layer]), rwt, rb, tmix)
        x = _moe(x1e, key, cnt, row(ln_ffn_g[layer]), row(ln_ffn_b[layer]),
                 moe_w_gate, moe_w_up, moe_w_down, layer, tm)
        xin = [(x, 0), (x, n_p)]

    stack = lambda key, gi: jnp.stack(new[key][gi])
    return (x[:n_p].reshape(bp, tp, d), x[n_p:n].reshape(bs, ts, d),
            delta_new[0], stack("dconv", 0), stack("pool", 0), stack("lru", 0), stack("lconv", 0),
            delta_new[1], stack("dconv", 1), stack("pool", 1), stack("lru", 1), stack("lconv", 1))
```

```python
import functools

import jax
import jax.numpy as jnp
from jax import lax
from jax.experimental import pallas as pl
from jax.experimental.pallas import tpu as pltpu

F32 = jnp.float32
I32 = jnp.int32
MM = jnp.bfloat16

D_MODEL = 1024
DEPTH = 4
N_A_LAYERS = (DEPTH + 1) // 2
A_HEADS = 4
A_DK = 128
A_DV = 128
A_QK = A_HEADS * A_DK
A_V = A_HEADS * A_DV
A_CONV_CH = 2 * A_QK + A_V
A_CHUNK = 64
B_GROUPS = 4
B_GW = 128
B_WINDOWS = (2, 4, 8, 16)
B_HIST = 15
B_WIDTH = B_GROUPS * B_GW
C_WIDTH = D_MODEL
C_HEADS = 4
C_HW = C_WIDTH // C_HEADS
C_GATE = 8.0
N_EXPERTS = 16
N_GROUPS = 4
EPG = N_EXPERTS // N_GROUPS
D_EXPERT = 512
ALPHA = (2 * DEPTH) ** 0.25
LN_EPS = 1e-5
RMS_EPS = 1e-6

LANES = 128
SUBLANES = 8
VMEM_LIMIT = 56 * 1024 * 1024

EV_Z0 = A_CONV_CH
EV_U0 = EV_Z0 + A_V
EV_BA0 = EV_U0 + B_WIDTH
EV_W = EV_BA0 + LANES

PAIRS = ((0, 1), (0, 2), (0, 3), (1, 2), (1, 3), (2, 3))
N_CLASSES = N_GROUPS * len(PAIRS)

TOKEN_TILE = 256
MIX_TILE = 512
CLS_ROWS = 32
KEY_CLS = 1 << 16
X1E_W = D_MODEL + LANES


def _cparams(sem):
    return pltpu.CompilerParams(dimension_semantics=sem, vmem_limit_bytes=VMEM_LIMIT)


def _dot(a, b):
    return jnp.dot(a.astype(MM), b.astype(MM), preferred_element_type=F32)


def _dot_nt(a, b):
    return lax.dot_general(a.astype(MM), b.astype(MM), (((1,), (1,)), ((), ())),
                           preferred_element_type=F32)


def _dot_tn(a, b):
    return lax.dot_general(a.astype(MM), b.astype(MM), (((0,), (0,)), ((), ())),
                           preferred_element_type=F32)


def _split3(v):
    hi = v.astype(MM)
    r1 = v - hi.astype(F32)
    mid = r1.astype(MM)
    lo = (r1 - mid.astype(F32)).astype(MM)
    return hi, mid, lo


def _sigmoid(x):
    return 1.0 / (1.0 + jnp.exp(-x))


def _sigmoid_t(x):
    return 0.5 * jnp.tanh(0.5 * x) + 0.5


def _silu(x):
    return x * _sigmoid_t(x)


def _softplus(x):
    return jnp.maximum(x, 0.0) + jnp.log1p(jnp.exp(-jnp.abs(x)))


def _gelu_tanh(x):
    return x * (0.5 * (1.0 + jnp.tanh(0.7978845608028654 * (x + 0.044715 * (x * x * x)))))


def _layer_norm(v, g, b):
    mu = jnp.mean(v, axis=-1, keepdims=True)
    d = v - mu
    var = jnp.mean(d * d, axis=-1, keepdims=True)
    return d * lax.rsqrt(var + LN_EPS) * g + b


def _proj_kernel(x_ref, w_ref, o_ref):
    o_ref[...] = jnp.dot(x_ref[...].astype(MM), w_ref[...], preferred_element_type=F32)


def _proj(x, w_mm, row0, nrows, tm):
    k, width = w_mm.shape
    off = row0 // tm
    return pl.pallas_call(
        _proj_kernel,
        grid=(nrows // tm,),
        in_specs=[pl.BlockSpec((tm, k), lambda i: (i + off, 0)),
                  pl.BlockSpec((k, width), lambda i: (0, 0))],
        out_specs=pl.BlockSpec((tm, width), lambda i: (i, 0)),
        out_shape=jax.ShapeDtypeStruct((nrows, width), F32),
        compiler_params=_cparams(("parallel",)),
        name="proj",
    )(x, w_mm)


def _even_kernel(*refs, nb_blk, rows, chunk, hist_valid, carry_hist, has_prev):
    (p_ref, convh_ref, s0_ref, poolh_ref, convw_ref, gvec_ref, normw_ref,
     wgrp_ref, scale_ref) = refs[:9]
    o_ref, snew_ref, ext_scr, qkv_scr, pext_scr, s_scr = refs[9 + (1 if has_prev else 0):]
    i = pl.program_id(1)
    n_chunks = rows // chunk
    n_neumann = max((chunk - 1).bit_length() - 1, 0)
    hb = 2 * SUBLANES

    @pl.when(i == 0)
    def _():
        s_scr[...] = s0_ref[0]
        ext_scr[:, 0:SUBLANES, :] = convh_ref[...]
        pext_scr[:, 0:hb, :] = poolh_ref[...]

    rid = lax.broadcasted_iota(I32, (chunk, chunk), 0)
    cid = lax.broadcasted_iota(I32, (chunk, chunk), 1)
    incl = rid >= cid
    strict = rid > cid
    eye = (rid == cid).astype(F32)
    ltri = incl.astype(F32)
    lane = lax.broadcasted_iota(I32, (chunk, LANES), 1)
    trow = lax.broadcasted_iota(I32, (rows, LANES), 0)
    pos = (i * rows + trow + (1 + hist_valid)).astype(F32)
    neg_decay_rate = -jnp.exp(gvec_ref[0:1, :])
    dt_bias = gvec_ref[1:2, :]

    for nb in range(nb_blk):
        r0 = nb * rows
        ext_scr[nb, SUBLANES:SUBLANES + rows, :] = p_ref[r0:r0 + rows, 0:A_CONV_CH]
        for ct in range(A_CONV_CH // LANES):
            cs = slice(ct * LANES, (ct + 1) * LANES)
            acc = None
            for j in range(4):
                lo = SUBLANES - 3 + j
                term = ext_scr[nb, lo:lo + rows, cs] * convw_ref[j:j + 1, cs]
                acc = term if acc is None else acc + term
            t = _silu(acc)
            if ct < 2 * A_HEADS:
                t = t * lax.rsqrt(jnp.sum(t * t, axis=-1, keepdims=True) + 1e-6)
                if ct < A_HEADS:
                    t = t * (A_DK ** -0.5)
            qkv_scr[r0:r0 + rows, cs] = t
        if carry_hist:
            ext_scr[nb, 0:SUBLANES, :] = ext_scr[nb, rows:rows + SUBLANES, :]

        pext_scr[nb, hb:hb + rows, :] = p_ref[r0:r0 + rows, EV_U0:EV_U0 + B_WIDTH]
        for gi, w in enumerate(B_WINDOWS):
            cs = slice(gi * B_GW, (gi + 1) * B_GW)
            cur = pext_scr[nb, hb:hb + rows, cs]
            tot = cur
            for j in range(1, w):
                tot = tot + pext_scr[nb, hb - j:hb - j + rows, cs]
            pooled = tot / jnp.minimum(pos, float(w)) - cur
            ob = _dot(pooled, wgrp_ref[gi]) * scale_ref[:, cs]
            o_ref[r0:r0 + rows, A_V + gi * B_GW:A_V + (gi + 1) * B_GW] = ob
        if carry_hist:
            pext_scr[nb, 0:hb, :] = pext_scr[nb, rows:rows + hb, :]

    chunks = [(nb, c) for nb in range(nb_blk) for c in range(n_chunks)]
    probs = [(nb, c, h) for nb, c in chunks for h in range(A_HEADS)]
    rs = {(nb, c): slice(nb * rows + c * chunk, nb * rows + (c + 1) * chunk) for nb, c in chunks}
    gcol = lambda arr, h: arr[:, A_HEADS + h:A_HEADS + h + 1]
    qf = lambda p: qkv_scr[rs[p[:2]], p[2] * LANES:(p[2] + 1) * LANES]
    kf = lambda p: qkv_scr[rs[p[:2]], A_QK + p[2] * LANES:A_QK + (p[2] + 1) * LANES]
    vf = lambda p: qkv_scr[rs[p[:2]], 2 * A_QK + p[2] * LANES:2 * A_QK + (p[2] + 1) * LANES]

    ba = {ck: p_ref[rs[ck], EV_BA0:EV_BA0 + LANES] for ck in chunks}
    beta_full = {ck: _sigmoid(ba[ck]) for ck in chunks}
    g_full = {ck: neg_decay_rate * _softplus(ba[ck] + dt_bias) for ck in chunks}
    g_parts = {ck: _split3(g_full[ck]) for ck in chunks}
    gc_full = {ck: sum(jnp.dot(ltri.astype(MM), part, preferred_element_type=F32)
                       for part in g_parts[ck]) for ck in chunks}
    eg_full = {ck: jnp.exp(gc_full[ck]) for ck in chunks}
    gl_full = {ck: gc_full[ck][chunk - 1:chunk, :] for ck in chunks}
    ekd_full = {ck: jnp.exp(gl_full[ck] - gc_full[ck]) for ck in chunks}
    egl_full = {ck: jnp.exp(gl_full[ck]) for ck in chunks}
    sel = [(lane == A_HEADS + h).astype(MM) for h in range(A_HEADS)]
    gc_parts = {ck: _split3(gc_full[ck]) for ck in chunks}
    grow = {p: sum(lax.dot_general(sel[p[2]], part, (((1,), (1,)), ((), ())),
                                   preferred_element_type=F32) for part in gc_parts[p[:2]])
            for p in probs}
    decay = {p: jnp.where(incl, jnp.exp(jnp.where(incl, gcol(gc_full[p[:2]], p[2]) - grow[p], 0.0)), 0.0)
             for p in probs}
    beta = {p: beta_full[p[:2]][:, p[2]:p[2] + 1] for p in probs}
    kq = {p: _dot_nt(jnp.concatenate([kf(p) * beta[p], qf(p)], axis=0), kf(p)) for p in probs}
    a_intra = {p: jnp.where(incl, kq[p][chunk:2 * chunk] * decay[p], 0.0) for p in probs}
    power = {p: -jnp.where(strict, kq[p][0:chunk] * decay[p], 0.0) for p in probs}
    tinv = {p: eye + power[p] for p in probs}
    for _ in range(n_neumann):
        power = {p: _dot(power[p], power[p]) for p in probs}
        tinv = {p: tinv[p] + _dot(tinv[p], power[p]) for p in probs}
    uw = {p: _dot(tinv[p], jnp.concatenate(
        [vf(p) * beta[p], kf(p) * beta[p] * gcol(eg_full[p[:2]], p[2])], axis=1)) for p in probs}
    u_in = {p: uw[p][:, 0:A_DV] for p in probs}
    w_in = {p: uw[p][:, A_DV:A_DV + A_DK] for p in probs}

    for c in range(n_chunks):
        cp = [(nb, c, h) for nb in range(nb_blk) for h in range(A_HEADS)]
        s_old = {p: s_scr[p[0], p[2]] for p in cp}
        wq = {p: _dot(jnp.concatenate([w_in[p], qf(p) * gcol(eg_full[p[:2]], p[2])], axis=0), s_old[p])
              for p in cp}
        u_new = {p: u_in[p] - wq[p][0:chunk] for p in cp}
        au = {p: _dot(a_intra[p], u_new[p]) for p in cp}
        ku = {p: _dot_tn(kf(p) * gcol(ekd_full[p[:2]], p[2]), u_new[p]) for p in cp}
        for p in cp:
            nb, _, h = p
            s_scr[nb, h] = s_old[p] * gcol(egl_full[p[:2]], h) + ku[p]
            o = wq[p][chunk:2 * chunk] + au[p]
            o = o * lax.rsqrt(jnp.mean(o * o, axis=-1, keepdims=True) + RMS_EPS) * normw_ref[...]
            z = p_ref[rs[p[:2]], EV_Z0 + h * LANES:EV_Z0 + (h + 1) * LANES]
            o_ref[rs[p[:2]], h * LANES:(h + 1) * LANES] = o * _silu(z)

    @pl.when(i == pl.num_programs(1) - 1)
    def _():
        snew_ref[0] = s_scr[...]
        if not has_prev:
            for other in range(1, snew_ref.shape[0]):
                snew_ref[other] = jnp.zeros(s_scr.shape, F32)


def _even_mixer(proj, convh8, s0_all, layer_j, poolh16, convw, gvec, normw, wgrp_mm, scale, snew_prev,
                *, batch, seq, nb_blk, rows, hist_valid):
    chunk = min(A_CHUNK, seq)
    assert seq % rows == 0 and rows % chunk == 0 and batch % nb_blk == 0
    assert nb_blk == 1 or rows == seq
    t_blocks = seq // rows
    blk = nb_blk * rows
    hb = 2 * SUBLANES
    has_prev = snew_prev is not None
    slot = layer_j if s0_all.shape[0] > 1 else 0
    kern = functools.partial(_even_kernel, nb_blk=nb_blk, rows=rows, chunk=chunk,
                             hist_valid=hist_valid, carry_hist=t_blocks > 1, has_prev=has_prev)
    full = lambda shape: pl.BlockSpec(shape, lambda b, i: (0,) * len(shape))
    state_blk = (1, nb_blk, A_HEADS, A_DK, A_DV)
    in_specs = [
        pl.BlockSpec((blk, EV_W), lambda b, i: (b * t_blocks + i, 0)),
        pl.BlockSpec((nb_blk, SUBLANES, A_CONV_CH), lambda b, i: (b, 0, 0)),
        pl.BlockSpec(state_blk, lambda b, i: (slot, b, 0, 0, 0)),
        pl.BlockSpec((nb_blk, hb, B_WIDTH), lambda b, i: (b, 0, 0)),
        full((4, A_CONV_CH)), full((2, LANES)), full((1, A_DV)),
        full((B_GROUPS, B_GW, B_GW)), full((1, B_WIDTH)),
    ]
    args = [proj, convh8, s0_all, poolh16, convw, gvec, normw, wgrp_mm, scale]
    if has_prev:
        in_specs.append(pl.BlockSpec(memory_space=pl.ANY))
        args.append(snew_prev)
        snew_spec = pl.BlockSpec(state_blk, lambda b, i: (layer_j, b, 0, 0, 0))
    else:
        snew_spec = pl.BlockSpec((N_A_LAYERS,) + state_blk[1:], lambda b, i: (0, b, 0, 0, 0))
    return pl.pallas_call(
        kern,
        grid=(batch // nb_blk, t_blocks),
        in_specs=in_specs,
        out_specs=[
            pl.BlockSpec((blk, D_MODEL), lambda b, i: (b * t_blocks + i, 0)),
            snew_spec,
        ],
        out_shape=[jax.ShapeDtypeStruct((batch * seq, D_MODEL), F32),
                   jax.ShapeDtypeStruct((N_A_LAYERS, batch, A_HEADS, A_DK, A_DV), F32)],
        scratch_shapes=[
            pltpu.VMEM((nb_blk, SUBLANES + rows, A_CONV_CH), F32),
            pltpu.VMEM((blk, A_CONV_CH), F32),
            pltpu.VMEM((nb_blk, hb + rows, B_WIDTH), F32),
            pltpu.VMEM((nb_blk, A_HEADS, A_DK, A_DV), F32),
        ],
        input_output_aliases={len(args) - 1: 1} if has_prev else {},
        compiler_params=_cparams(("parallel", "arbitrary")),
        name="even_mixer",
    )(*args)


def _odd_kernel(p_ref, convh_ref, h0_ref, convw_ref, convb_ref, wr_ref, br_ref, wi_ref, bi_ref,
                lam_ref, o_ref, hlast_ref, ext_scr, a_scr, b_scr, hs_scr, h_scr,
                *, nb_blk, rows, carry_hist):
    i = pl.program_id(1)
    blk = nb_blk * rows
    pitch = a_scr.shape[1] // nb_blk

    @pl.when(i == 0)
    def _():
        h_scr[...] = h0_ref[...]
        ext_scr[:, 0:SUBLANES, :] = convh_ref[...]

    ext_scr[:, SUBLANES:SUBLANES + rows, :] = p_ref[:, :, C_WIDTH:2 * C_WIDTH]
    log_base = -C_GATE * _softplus(-lam_ref[...])
    for h in range(C_HEADS):
        cs = slice(h * C_HW, (h + 1) * C_HW)
        acc = None
        for j in range(4):
            lo = SUBLANES - 3 + j
            term = ext_scr[:, lo:lo + rows, cs] * convw_ref[j:j + 1, cs]
            acc = term if acc is None else acc + term
        xc = (acc + convb_ref[:, cs]).reshape(blk, C_HW)
        r = _sigmoid_t(_dot(xc, wr_ref[h]) + br_ref[:, cs])
        gi = _sigmoid_t(_dot(xc, wi_ref[h]) + bi_ref[:, cs])
        log_a = r * log_base[:, cs]
        a = jnp.exp(log_a)
        th = jnp.tanh(log_a)
        bt = jnp.sqrt(-2.0 * th) * lax.rsqrt(1.0 - th) * gi * xc
        for lt in range(C_HW // LANES):
            for b in range(nb_blk):
                dst = pl.ds(b * pitch, rows)
                src = slice(b * rows, (b + 1) * rows)
                a_scr[h * (C_HW // LANES) + lt, dst, :] = a[src, lt * LANES:(lt + 1) * LANES]
                b_scr[h * (C_HW // LANES) + lt, dst, :] = bt[src, lt * LANES:(lt + 1) * LANES]
    if carry_hist:
        ext_scr[:, 0:SUBLANES, :] = ext_scr[:, rows:rows + SUBLANES, :]

    n_lt = C_WIDTH // LANES

    def step(t, hcur):
        hnew = []
        for lt in range(n_lt):
            a_t = a_scr[lt, pl.ds(t, nb_blk, stride=pitch), :]
            b_t = b_scr[lt, pl.ds(t, nb_blk, stride=pitch), :]
            hn = a_t * hcur[lt] + b_t
            hs_scr[lt, pl.ds(t, nb_blk, stride=pitch), :] = hn
            hnew.append(hn)
        return tuple(hnew)

    h_init = tuple(h_scr[:, lt * LANES:(lt + 1) * LANES] for lt in range(n_lt))
    h_fin = lax.fori_loop(0, rows, step, h_init, unroll=SUBLANES)
    for lt in range(n_lt):
        cs = slice(lt * LANES, (lt + 1) * LANES)
        h_scr[:, cs] = h_fin[lt]
        for b in range(nb_blk):
            o_ref[b, :, cs] = _gelu_tanh(p_ref[b, :, cs]) * hs_scr[lt, pl.ds(b * pitch, rows), :]

    @pl.when(i == pl.num_programs(1) - 1)
    def _():
        hlast_ref[...] = h_scr[...]


def _odd_mixer(proj3, convh8, h0, convw, convb, wr_mm, br, wi_mm, bi, lam, *, nb_blk, rows):
    batch, seq, _ = proj3.shape
    assert seq % rows == 0 and batch % nb_blk == 0 and rows % SUBLANES == 0
    t_blocks = seq // rows
    pitch = rows + SUBLANES
    kern = functools.partial(_odd_kernel, nb_blk=nb_blk, rows=rows, carry_hist=t_blocks > 1)
    full = lambda shape: pl.BlockSpec(shape, lambda b, i: (0,) * len(shape))
    return pl.pallas_call(
        kern,
        grid=(batch // nb_blk, t_blocks),
        in_specs=[
            pl.BlockSpec((nb_blk, rows, 2 * C_WIDTH), lambda b, i: (b, i, 0)),
            pl.BlockSpec((nb_blk, SUBLANES, C_WIDTH), lambda b, i: (b, 0, 0)),
            pl.BlockSpec((nb_blk, C_WIDTH), lambda b, i: (b, 0)),
            full((4, C_WIDTH)), full((1, C_WIDTH)),
            full((C_HEADS, C_HW, C_HW)), full((1, C_WIDTH)),
            full((C_HEADS, C_HW, C_HW)), full((1, C_WIDTH)),
            full((1, C_WIDTH)),
        ],
        out_specs=[
            pl.BlockSpec((nb_blk, rows, C_WIDTH), lambda b, i: (b, i, 0)),
            pl.BlockSpec((nb_blk, C_WIDTH), lambda b, i: (b, 0)),
        ],
        out_shape=[jax.ShapeDtypeStruct((batch, seq, C_WIDTH), F32),
                   jax.ShapeDtypeStruct((batch, C_WIDTH), F32)],
        scratch_shapes=[
            pltpu.VMEM((nb_blk, SUBLANES + rows, C_WIDTH), F32),
            pltpu.VMEM((C_WIDTH // LANES, nb_blk * pitch, LANES), F32),
            pltpu.VMEM((C_WIDTH // LANES, nb_blk * pitch, LANES), F32),
            pltpu.VMEM((C_WIDTH // LANES, nb_blk * pitch, LANES), F32),
            pltpu.VMEM((nb_blk, C_WIDTH), F32),
        ],
        compiler_params=_cparams(("parallel", "arbitrary")),
        name="odd_mixer",
    )(proj3, convh8, h0, convw, convb, wr_mm, br, wi_mm, bi, lam)


def _route(x1, rwt_ref, rb_ref):
    def split(v):
        hi = v.astype(MM)
        return hi, (v - hi.astype(F32)).astype(MM)

    nt = lambda a, b: lax.dot_general(a, b, (((1,), (1,)), ((), ())), preferred_element_type=F32)
    r_hi, r_lo = split(rwt_ref[...])
    x_hi, x_lo = split(x1)
    logits = nt(r_hi, x_hi) + nt(r_hi, x_lo) + nt(r_lo, x_hi)
    sc = _sigmoid(logits)
    bz = sc + rb_ref[...]
    row = lambda arr, e: arr[e:e + 1, :]
    best = None
    gidx = None
    for g in range(N_GROUPS):
        r = [row(bz, EPG * g + k) for k in range(EPG)]
        top2 = None
        for a, b in PAIRS:
            s = r[a] + r[b]
            top2 = s if top2 is None else jnp.maximum(top2, s)
        if best is None:
            best, gidx = top2, jnp.zeros(top2.shape, I32)
        else:
            upd = top2 > best
            gidx = jnp.where(upd, g, gidx)
            best = jnp.where(upd, top2, best)
    sb, ss = [], []
    for k in range(EPG):
        vb, vs = row(bz, k), row(sc, k)
        for g in range(1, N_GROUPS):
            vb = jnp.where(gidx == g, row(bz, EPG * g + k), vb)
            vs = jnp.where(gidx == g, row(sc, EPG * g + k), vs)
        sb.append(vb)
        ss.append(vs)
    m1, i1 = sb[0], jnp.zeros(sb[0].shape, I32)
    for k in range(1, EPG):
        upd = sb[k] > m1
        i1 = jnp.where(upd, k, i1)
        m1 = jnp.where(upd, sb[k], m1)
    m2, i2 = None, None
    for k in range(EPG):
        cand = jnp.where(i1 == k, -jnp.inf, sb[k])
        if m2 is None:
            m2, i2 = cand, jnp.zeros(cand.shape, I32)
        else:
            upd = cand > m2
            i2 = jnp.where(upd, k, i2)
            m2 = jnp.where(upd, cand, m2)
    lo = jnp.minimum(i1, i2)
    hi = jnp.maximum(i1, i2)
    pair = jnp.where(lo == 0, 0, jnp.where(lo == 1, 3, 5)) + hi - lo - 1
    s_lo, s_hi = ss[0], ss[0]
    for k in range(1, EPG):
        s_lo = jnp.where(lo == k, ss[k], s_lo)
        s_hi = jnp.where(hi == k, ss[k], s_hi)
    den = s_lo + s_hi
    return gidx * len(PAIRS) + pair, s_lo / den, s_hi / den


def _mix_out_kernel(ap_ref, as_ref, w_ref, xp_ref, xs_ref, g_ref, b_ref, rwt_ref, rb_ref, upper_ref,
                    x1e_ref, key_ref, cnt_ref, run_scr, *, np_tiles):
    i = pl.program_id(0)
    tm = xp_ref.shape[0]

    @pl.when(i == 0)
    def _():
        run_scr[...] = jnp.zeros(run_scr.shape, F32)

    def finish(a_ref, x_ref):
        hmix = jnp.dot(a_ref[...].astype(MM), w_ref[...], preferred_element_type=F32)
        x1 = _layer_norm(ALPHA * x_ref[...] + hmix, g_ref[...], b_ref[...])
        cls, wa, wb = _route(x1, rwt_ref, rb_ref)
        crow = lax.broadcasted_iota(I32, (CLS_ROWS, tm), 0)
        onehot = (crow == cls).astype(F32)
        prefix = jnp.dot(onehot.astype(MM), upper_ref[...], preferred_element_type=F32)
        run = run_scr[:, 0:1]
        rank = jnp.sum(onehot * (prefix + (run - 1.0)), axis=0, keepdims=True)
        run_new = run + jnp.sum(onehot, axis=1, keepdims=True)
        run_scr[...] = jnp.broadcast_to(run_new, run_scr.shape)
        cnt_ref[...] = jnp.broadcast_to(run_new, cnt_ref.shape).astype(I32)
        key_ref[...] = cls * KEY_CLS + rank.astype(I32)
        wrow = lax.broadcasted_iota(I32, (LANES, tm), 0)
        wpad = jnp.where(wrow == 0, wa, jnp.where(wrow == 1, wb, 0.0))
        x1e_ref[:, 0:D_MODEL] = x1
        x1e_ref[:, D_MODEL:D_MODEL + LANES] = wpad.T

    @pl.when(i < np_tiles)
    def _():
        finish(ap_ref, xp_ref)

    @pl.when(i >= np_tiles)
    def _():
        finish(as_ref, xs_ref)


def _mix_out(a_p, a_s, w_mm, x_p, row0_p, x_s, row0_s, g, b, rwt, rb, tm):
    n = a_p.shape[0] + a_s.shape[0]
    np_tiles = a_p.shape[0] // tm
    ns_tiles = a_s.shape[0] // tm
    assert (np_tiles + ns_tiles) * tm == n and row0_p % tm == 0 and row0_s % tm == 0
    off_p, off_s = row0_p // tm, row0_s // tm
    kern = functools.partial(_mix_out_kernel, np_tiles=np_tiles)
    full = lambda shape: pl.BlockSpec(shape, lambda i: (0,) * len(shape))
    upper = (jnp.arange(tm)[:, None] <= jnp.arange(tm)[None, :]).astype(MM)
    return pl.pallas_call(
        kern,
        grid=(np_tiles + ns_tiles,),
        in_specs=[
            pl.BlockSpec((tm, D_MODEL), lambda i: (jnp.minimum(i, np_tiles - 1), 0)),
            pl.BlockSpec((tm, D_MODEL), lambda i: (jnp.maximum(i - np_tiles, 0), 0)),
            full((D_MODEL, D_MODEL)),
            pl.BlockSpec((tm, D_MODEL), lambda i: (jnp.minimum(i, np_tiles - 1) + off_p, 0)),
            pl.BlockSpec((tm, D_MODEL), lambda i: (jnp.maximum(i - np_tiles, 0) + off_s, 0)),
            full((1, D_MODEL)), full((1, D_MODEL)),
            full((N_EXPERTS, D_MODEL)), full((N_EXPERTS, 1)),
            full((tm, tm)),
        ],
        out_specs=[
            pl.BlockSpec((tm, X1E_W), lambda i: (i, 0)),
            pl.BlockSpec((1, tm), lambda i: (0, i)),
            full((CLS_ROWS, LANES)),
        ],
        out_shape=[jax.ShapeDtypeStruct((n, X1E_W), F32),
                   jax.ShapeDtypeStruct((1, n), I32),
                   jax.ShapeDtypeStruct((CLS_ROWS, LANES), I32)],
        scratch_shapes=[pltpu.VMEM((CLS_ROWS, LANES), F32)],
        compiler_params=_cparams(("arbitrary",)),
        name="mix_out",
    )(a_p, a_s, w_mm, x_p, x_s, g, b, rwt, rb, upper)


def _moe_kernel(nvalid_ref, chga_ref, chgb_ref, ea_ref, eb_ref, pos_ref,
                x_hbm, g_ref, b_ref, wga_ref, wua_ref, wda_ref, wgb_ref, wub_ref, wdb_ref,
                y_hbm, xbuf0, xbuf1, xbuf2, obuf0, obuf1, obuf2,
                ga_scr, ua_scr, da_scr, gb_scr, ub_scr, db_scr, rowsrc, gsem, ssem, *, tm, n_tok):
    t = pl.program_id(0)
    nt = pl.num_programs(0)
    xbufs, obufs = (xbuf0, xbuf1, xbuf2), (obuf0, obuf1, obuf2)

    def gather_copy(tok, r, sl):
        return pltpu.make_async_copy(x_hbm.at[pl.ds(tok, 1)], xbufs[sl].at[pl.ds(r, 1)], gsem.at[sl])

    def scatter_copy(tok, r, sl):
        return pltpu.make_async_copy(obufs[sl].at[pl.ds(r, 1)], y_hbm.at[pl.ds(tok, 1)], ssem.at[sl])

    def src_token(tile, n_tile, r):
        return rowsrc[jnp.where(r < n_tile, tile * tm + r, 0)]

    def tile_rows(tile):
        return jnp.where(tile < nt, nvalid_ref[jnp.minimum(tile, nt - 1)], 0)

    def issue_gather(tile, n_tile, sl, r, priority=0):
        gather_copy(src_token(tile, n_tile, r), r, sl).start(priority=priority)

    def issue_scatter(tile, n_tile, sl, r, priority=0):
        dst = jnp.where(r < n_tile, src_token(tile, n_tile, r), n_tok + sl * tm + r)
        scatter_copy(dst, r, sl).start(priority=priority)

    def wait_gather(sl):
        pltpu.make_async_copy(x_hbm.at[pl.ds(0, tm)], xbufs[sl], gsem.at[sl]).wait()

    def wait_scatter(sl):
        pltpu.make_async_copy(obufs[sl], y_hbm.at[pl.ds(0, tm)], ssem.at[sl]).wait()

    @pl.when(t == 0)
    def _():
        def build(i, carry):
            rowsrc[pos_ref[i]] = i
            return carry
        lax.fori_loop(0, n_tok, build, 0, unroll=16)

        def body(r, carry):
            issue_gather(0, tile_rows(0), 0, r)
            issue_gather(1, tile_rows(1), 1, r)
            return carry
        lax.fori_loop(0, tm, body, 0, unroll=8)
        for sl in range(3):
            obufs[sl][...] = jnp.zeros(obufs[sl].shape, obufs[sl].dtype)
        for sl in range(2):
            pltpu.make_async_copy(obufs[sl], y_hbm.at[pl.ds(n_tok + sl * tm, tm)], ssem.at[sl]).start()

    @pl.when(chga_ref[t] == 1)
    def _():
        ga_scr[...] = wga_ref[0, 0].astype(MM)
        ua_scr[...] = wua_ref[0, 0].astype(MM)
        da_scr[...] = wda_ref[0, 0].astype(MM)

    @pl.when(chgb_ref[t] == 1)
    def _():
        gb_scr[...] = wgb_ref[0, 0].astype(MM)
        ub_scr[...] = wub_ref[0, 0].astype(MM)
        db_scr[...] = wdb_ref[0, 0].astype(MM)

    prev = jnp.maximum(t - 1, 0)
    n_prev = jnp.where(t > 0, nvalid_ref[prev], 0)
    n_cur = nvalid_ref[t]

    n_ahead = tile_rows(t + 2)

    def tile_block(sl):
        other = (sl + 2) % 3
        wait_gather(sl)
        wait_scatter(sl)
        x1 = xbufs[sl][:, 0:D_MODEL]
        x = x1.astype(MM)

        def expert(wg, wu, wd, wrow):
            gate = jnp.dot(x, wg[...], preferred_element_type=F32)
            up = jnp.dot(x, wu[...], preferred_element_type=F32)
            hid = (_silu(gate) * up).astype(MM)
            return jnp.dot(hid, wd[...], preferred_element_type=F32) * wrow

        f = (expert(ga_scr, ua_scr, da_scr, xbufs[sl][:, D_MODEL:D_MODEL + 1])
             + expert(gb_scr, ub_scr, db_scr, xbufs[sl][:, D_MODEL + 1:D_MODEL + 2]))
        obufs[sl][...] = _layer_norm(ALPHA * x1 + f, g_ref[...], b_ref[...])
        for r in range(tm):
            issue_gather(t + 2, n_ahead, other, r, priority=r % 2)
            issue_scatter(prev, n_prev, other, r, priority=r % 2)

    def drain(sl):
        other = (sl + 2) % 3
        wait_gather(sl)
        wait_gather((sl + 1) % 3)

        def body(r, carry):
            issue_scatter(prev, n_prev, other, r)
            return carry
        lax.fori_loop(0, tm, body, 0, unroll=8)
        for k in range(3):
            wait_scatter(k)

    for sl in range(3):
        pl.when(jnp.logical_and(n_cur > 0, t % 3 == sl))(functools.partial(tile_block, sl))
        pl.when(jnp.logical_and(jnp.logical_and(n_cur == 0, n_prev > 0), t % 3 == sl))(
            functools.partial(drain, sl))


def _moe(x1e, key, cnt, ln_g, ln_b, w_gate, w_up, w_down, layer, tm):
    n = x1e.shape[0]
    n_tiles = n // tm + N_CLASSES
    n_rows = n_tiles * tm
    counts = cnt[:N_CLASSES, 0]
    tiles_c = (counts + tm - 1) // tm
    tile_end_c = jnp.cumsum(tiles_c)
    tile_start_c = tile_end_c - tiles_c
    total = tile_end_c[-1]
    tid = jnp.arange(n_tiles, dtype=I32)
    t_eff = jnp.minimum(tid, total - 1)
    tile_cls = jnp.minimum(jnp.sum((tile_end_c[None, :] <= t_eff[:, None]).astype(I32), axis=1),
                           N_CLASSES - 1)
    sel = (tile_cls[:, None] == jnp.arange(N_CLASSES, dtype=I32)[None, :]).astype(I32)
    left = jnp.sum(sel * counts[None, :], axis=1) - (tid - jnp.sum(sel * tile_start_c[None, :], axis=1)) * tm
    nvalid = jnp.where(tid < total, jnp.clip(left, 0, tm), 0).astype(I32)
    pair_id = tile_cls % len(PAIRS)
    pair_lo = sum(jnp.where(pair_id == k, p[0], 0) for k, p in enumerate(PAIRS))
    pair_hi = sum(jnp.where(pair_id == k, p[1], 0) for k, p in enumerate(PAIRS))
    ea = (EPG * (tile_cls // len(PAIRS)) + pair_lo).astype(I32)
    eb = (EPG * (tile_cls // len(PAIRS)) + pair_hi).astype(I32)
    first = tid == 0
    chga = jnp.logical_or(first, ea != jnp.roll(ea, 1)).astype(I32)
    chgb = jnp.logical_or(first, eb != jnp.roll(eb, 1)).astype(I32)
    key = key.reshape(n)
    tok_cls = key // KEY_CLS
    tok_sel = (tok_cls[:, None] == jnp.arange(N_CLASSES, dtype=I32)[None, :]).astype(I32)
    pos = (jnp.sum(tok_sel * tile_start_c[None, :], axis=1) * tm + key % KEY_CLS).astype(I32)

    kern = functools.partial(_moe_kernel, tm=tm, n_tok=n)
    wspec_a = lambda shape: pl.BlockSpec(
        shape, lambda t, nv, ca, cb, ea_, eb_, ps: (layer, ea_[t], 0, 0))
    wspec_b = lambda shape: pl.BlockSpec(
        shape, lambda t, nv, ca, cb, ea_, eb_, ps: (layer, eb_[t], 0, 0))
    vec = pl.BlockSpec((1, D_MODEL), lambda t, nv, ca, cb, ea_, eb_, ps: (0, 0))
    gu = (1, 1, D_MODEL, D_EXPERT)
    dn = (1, 1, D_EXPERT, D_MODEL)
    grid_spec = pltpu.PrefetchScalarGridSpec(
        num_scalar_prefetch=6,
        grid=(n_tiles,),
        in_specs=[pl.BlockSpec(memory_space=pl.ANY), vec, vec,
                  wspec_a(gu), wspec_a(gu), wspec_a(dn),
                  wspec_b(gu), wspec_b(gu), wspec_b(dn)],
        out_specs=pl.BlockSpec(memory_space=pl.ANY),
        scratch_shapes=[
            pltpu.VMEM((tm, X1E_W), F32), pltpu.VMEM((tm, X1E_W), F32), pltpu.VMEM((tm, X1E_W), F32),
            pltpu.VMEM((tm, D_MODEL), F32), pltpu.VMEM((tm, D_MODEL), F32),
            pltpu.VMEM((tm, D_MODEL), F32),
            pltpu.VMEM((D_MODEL, D_EXPERT), MM), pltpu.VMEM((D_MODEL, D_EXPERT), MM),
            pltpu.VMEM((D_EXPERT, D_MODEL), MM),
            pltpu.VMEM((D_MODEL, D_EXPERT), MM), pltpu.VMEM((D_MODEL, D_EXPERT), MM),
            pltpu.VMEM((D_EXPERT, D_MODEL), MM),
            pltpu.SMEM((n_rows,), I32),
            pltpu.SemaphoreType.DMA((3,)),
            pltpu.SemaphoreType.DMA((3,)),
        ],
    )
    return pl.pallas_call(
        kern,
        grid_spec=grid_spec,
        out_shape=jax.ShapeDtypeStruct((n + 3 * tm, D_MODEL), F32),
        compiler_params=_cparams(("arbitrary",)),
        name="moe",
    )(nvalid, chga, chgb, ea, eb, pos,
      x1e, ln_g, ln_b, w_gate, w_up, w_down, w_gate, w_up, w_down)


def _pad_hist(hist, rows):
    b, r, c = hist.shape
    return jnp.concatenate([jnp.zeros((b, rows - r, c), hist.dtype), hist], axis=1)


def _new_hist(hist, cur, c0, c1, keep):
    t = cur.shape[1]
    if t >= keep:
        return cur[:, t - keep:, c0:c1]
    return jnp.concatenate([hist[:, t:], cur[:, :, c0:c1]], axis=1)


def kernel(x_prompt, x_sample, state_delta, state_delta_conv, state_pool, state_lru, state_lru_conv,
           ab_w_in, a_conv_w, a_log_decay, a_dt_bias, a_norm_w, b_w_group, b_scale, ab_w_out,
           c_w_in, c_conv_w, c_conv_b, c_w_r, c_b_r, c_w_i, c_b_i, c_lambda, c_w_out,
           ln_mix_g, ln_mix_b, ln_ffn_g, ln_ffn_b, router_w, router_bias,
           moe_w_gate, moe_w_up, moe_w_down):
    bp, tp, d = x_prompt.shape
    bs, ts, _ = x_sample.shape
    n_p, n_s = bp * tp, bs * ts
    n = n_p + n_s
    tm = TOKEN_TILE
    tmix = MIX_TILE
    assert n_p % tmix == 0 and n_s % tmix == 0 and tmix % tm == 0

    xin = [(x_prompt.reshape(n_p, d), 0), (x_sample.reshape(n_s, d), 0)]
    groups = (
        dict(batch=bp, seq=tp, hist_valid=0, fresh=True,
             ev=dict(nb_blk=1, rows=min(tp, 512)), od=dict(nb_blk=min(bp, 8), rows=min(tp, 128))),
        dict(batch=bs, seq=ts, hist_valid=B_HIST, fresh=False,
             ev=dict(nb_blk=min(bs, 8), rows=ts), od=dict(nb_blk=min(bs, 32), rows=ts)),
    )
    rwt = router_w.T
    rb = router_bias.reshape(N_EXPERTS, 1)
    row = lambda v: v.reshape(1, -1)

    new = {k: ([], []) for k in ("dconv", "pool", "lru", "lconv")}
    delta_new = [None, None]
    for layer in range(DEPTH):
        j = layer // 2
        mixed = []
        if layer % 2 == 0:
            w = ab_w_in[j]
            c1 = EV_Z0 + A_V
            w_perm = jnp.concatenate(
                [w[:, :c1], w[:, c1 + 2 * A_HEADS:], w[:, c1:c1 + 2 * A_HEADS],
                 jnp.zeros((d, LANES - 2 * A_HEADS), w.dtype)], axis=1).astype(MM)
            gvec = jnp.zeros((2, LANES), F32)
            gvec = gvec.at[0, A_HEADS:2 * A_HEADS].set(a_log_decay[j])
            gvec = gvec.at[1, A_HEADS:2 * A_HEADS].set(a_dt_bias[j])
            for gi, g in enumerate(groups):
                b_, t_ = g["batch"], g["seq"]
                proj = _proj(xin[gi][0], w_perm, xin[gi][1], b_ * t_, tmix)
                p3 = proj.reshape(b_, t_, EV_W)
                if g["fresh"]:
                    dconv = jnp.zeros((b_, 3, A_CONV_CH), F32)
                    delta_all = jnp.zeros((1, b_, A_HEADS, A_DK, A_DV), F32)
                    pool = jnp.zeros((b_, B_HIST, B_WIDTH), F32)
                else:
                    dconv, delta_all, pool = state_delta_conv[j], state_delta, state_pool[j]
                o, delta_new[gi] = _even_mixer(
                    proj, _pad_hist(dconv, SUBLANES), delta_all, j, _pad_hist(pool, 2 * SUBLANES),
                    a_conv_w[j], gvec, row(a_norm_w[j]), b_w_group[j].astype(MM), row(b_scale[j]),
                    delta_new[gi], batch=b_, seq=t_, hist_valid=g["hist_valid"], **g["ev"])
                mixed.append(o)
                new["dconv"][gi].append(_new_hist(dconv, p3, 0, A_CONV_CH, 3))
                new["pool"][gi].append(_new_hist(pool, p3, EV_U0, EV_U0 + B_WIDTH, B_HIST))
            w_out = ab_w_out[j].astype(MM)
        else:
            w_mm = c_w_in[j].astype(MM)
            for gi, g in enumerate(groups):
                b_, t_ = g["batch"], g["seq"]
                p3 = _proj(xin[gi][0], w_mm, xin[gi][1], b_ * t_, tmix).reshape(b_, t_, 2 * C_WIDTH)
                if g["fresh"]:
                    lconv = jnp.zeros((b_, 3, C_WIDTH), F32)
                    lru = jnp.zeros((b_, C_WIDTH), F32)
                else:
                    lconv, lru = state_lru_conv[j], state_lru[j]
                o3, h_last = _odd_mixer(
                    p3, _pad_hist(lconv, SUBLANES), lru, c_conv_w[j], row(c_conv_b[j]),
                    c_w_r[j].astype(MM), row(c_b_r[j]), c_w_i[j].astype(MM), row(c_b_i[j]),
                    row(c_lambda[j]), **g["od"])
                mixed.append(o3.reshape(b_ * t_, C_WIDTH))
                new["lru"][gi].append(h_last)
                new["lconv"][gi].append(_new_hist(lconv, p3, C_WIDTH, 2 * C_WIDTH, 3))
            w_out = c_w_out[j].astype(MM)

        x1e, key, cnt = _mix_out(mixed[0], mixed[1], w_out, xin[0][0], xin[0][1], xin[1][0], xin[1][1],
                                 row(ln_mix_g[layer]), row(ln_mix_b[layer]), rwt, rb, tmix)
        x = _moe(x1e, key, cnt, row(ln_ffn_g[layer]), row(ln_ffn_b[layer]),
                 moe_w_gate, moe_w_up, moe_w_down, layer, tm)
        xin = [(x, 0), (x, n_p)]

    stack = lambda key, gi: jnp.stack(new[key][gi])
    return (x[:n_p].reshape(bp, tp, d), x[n_p:n].reshape(bs, ts, d),
            delta_new[0], stack("dconv", 0), stack("pool", 0), stack("lru", 0), stack("lconv", 0),
            delta_new[1], stack("dconv", 1), stack("pool", 1), stack("lru", 1), stack("lconv", 1))
```
